```python
import math
import jax, jax.numpy as jnp
from jax import lax
import numpy as np

D_MODEL = 1024
BATCH = 4
SEQ = 4096
DEPTH = 2

CHUNK = 64
Q_BLOCK = 128
PLE_DIM = 256
NORM_EPS = 1e-6
CONV_WIDTH = 256
CONV_K = 3
RWKV_HEADS = 4
RWKV_HEAD_DIM = 64
RWKV_WIDTH = RWKV_HEADS * RWKV_HEAD_DIM
DECAY_LORA = 64
ICLR_LORA = 64
DECAY_SCALE = math.exp(-0.5)
GN_EPS = 64e-5
RWKV_SHIFT_WIDTH = 3 * RWKV_WIDTH + DECAY_LORA + ICLR_LORA
MLA_HEADS = 4
QK_NOPE_DIM = 128
QK_ROPE_DIM = 64
V_HEAD_DIM = 128
Q_LORA_RANK = 384
KV_LORA_RANK = 256
MLA_WIDTH = MLA_HEADS * V_HEAD_DIM
ROPE_THETA = 10000.0
D_MIX = CONV_WIDTH + RWKV_WIDTH + MLA_WIDTH
IN_SPLITS = (CONV_WIDTH, CONV_WIDTH, CONV_WIDTH, CONV_WIDTH,
             RWKV_SHIFT_WIDTH, RWKV_WIDTH,
             Q_LORA_RANK, KV_LORA_RANK, QK_ROPE_DIM, MLA_WIDTH)
D_IN = sum(IN_SPLITS)

kernel_name = "hybrid_conv_rwkv7_mla_parallel_heads"


def _split(z, sizes):
    idx = [int(i) for i in np.cumsum(sizes)[:-1]]
    return jnp.split(z, idx, axis=-1)


def rmsnorm(x, g):
    xf = x.astype(jnp.float32)
    y = xf * lax.rsqrt(jnp.mean(xf * xf, axis=-1, keepdims=True) + NORM_EPS)
    return (y * g.astype(jnp.float32)).astype(x.dtype)


def rope(x, cos, sin):
    x1, x2 = jnp.split(x, 2, axis=-1)
    return jnp.concatenate([x1 * cos - x2 * sin, x2 * cos + x1 * sin], axis=-1)


def conv_branch(c_b, c_c, c_h, c_g, conv_w):
    u = c_c * c_h
    u = lax.conv_general_dilated(
        u, conv_w[:, None, :].astype(u.dtype), window_strides=(1,),
        padding=[(CONV_K - 1, 0)], dimension_numbers=("NWC", "WIO", "NWC"),
        feature_group_count=CONV_WIDTH)
    return c_b * u * jax.nn.silu(c_g)


def _rwkv_step(state, inp):
    r, w, k, v, za, zb = inp
    sa = jnp.einsum("bhvk,bhk->bhv", state, za)
    state = (state * w[:, :, None, :] + sa[..., None] * zb[:, :, None, :]
             + v[..., None] * k[:, :, None, :])
    y = jnp.einsum("bhvk,bhk->bhv", state, r)
    return state, y


def rwkv_branch(zs, g, mu, w0, w2, a0, a2, k_k, k_a, r_k, gn_w, gn_b):
    B, S, _ = zs.shape
    prev = jnp.pad(zs, ((0, 0), (1, 0), (0, 0)))[:, :-1]
    zs = zs + (prev - zs) * mu
    r, k, v, wd, ad = _split(zs, (RWKV_WIDTH, RWKV_WIDTH, RWKV_WIDTH, DECAY_LORA, ICLR_LORA))
    w = jnp.exp(-DECAY_SCALE * jax.nn.sigmoid(w0 + jnp.tanh(wd) @ w2))
    a = jax.nn.sigmoid(a0 + ad @ a2)
    hs = (B, S, RWKV_HEADS, RWKV_HEAD_DIM)
    r, k, v, w, a = (t.reshape(hs) for t in (r, k, v, w, a))
    kk = (k * k_k.reshape(RWKV_HEADS, RWKV_HEAD_DIM)).astype(jnp.float32)
    kk = kk * lax.rsqrt(jnp.sum(kk * kk, axis=-1, keepdims=True) + 1e-12)
    kk = kk.astype(k.dtype)
    k = k * (1.0 + (a - 1.0) * k_a.reshape(RWKV_HEADS, RWKV_HEAD_DIM))
    za = -kk
    zb = kk * a
    xs = tuple(jnp.swapaxes(t, 0, 1).astype(jnp.float32) for t in (r, w, k, v, za, zb))
    s0 = jnp.zeros((B, RWKV_HEADS, RWKV_HEAD_DIM, RWKV_HEAD_DIM), jnp.float32)
    _, y = lax.scan(_rwkv_step, s0, xs)
    y = jnp.swapaxes(y, 0, 1)
    mean = jnp.mean(y, axis=-1, keepdims=True)
    var = jnp.mean(jnp.square(y - mean), axis=-1, keepdims=True)
    y = (y - mean) * lax.rsqrt(var + GN_EPS)
    y = (y * gn_w.reshape(RWKV_HEADS, RWKV_HEAD_DIM).astype(jnp.float32)
         + gn_b.reshape(RWKV_HEADS, RWKV_HEAD_DIM).astype(jnp.float32)).astype(v.dtype)
    bonus = jnp.sum(r * k * r_k, axis=-1, keepdims=True) * v
    y = (y + bonus).reshape(B, S, RWKV_WIDTH)
    return y * jax.nn.silu(g)


def chunk_causal_attention(q, k, v):
    B, S, H, Dk = q.shape
    Dv = v.shape[-1]
    nq = S // Q_BLOCK
    qb = jnp.moveaxis(q.reshape(B, nq, Q_BLOCK, H, Dk), 1, 0)
    key_chunk = jnp.arange(S) // CHUNK
    q_chunk = key_chunk.reshape(nq, Q_BLOCK)
    scale = 1.0 / math.sqrt(Dk)

    def one_block(args):
        qi, qc = args
        s = jnp.einsum("bqhd,bkhd->bhqk", qi, k).astype(jnp.float32) * scale
        mask = key_chunk[None, :] <= qc[:, None]
        s = jnp.where(mask, s, jnp.finfo(jnp.float32).min)
        pr = jax.nn.softmax(s, axis=-1).astype(v.dtype)
        return jnp.einsum("bhqk,bkhd->bqhd", pr, v)

    o = lax.map(one_block, (qb, q_chunk))
    return jnp.moveaxis(o, 0, 1).reshape(B, S, H * Dv)


def mla_branch(qa, kva, krope, g, q_norm_g, w_qb, kv_norm_g, w_kvb, cos, sin):
    B, S, _ = qa.shape
    q = (rmsnorm(qa, q_norm_g) @ w_qb).reshape(B, S, MLA_HEADS, QK_NOPE_DIM + QK_ROPE_DIM)
    q_nope, q_rope = q[..., :QK_NOPE_DIM], q[..., QK_NOPE_DIM:]
    q_rope = rope(q_rope, cos[:, :, None, :], sin[:, :, None, :])
    kv = (rmsnorm(kva, kv_norm_g) @ w_kvb).reshape(B, S, MLA_HEADS, QK_NOPE_DIM + V_HEAD_DIM)
    k_nope, v = kv[..., :QK_NOPE_DIM], kv[..., QK_NOPE_DIM:]
    k_rope = rope(krope, cos, sin)
    q = jnp.concatenate([q_nope, q_rope], axis=-1)
    k = jnp.concatenate(
        [k_nope, jnp.broadcast_to(k_rope[:, :, None, :], (B, S, MLA_HEADS, QK_ROPE_DIM))], axis=-1)
    o = chunk_causal_attention(q, k, v)
    return o * jax.nn.silu(g)


def setup_inputs(seed: int = 0) -> dict:
    key = jax.random.key(seed)
    ks = jax.random.split(key, 32)
    f32 = jnp.float32
    nrm = lambda k, shape, s: jax.random.normal(k, shape, f32) * s
    L = DEPTH
    offsets = jax.random.randint(ks[2], (BATCH, 1), 0, 64) * CHUNK
    positions = (offsets + jnp.arange(SEQ)[None, :]).astype(jnp.int32)
    return {
        "x": nrm(ks[0], (BATCH, SEQ, D_MODEL), 1.0),
        "p": nrm(ks[1], (DEPTH, BATCH, SEQ, PLE_DIM), 1.0),
        "positions": positions,
        "norm_mix_g": 1.0 + nrm(ks[3], (L, D_MODEL), 0.02),
        "w_in": nrm(ks[4], (L, D_MODEL, D_IN), D_MODEL ** -0.5),
        "conv_w": nrm(ks[5], (L, CONV_K, CONV_WIDTH), CONV_K ** -0.5),
        "rwkv_mu": jax.random.uniform(ks[6], (L, RWKV_SHIFT_WIDTH), f32),
        "rwkv_w0": nrm(ks[7], (L, RWKV_WIDTH), 1.0),
        "rwkv_w2": nrm(ks[8], (L, DECAY_LORA, RWKV_WIDTH), 0.1 * DECAY_LORA ** -0.5),
        "rwkv_a0": nrm(ks[9], (L, RWKV_WIDTH), 0.1),
        "rwkv_a2": nrm(ks[10], (L, ICLR_LORA, RWKV_WIDTH), 0.1 * ICLR_LORA ** -0.5),
        "rwkv_kk": 0.85 + nrm(ks[11], (L, RWKV_WIDTH), 0.05),
        "rwkv_ka": 1.0 + nrm(ks[12], (L, RWKV_WIDTH), 0.05),
        "rwkv_rk": nrm(ks[13], (L, RWKV_HEADS, RWKV_HEAD_DIM), 0.1),
        "rwkv_gn_w": 1.0 + nrm(ks[14], (L, RWKV_WIDTH), 0.02),
        "rwkv_gn_b": nrm(ks[15], (L, RWKV_WIDTH), 0.02),
        "mla_q_norm_g": 1.0 + nrm(ks[16], (L, Q_LORA_RANK), 0.02),
        "mla_w_qb": nrm(ks[17], (L, Q_LORA_RANK, MLA_HEADS * (QK_NOPE_DIM + QK_ROPE_DIM)), Q_LORA_RANK ** -0.5),
        "mla_kv_norm_g": 1.0 + nrm(ks[18], (L, KV_LORA_RANK), 0.02),
        "mla_w_kvb": nrm(ks[19], (L, KV_LORA_RANK, MLA_HEADS * (QK_NOPE_DIM + V_HEAD_DIM)), KV_LORA_RANK ** -0.5),
        "w_out": nrm(ks[20], (L, D_MIX, D_MODEL), 0.5 * D_MIX ** -0.5),
        "ple_w": nrm(ks[21], (L, PLE_DIM, D_MODEL), 0.5 * PLE_DIM ** -0.5),
        "ple_norm_g": 1.0 + nrm(ks[22], (L, D_MODEL), 0.02),
        "ple_gate_w": nrm(ks[23], (L, D_MODEL, D_MODEL), D_MODEL ** -0.5),
        "final_norm_g": 1.0 + nrm(ks[24], (D_MODEL,), 0.02),
    }


def reference(x, p, positions, norm_mix_g, w_in, conv_w, rwkv_mu, rwkv_w0, rwkv_w2,
              rwkv_a0, rwkv_a2, rwkv_kk, rwkv_ka, rwkv_rk, rwkv_gn_w, rwkv_gn_b,
              mla_q_norm_g, mla_w_qb, mla_kv_norm_g, mla_w_kvb, w_out,
              ple_w, ple_norm_g, ple_gate_w, final_norm_g):
    inv_freq = 1.0 / (ROPE_THETA ** (jnp.arange(0, QK_ROPE_DIM, 2, dtype=jnp.float32) / QK_ROPE_DIM))
    ang = positions.astype(jnp.float32)[..., None] * inv_freq
    cos = jnp.cos(ang).astype(x.dtype)
    sin = jnp.sin(ang).astype(x.dtype)

    h = x
    for i in range(DEPTH):
        u = rmsnorm(h, norm_mix_g[i])
        z = u @ w_in[i]
        (c_b, c_c, c_h, c_g, r_shift, r_g, m_qa, m_kva, m_krope, m_g) = _split(z, IN_SPLITS)
        y_conv = conv_branch(c_b, c_c, c_h, c_g, conv_w[i])
        y_rwkv = rwkv_branch(r_shift, r_g, rwkv_mu[i], rwkv_w0[i], rwkv_w2[i], rwkv_a0[i],
                             rwkv_a2[i], rwkv_kk[i], rwkv_ka[i], rwkv_rk[i],
                             rwkv_gn_w[i], rwkv_gn_b[i])
        y_mla = mla_branch(m_qa, m_kva, m_krope, m_g, mla_q_norm_g[i], mla_w_qb[i],
                           mla_kv_norm_g[i], mla_w_kvb[i], cos, sin)
        y = jnp.concatenate([y_conv, y_rwkv, y_mla], axis=-1) @ w_out[i]
        h = h + y
        gate = jax.nn.sigmoid(rmsnorm(h, ple_norm_g[i]) @ ple_gate_w[i])
        h = h + (p[i] @ ple_w[i]) * gate
    return rmsnorm(h, final_norm_g)
```

```python
import functools
import math

import jax
import jax.numpy as jnp
import numpy as np
from jax import lax
from jax.experimental import pallas as pl
from jax.experimental.pallas import tpu as pltpu

F32 = jnp.float32
BF16 = jnp.bfloat16

D_MODEL = 1024
CHUNK = 64
PLE_DIM = 256
NORM_EPS = 1e-6
CONV_WIDTH = 256
RWKV_HEADS = 4
RWKV_HEAD_DIM = 64
RWKV_WIDTH = RWKV_HEADS * RWKV_HEAD_DIM
LORA = 64
DECAY_SCALE = math.exp(-0.5)
GN_EPS = 64e-5
MLA_HEADS = 4
QK_NOPE_DIM = 128
QK_ROPE_DIM = 64
V_HEAD_DIM = 128
Q_LORA_RANK = 384
KV_LORA_RANK = 256
MLA_WIDTH = MLA_HEADS * V_HEAD_DIM
ROPE_THETA = 10000.0
D_MIX = CONV_WIDTH + RWKV_WIDTH + MLA_WIDTH

C_CONV = 0
C_RKV = 1024
C_RG = 1792
C_MG = 2048
C_QA = 2560
C_KVA = 2944
C_WA = 3200
C_KR = 3328
C_KRS = 3456
D_IN_P = 3584
QK_PAD = 256

LANE = 128
HALO = 16
TM = 512
TQ = 512
RWKV_CHUNKS_PER_STEP = 2
VMEM_LIMIT = 56 * 1024 * 1024


def _dot(a, b):
    return jnp.dot(a.astype(BF16), b.astype(BF16), preferred_element_type=F32)


def _dot_nt(a, b):
    return lax.dot_general(a.astype(BF16), b.astype(BF16), (((1,), (1,)), ((), ())),
                           preferred_element_type=F32)


def _dot_tn(a, b):
    return lax.dot_general(a.astype(BF16), b.astype(BF16), (((0,), (0,)), ((), ())),
                           preferred_element_type=F32)


def _split3(x):
    hi = x.astype(BF16)
    r1 = x - hi.astype(F32)
    mid = r1.astype(BF16)
    lo = (r1 - mid.astype(F32)).astype(BF16)
    return hi, mid, lo


def _dot_mask_lhs(m01, x):
    hi, mid, lo = _split3(x)
    m = m01.astype(BF16)
    acc = jnp.dot(m, lo, preferred_element_type=F32)
    acc = acc + jnp.dot(m, mid, preferred_element_type=F32)
    return acc + jnp.dot(m, hi, preferred_element_type=F32)


def _rms(x, g):
    return x * lax.rsqrt(jnp.mean(x * x, axis=-1, keepdims=True) + NORM_EPS) * g


def _silu(x):
    return x * jax.nn.sigmoid(x)


def _proj_kernel(h_ref, halo_ref, ng_ref, win_ref, cw_ref, mu_rkv_ref, mu_wa_ref, w0_ref, a0_ref,
                 wl_ref, kk_ref, ka_ref, seg_ref, qg_ref, wq_ref, wqs_ref, kvg_ref, wkv_ref,
                 ck_ref, sk_ref,
                 yconv_ref, r_ref, wlog_ref, k_ref, v_ref, kkn_ref, zb_ref, gr_ref,
                 q_ref, kq_ref, vq_ref, gm_ref,
                 conv_scr, rkv_scr, wa_scr, *, tiles_per_seq):
    i = pl.program_id(0)
    tm = h_ref.shape[0]
    hx = jnp.concatenate([halo_ref[...], h_ref[...]], axis=0)
    u = _rms(hx, ng_ref[...]).astype(BF16)
    z = jnp.dot(u, win_ref[...], preferred_element_type=F32)

    row = lax.broadcasted_iota(jnp.int32, (HALO + tm, 1), 0)
    keep = row >= jnp.where(i % tiles_per_seq == 0, HALO, 0)

    zc = z[HALO:]
    conv_scr[...] = jnp.where(keep, z[:, C_CONV + 256:C_CONV + 512] * z[:, C_CONV + 512:C_CONV + 768], 0.0)
    cw = cw_ref[...]
    conv = (conv_scr[pl.ds(HALO - 2, tm), :] * cw[0:1, :]
            + conv_scr[pl.ds(HALO - 1, tm), :] * cw[1:2, :]
            + conv_scr[pl.ds(HALO, tm), :] * cw[2:3, :])
    yconv_ref[...] = (zc[:, C_CONV:C_CONV + 256] * conv * _silu(zc[:, C_CONV + 768:C_CONV + 1024])).astype(BF16)

    rkv_scr[...] = jnp.where(keep, z[:, C_RKV:C_RKV + 768], 0.0)
    wa_scr[...] = jnp.where(keep, z[:, C_WA:C_WA + 128], 0.0)
    cur = rkv_scr[pl.ds(HALO, tm), :]
    rkv = cur + (rkv_scr[pl.ds(HALO - 1, tm), :] - cur) * mu_rkv_ref[...]
    cur = wa_scr[pl.ds(HALO, tm), :]
    wa = cur + (wa_scr[pl.ds(HALO - 1, tm), :] - cur) * mu_wa_ref[...]
    r = rkv[:, 0:256]
    k = rkv[:, 256:512]
    v = rkv[:, 512:768]
    lane = lax.broadcasted_iota(jnp.int32, wa.shape, 1)
    lora_in = jnp.where(lane < LORA, jnp.tanh(wa), wa)
    lora = jnp.dot(lora_in.astype(BF16), wl_ref[...], preferred_element_type=F32)
    wlog = -DECAY_SCALE * jax.nn.sigmoid(w0_ref[...] + lora[:, :RWKV_WIDTH])
    a = jax.nn.sigmoid(a0_ref[...] + lora[:, RWKV_WIDTH:])
    kk = k * kk_ref[...]
    hi = (kk * kk).astype(BF16)
    lo = (kk * kk - hi.astype(F32)).astype(BF16)
    ss = (jnp.dot(hi, seg_ref[...], preferred_element_type=F32)
          + jnp.dot(lo, seg_ref[...], preferred_element_type=F32))
    kk = kk * lax.rsqrt(ss + 1e-12)
    kmod = k * (1.0 + (a - 1.0) * ka_ref[...])
    zb = kk * a
    for hd in range(RWKV_HEADS):
        sl = slice(hd * RWKV_HEAD_DIM, (hd + 1) * RWKV_HEAD_DIM)
        r_ref[0, hd] = r[:, sl]
        wlog_ref[0, hd] = wlog[:, sl]
        k_ref[0, hd] = kmod[:, sl]
        v_ref[0, hd] = v[:, sl]
        kkn_ref[0, hd] = kk[:, sl]
        zb_ref[0, hd] = zb[:, sl]
    gr_ref[...] = _silu(zc[:, C_RG:C_RG + 256])

    ck = ck_ref[...]
    sk = sk_ref[...]
    qn = _rms(zc[:, C_QA:C_QA + Q_LORA_RANK], qg_ref[...]).astype(BF16)
    qm = jnp.dot(qn, wq_ref[...], preferred_element_type=F32)
    qs = jnp.dot(qn, wqs_ref[...], preferred_element_type=F32)
    scale = 1.0 / math.sqrt(QK_NOPE_DIM + QK_ROPE_DIM)
    kvn = _rms(zc[:, C_KVA:C_KVA + KV_LORA_RANK], kvg_ref[...]).astype(BF16)
    kv = jnp.dot(kvn, wkv_ref[...], preferred_element_type=F32)
    kr = zc[:, C_KR:C_KR + LANE] * ck + zc[:, C_KRS:C_KRS + LANE] * sk
    for hd in range(MLA_HEADS):
        o = hd * QK_PAD
        qr = qm[:, o + LANE:o + QK_PAD] * ck + qs[:, hd * LANE:(hd + 1) * LANE] * sk
        q_ref[:, o:o + LANE] = (qm[:, o:o + LANE] * scale).astype(BF16)
        q_ref[:, o + LANE:o + QK_PAD] = (qr * scale).astype(BF16)
        kq_ref[:, o:o + LANE] = kv[:, hd * LANE:(hd + 1) * LANE].astype(BF16)
        kq_ref[:, o + LANE:o + QK_PAD] = kr.astype(BF16)
    vq_ref[...] = kv[:, MLA_HEADS * QK_NOPE_DIM:].astype(BF16)
    gm_ref[...] = _silu(zc[:, C_MG:C_MG + MLA_WIDTH])


def _const_spec(shape):
    return pl.BlockSpec(shape, lambda i: (0,) * len(shape))


def _proj_call(h, lw, ck, sk, batch, seq):
    t = h.shape[0]
    tiles_per_seq = seq // TM
    row = lambda w: pl.BlockSpec((TM, w), lambda i: (i, 0))
    hm = pl.BlockSpec((1, RWKV_HEADS, TM, RWKV_HEAD_DIM),
                      lambda i: (i // tiles_per_seq, 0, i % tiles_per_seq, 0))
    halo = pl.BlockSpec((HALO, D_MODEL), lambda i: (jnp.maximum(i * (TM // HALO) - 1, 0), 0))
    consts = [lw["ng"], lw["win"], lw["cw"], lw["mu_rkv"], lw["mu_wa"], lw["w0"], lw["a0"], lw["wl"],
              lw["kk"], lw["ka"], lw["seg"], lw["qg"], lw["wq"], lw["wqs"], lw["kvg"], lw["wkv"]]
    hm_shape = jax.ShapeDtypeStruct((batch, RWKV_HEADS, seq, RWKV_HEAD_DIM), F32)
    out_shape = [jax.ShapeDtypeStruct((t, CONV_WIDTH), BF16)] + [hm_shape] * 6 + [
        jax.ShapeDtypeStruct((t, RWKV_WIDTH), F32),
        jax.ShapeDtypeStruct((t, MLA_HEADS * QK_PAD), BF16),
        jax.ShapeDtypeStruct((t, MLA_HEADS * QK_PAD), BF16),
        jax.ShapeDtypeStruct((t, MLA_WIDTH), BF16),
        jax.ShapeDtypeStruct((t, MLA_WIDTH), F32),
    ]
    out_specs = [row(CONV_WIDTH)] + [hm] * 6 + [row(RWKV_WIDTH), row(MLA_HEADS * QK_PAD),
                                                  row(MLA_HEADS * QK_PAD), row(MLA_WIDTH), row(MLA_WIDTH)]
    return pl.pallas_call(
        functools.partial(_proj_kernel, tiles_per_seq=tiles_per_seq),
        grid=(t // TM,),
        in_specs=[row(D_MODEL), halo] + [_const_spec(c.shape) for c in consts] + [row(LANE), row(LANE)],
        out_specs=out_specs,
        out_shape=out_shape,
        scratch_shapes=[pltpu.VMEM((HALO + TM, CONV_WIDTH), F32),
                        pltpu.VMEM((HALO + TM, 3 * RWKV_WIDTH), F32),
                        pltpu.VMEM((HALO + TM, 2 * LORA), F32)],
        compiler_params=pltpu.CompilerParams(dimension_semantics=("arbitrary",),
                                             vmem_limit_bytes=VMEM_LIMIT),
        name="proj",
    )(h, h, *consts, ck, sk)


def _rwkv_chunk(r, wl, k, v, kk, zb, state, tri_incl, tri_strict, eye):
    cw = _dot_mask_lhs(tri_incl, wl)
    cw_end = cw[CHUNK - 1:CHUNK, :]
    r_t = r * jnp.exp(cw)
    a_t = -kk * jnp.exp(cw - wl)
    inv = jnp.exp(-cw)
    b_t = zb * inv
    k_t = k * inv
    to_end = jnp.exp(cw_end - cw)
    bh = zb * to_end
    kh = k * to_end

    lhs = jnp.concatenate([a_t, r_t], axis=0)
    ab = _dot_nt(lhs, b_t)
    ak = _dot_nt(lhs, k_t)
    aab = jnp.where(tri_strict, ab[:CHUNK], 0.0)
    aak = jnp.where(tri_strict, ak[:CHUNK], 0.0)
    arb = jnp.where(tri_incl, ab[CHUNK:], 0.0)
    ark = jnp.where(tri_incl, ak[CHUNK:], 0.0)

    x = jnp.concatenate([a_t, _dot(aak, v)], axis=1)
    p = aab
    for it in range(6):
        x = x + _dot(p, x)
        if it < 5:
            p = _dot(p, p)
    at = x[:, :RWKV_HEAD_DIM]
    u = x[:, RWKV_HEAD_DIM:]

    qp = r_t + _dot(arb, at)
    y = _dot(qp, state) + _dot(arb, u) + _dot(ark, v)
    mc = jnp.where(eye, jnp.exp(cw_end), 0.0) + _dot_tn(bh, at)
    nc = _dot_tn(jnp.concatenate([bh, kh], axis=0), jnp.concatenate([u, v], axis=0))
    return y, _dot(mc, state) + nc


def _rwkv_kernel(r_ref, wlog_ref, k_ref, v_ref, kk_ref, zb_ref, gate_ref, rk_ref, gnw_ref, gnb_ref,
                 y_ref, state_ref):
    @pl.when(pl.program_id(1) == 0)
    def _():
        state_ref[...] = jnp.zeros_like(state_ref)

    ri = lax.broadcasted_iota(jnp.int32, (CHUNK, CHUNK), 0)
    ci = lax.broadcasted_iota(jnp.int32, (CHUNK, CHUNK), 1)
    tri_incl = ri >= ci
    tri_strict = ri > ci
    eye = ri == ci
    for c in range(RWKV_CHUNKS_PER_STEP):
        rows = pl.ds(c * CHUNK, CHUNK)
        outs = []
        for hd in range(RWKV_HEADS):
            r = r_ref[0, hd, rows, :]
            k = k_ref[0, hd, rows, :]
            v = v_ref[0, hd, rows, :]
            y, new_state = _rwkv_chunk(r, wlog_ref[0, hd, rows, :], k, v, kk_ref[0, hd, rows, :],
                                       zb_ref[0, hd, rows, :], state_ref[hd], tri_incl, tri_strict, eye)
            state_ref[hd] = new_state
            mean = jnp.mean(y, axis=-1, keepdims=True)
            yc = y - mean
            var = jnp.mean(yc * yc, axis=-1, keepdims=True)
            yn = yc * lax.rsqrt(var + GN_EPS) * gnw_ref[hd:hd + 1, :] + gnb_ref[hd:hd + 1, :]
            bonus = jnp.sum(r * k * rk_ref[hd:hd + 1, :], axis=-1, keepdims=True) * v
            outs.append(yn + bonus)
        y_ref[0, rows, :] = (jnp.concatenate(outs, axis=1) * gate_ref[0, rows, :]).astype(BF16)


def _rwkv_call(hm, gate, lw, batch, seq):
    tc = RWKV_CHUNKS_PER_STEP * CHUNK
    hm_spec = pl.BlockSpec((1, RWKV_HEADS, tc, RWKV_HEAD_DIM), lambda b, c: (b, 0, c, 0))
    nat = pl.BlockSpec((1, tc, RWKV_WIDTH), lambda b, c: (b, c, 0))
    par = pl.BlockSpec((RWKV_HEADS, RWKV_HEAD_DIM), lambda b, c: (0, 0))
    return pl.pallas_call(
        _rwkv_kernel,
        grid=(batch, seq // tc),
        in_specs=[hm_spec] * 6 + [nat, par, par, par],
        out_specs=nat,
        out_shape=jax.ShapeDtypeStruct((batch, seq, RWKV_WIDTH), BF16),
        scratch_shapes=[pltpu.VMEM((RWKV_HEADS, RWKV_HEAD_DIM, RWKV_HEAD_DIM), F32)],
        compiler_params=pltpu.CompilerParams(dimension_semantics=("arbitrary", "arbitrary"),
                                             vmem_limit_bytes=VMEM_LIMIT),
        name="rwkv",
    )(*hm, gate.reshape(batch, seq, RWKV_WIDTH), lw["rk"], lw["gnw"], lw["gnb"])


def _attn_kernel(q_ref, k_ref, v_ref, g_ref, o_ref):
    i = pl.program_id(2)
    tq = q_ref.shape[1]
    q = q_ref[0]

    def tile(j, carry, masked):
        m, l, acc = carry
        rows = pl.ds(pl.multiple_of(j * tq, tq), tq)
        s = lax.dot_general(q, k_ref[0, rows, :], (((1,), (1,)), ((), ())), preferred_element_type=F32)
        if masked:
            qc = lax.broadcasted_iota(jnp.int32, (tq, tq), 0) // CHUNK
            kc = lax.broadcasted_iota(jnp.int32, (tq, tq), 1) // CHUNK
            s = jnp.where(kc <= qc, s, -1e30)
        m_new = jnp.maximum(m, jnp.max(s, axis=-1, keepdims=True))
        alpha = jnp.exp(m - m_new)
        p = jnp.exp(s - m_new)
        l = alpha * l + jnp.sum(p, axis=-1, keepdims=True)
        acc = alpha * acc + jnp.dot(p.astype(BF16), v_ref[0, rows, :], preferred_element_type=F32)
        return m_new, l, acc

    init = (jnp.full((tq, 1), -jnp.inf, F32), jnp.zeros((tq, 1), F32), jnp.zeros((tq, V_HEAD_DIM), F32))
    carry = lax.fori_loop(0, i, lambda j, c: tile(j, c, False), init)
    _, l, acc = tile(i, carry, True)
    o_ref[0] = (acc / l * g_ref[0]).astype(BF16)


def _attn_call(q, k, v, gate, batch, seq):
    q = q.reshape(batch, seq, MLA_HEADS * QK_PAD)
    k = k.reshape(batch, seq, MLA_HEADS * QK_PAD)
    v = v.reshape(batch, seq, MLA_WIDTH)
    gate = gate.reshape(batch, seq, MLA_WIDTH)
    return pl.pallas_call(
        _attn_kernel,
        grid=(batch, MLA_HEADS, seq // TQ),
        in_specs=[pl.BlockSpec((1, TQ, QK_PAD), lambda b, h, i: (b, i, h)),
                  pl.BlockSpec((1, seq, QK_PAD), lambda b, h, i: (b, 0, h)),
                  pl.BlockSpec((1, seq, V_HEAD_DIM), lambda b, h, i: (b, 0, h)),
                  pl.BlockSpec((1, TQ, V_HEAD_DIM), lambda b, h, i: (b, i, h))],
        out_specs=pl.BlockSpec((1, TQ, V_HEAD_DIM), lambda b, h, i: (b, i, h)),
        out_shape=jax.ShapeDtypeStruct((batch, seq, MLA_WIDTH), BF16),
        compiler_params=pltpu.CompilerParams(
            dimension_semantics=("arbitrary", "arbitrary", "arbitrary"), vmem_limit_bytes=VMEM_LIMIT),
        name="attn",
    )(q, k, v, gate)


def _out_kernel(h_ref, yc_ref, yr_ref, ym_ref, wo_ref, p_ref, wple_ref, png_ref, wpg_ref, fg_ref,
                o_ref, *, final):
    ycat = jnp.concatenate([yc_ref[...], yr_ref[...], ym_ref[...]], axis=-1)
    h = h_ref[...] + jnp.dot(ycat, wo_ref[...], preferred_element_type=F32)
    gate = jax.nn.sigmoid(jnp.dot(_rms(h, png_ref[...]).astype(BF16), wpg_ref[...],
                                  preferred_element_type=F32))
    h = h + jnp.dot(p_ref[...].astype(BF16), wple_ref[...], preferred_element_type=F32) * gate
    if final:
        h = _rms(h, fg_ref[...])
    o_ref[...] = h


def _out_call(h, yc, yr, ym, p, layer, lw, final_g, final):
    t = h.shape[0]
    row = lambda w: pl.BlockSpec((TM, w), lambda i: (i, 0))
    consts_a = [lw["wo"]]
    consts_b = [lw["wple"], lw["png"], lw["wpg"], final_g]
    return pl.pallas_call(
        functools.partial(_out_kernel, final=final),
        grid=(t // TM,),
        in_specs=[row(D_MODEL), row(CONV_WIDTH), row(RWKV_WIDTH), row(MLA_WIDTH)]
        + [_const_spec(c.shape) for c in consts_a]
        + [pl.BlockSpec((None, TM, PLE_DIM), lambda i: (layer, i, 0))]
        + [_const_spec(c.shape) for c in consts_b],
        out_specs=row(D_MODEL),
        out_shape=jax.ShapeDtypeStruct((t, D_MODEL), F32),
        compiler_params=pltpu.CompilerParams(dimension_semantics=("arbitrary",),
                                             vmem_limit_bytes=VMEM_LIMIT),
        name="out",
    )(h, yc, yr, ym, *consts_a, p, *consts_b)


def _swap_halves(w):
    half = w.shape[-1] // 2
    return jnp.concatenate([w[..., half:], w[..., :half]], axis=-1)


def _prep_layer(i, norm_mix_g, w_in, conv_w, rwkv_mu, rwkv_w0, rwkv_w2, rwkv_a0, rwkv_a2, rwkv_kk,
                rwkv_ka, rwkv_rk, rwkv_gn_w, rwkv_gn_b, mla_q_norm_g, mla_w_qb, mla_kv_norm_g,
                mla_w_kvb, w_out, ple_w, ple_norm_g, ple_gate_w):
    w = w_in[i]
    o_rs = 4 * CONV_WIDTH
    o_rg = o_rs + 3 * RWKV_WIDTH + 2 * LORA
    o_qa = o_rg + RWKV_WIDTH
    o_kva = o_qa + Q_LORA_RANK
    o_kr = o_kva + KV_LORA_RANK
    o_mg = o_kr + QK_ROPE_DIM
    zeros = jnp.zeros((D_MODEL, QK_ROPE_DIM), w.dtype)
    krope = w[:, o_kr:o_mg]
    win = jnp.concatenate([
        w[:, :o_rs], w[:, o_rs:o_rs + 3 * RWKV_WIDTH], w[:, o_rg:o_qa], w[:, o_mg:],
        w[:, o_qa:o_kva], w[:, o_kva:o_kr], w[:, o_rs + 3 * RWKV_WIDTH:o_rg],
        krope, zeros, _swap_halves(krope), zeros], axis=1).astype(BF16)

    wl = jnp.zeros((2 * LORA, 2 * RWKV_WIDTH), F32)
    wl = wl.at[:LORA, :RWKV_WIDTH].set(rwkv_w2[i]).at[LORA:, RWKV_WIDTH:].set(rwkv_a2[i]).astype(BF16)
    head = np.arange(RWKV_WIDTH) // RWKV_HEAD_DIM
    seg = jnp.asarray(head[:, None] == head[None, :], BF16)

    wqb = mla_w_qb[i].reshape(Q_LORA_RANK, MLA_HEADS, QK_NOPE_DIM + QK_ROPE_DIM)
    zq = jnp.zeros((Q_LORA_RANK, MLA_HEADS, QK_ROPE_DIM), F32)
    wq = jnp.concatenate([wqb, zq], axis=-1).reshape(Q_LORA_RANK, MLA_HEADS * QK_PAD).astype(BF16)
    wqs = jnp.concatenate([_swap_halves(wqb[..., QK_NOPE_DIM:]), zq], axis=-1)
    wqs = wqs.reshape(Q_LORA_RANK, MLA_HEADS * LANE).astype(BF16)
    wkvb = mla_w_kvb[i].reshape(KV_LORA_RANK, MLA_HEADS, QK_NOPE_DIM + V_HEAD_DIM)
    wkv = jnp.concatenate([wkvb[..., :QK_NOPE_DIM].reshape(KV_LORA_RANK, -1),
                           wkvb[..., QK_NOPE_DIM:].reshape(KV_LORA_RANK, -1)], axis=1).astype(BF16)
    row = lambda x: x.reshape(1, -1)
    hd = lambda x: x.reshape(RWKV_HEADS, RWKV_HEAD_DIM)
    return dict(
        ng=row(norm_mix_g[i]), win=win, cw=conv_w[i],
        mu_rkv=row(rwkv_mu[i, :3 * RWKV_WIDTH]), mu_wa=row(rwkv_mu[i, 3 * RWKV_WIDTH:]),
        w0=row(rwkv_w0[i]), a0=row(rwkv_a0[i]), wl=wl, kk=row(rwkv_kk[i]), ka=row(rwkv_ka[i]), seg=seg,
        qg=row(mla_q_norm_g[i]), wq=wq, wqs=wqs, kvg=row(mla_kv_norm_g[i]), wkv=wkv,
        rk=hd(rwkv_rk[i]), gnw=hd(rwkv_gn_w[i]), gnb=hd(rwkv_gn_b[i]),
        wo=w_out[i].astype(BF16), wple=ple_w[i].astype(BF16), png=row(ple_norm_g[i]),
        wpg=ple_gate_w[i].astype(BF16))


def kernel(x, p, positions, norm_mix_g, w_in, conv_w, rwkv_mu, rwkv_w0, rwkv_w2, rwkv_a0, rwkv_a2,
           rwkv_kk, rwkv_ka, rwkv_rk, rwkv_gn_w, rwkv_gn_b, mla_q_norm_g, mla_w_qb, mla_kv_norm_g,
           mla_w_kvb, w_out, ple_w, ple_norm_g, ple_gate_w, final_norm_g):
    batch, seq, _ = x.shape
    depth = w_in.shape[0]
    t = batch * seq

    inv_freq = 1.0 / (ROPE_THETA ** (jnp.arange(0, QK_ROPE_DIM, 2, dtype=F32) / QK_ROPE_DIM))
    ang = positions.astype(F32).reshape(t, 1) * inv_freq
    cos, sin = jnp.cos(ang), jnp.sin(ang)
    pad = jnp.zeros((t, LANE - QK_ROPE_DIM), F32)
    ck = jnp.concatenate([cos, cos, pad], axis=1)
    sk = jnp.concatenate([-sin, sin, pad], axis=1)

    h = x.reshape(t, D_MODEL)
    p = p.reshape(depth, t, PLE_DIM)
    final_g = final_norm_g.reshape(1, D_MODEL)
    for i in range(depth):
        lw = _prep_layer(i, norm_mix_g, w_in, conv_w, rwkv_mu, rwkv_w0, rwkv_w2, rwkv_a0, rwkv_a2,
                         rwkv_kk, rwkv_ka, rwkv_rk, rwkv_gn_w, rwkv_gn_b, mla_q_norm_g, mla_w_qb,
                         mla_kv_norm_g, mla_w_kvb, w_out, ple_w, ple_norm_g, ple_gate_w)
        (yconv, r, wlog, k, v, kk, zb, gate_r, q, kq, vq, gate_m) = _proj_call(h, lw, ck, sk, batch, seq)
        y_rwkv = _rwkv_call((r, wlog, k, v, kk, zb), gate_r, lw, batch, seq).reshape(t, RWKV_WIDTH)
        y_mla = _attn_call(q, kq, vq, gate_m, batch, seq).reshape(t, MLA_WIDTH)
        h = _out_call(h, yconv, y_rwkv, y_mla, p, i, lw, final_g, final=(i == depth - 1))
    return h.reshape(batch, seq, D_MODEL)
```

```python
import functools
import math

import jax
import jax.numpy as jnp
import numpy as np
from jax import lax
from jax.experimental import pallas as pl
from jax.experimental.pallas import tpu as pltpu

F32 = jnp.float32
BF16 = jnp.bfloat16

D_MODEL = 1024
CHUNK = 64
PLE_DIM = 256
NORM_EPS = 1e-6
CONV_WIDTH = 256
RWKV_HEADS = 4
RWKV_HEAD_DIM = 64
RWKV_WIDTH = RWKV_HEADS * RWKV_HEAD_DIM
LORA = 64
DECAY_SCALE = math.exp(-0.5)
GN_EPS = 64e-5
MLA_HEADS = 4
QK_NOPE_DIM = 128
QK_ROPE_DIM = 64
V_HEAD_DIM = 128
Q_LORA_RANK = 384
KV_LORA_RANK = 256
MLA_WIDTH = MLA_HEADS * V_HEAD_DIM
ROPE_THETA = 10000.0
D_MIX = CONV_WIDTH + RWKV_WIDTH + MLA_WIDTH

C_CONV = 0
C_RKV = 1024
C_RG = 1792
C_MG = 2048
C_QA = 2560
C_KVA = 2944
C_WA = 3200
C_KR = 3328
C_KRS = 3456
D_IN_P = 3584
QK_PAD = 256

LANE = 128
HALO = 16
TM = 512
TQ = 512
RWKV_CHUNKS_PER_STEP = 8
VMEM_LIMIT = 56 * 1024 * 1024


def _dot(a, b):
    return jnp.dot(a.astype(BF16), b.astype(BF16), preferred_element_type=F32)


def _dot_nt(a, b):
    return lax.dot_general(a.astype(BF16), b.astype(BF16), (((1,), (1,)), ((), ())),
                           preferred_element_type=F32)


def _dot_tn(a, b):
    return lax.dot_general(a.astype(BF16), b.astype(BF16), (((0,), (0,)), ((), ())),
                           preferred_element_type=F32)


def _split2(x):
    hi = x.astype(BF16)
    return hi, (x - hi.astype(F32)).astype(BF16)


def _dot_mask(x, m01, left=False):
    hi, lo = _split2(x)
    if left:
        return jnp.dot(m01, hi, preferred_element_type=F32) + jnp.dot(m01, lo, preferred_element_type=F32)
    return jnp.dot(hi, m01, preferred_element_type=F32) + jnp.dot(lo, m01, preferred_element_type=F32)


def _rms(x, g):
    return x * lax.rsqrt(jnp.mean(x * x, axis=-1, keepdims=True) + NORM_EPS) * g


def _silu(x):
    return x * jax.nn.sigmoid(x)


def _proj_kernel(h_ref, halo_ref, ng_ref, win_ref, cw_ref, mu_rkv_ref, mu_wa_ref, w0_ref, a0_ref,
                 wl_ref, kk_ref, ka_ref, rk_ref, seg_ref, tril_ref, triu_ref, csum_ref,
                 qg_ref, wq_ref, wqs_ref, kvg_ref, wkv_ref, ck_ref, sk_ref,
                 yconv_ref, rt_ref, at_ref, bt_ref, kt_ref, bh_ref, kh_ref, v_ref, wc_ref, gr_ref, bg_ref,
                 q_ref, kq_ref, vq_ref, gm_ref,
                 conv_scr, rkv_scr, wa_scr, *, tiles_per_seq):
    hm_refs = (rt_ref, at_ref, bt_ref, kt_ref, bh_ref, kh_ref, v_ref)
    i = pl.program_id(0)
    tm = h_ref.shape[0]
    hx = jnp.concatenate([halo_ref[...], h_ref[...]], axis=0)
    u = _rms(hx, ng_ref[...]).astype(BF16)
    z = jnp.dot(u, win_ref[...], preferred_element_type=F32)

    row = lax.broadcasted_iota(jnp.int32, (HALO + tm, 1), 0)
    keep = row >= jnp.where(i % tiles_per_seq == 0, HALO, 0)

    zc = z[HALO:]
    conv_scr[...] = jnp.where(keep, z[:, C_CONV + 256:C_CONV + 512] * z[:, C_CONV + 512:C_CONV + 768], 0.0)
    cw = cw_ref[...]
    conv = (conv_scr[pl.ds(HALO - 2, tm), :] * cw[0:1, :]
            + conv_scr[pl.ds(HALO - 1, tm), :] * cw[1:2, :]
            + conv_scr[pl.ds(HALO, tm), :] * cw[2:3, :])
    yconv_ref[...] = (zc[:, C_CONV:C_CONV + 256] * conv * _silu(zc[:, C_CONV + 768:C_CONV + 1024])).astype(BF16)

    rkv_scr[...] = jnp.where(keep, z[:, C_RKV:C_RKV + 768], 0.0)
    wa_scr[...] = jnp.where(keep, z[:, C_WA:C_WA + 128], 0.0)
    cur = rkv_scr[pl.ds(HALO, tm), :]
    rkv = cur + (rkv_scr[pl.ds(HALO - 1, tm), :] - cur) * mu_rkv_ref[...]
    cur = wa_scr[pl.ds(HALO, tm), :]
    wa = cur + (wa_scr[pl.ds(HALO - 1, tm), :] - cur) * mu_wa_ref[...]
    r = rkv[:, 0:256]
    k = rkv[:, 256:512]
    v = rkv[:, 512:768]
    lane = lax.broadcasted_iota(jnp.int32, wa.shape, 1)
    lora_in = jnp.where(lane < LORA, jnp.tanh(wa), wa)
    lora = jnp.dot(lora_in.astype(BF16), wl_ref[...], preferred_element_type=F32)
    wlog = -DECAY_SCALE * jax.nn.sigmoid(w0_ref[...] + lora[:, :RWKV_WIDTH])
    a = jax.nn.sigmoid(a0_ref[...] + lora[:, RWKV_WIDTH:])
    kk = k * kk_ref[...]
    kk = kk * lax.rsqrt(_dot_mask(kk * kk, seg_ref[...]) + 1e-12)
    kmod = k * (1.0 + (a - 1.0) * ka_ref[...])
    zb = kk * a
    cw = _dot_mask(wlog, tril_ref[...], left=True)
    rest = _dot_mask(wlog, triu_ref[...], left=True)
    inv = jnp.exp(-cw)
    to_end = jnp.exp(rest)
    wc = jnp.exp(_dot_mask(wlog, csum_ref[...], left=True))
    ops = (r * jnp.exp(cw), -kk * jnp.exp(cw - wlog), zb * inv, kmod * inv, zb * to_end, kmod * to_end, v)
    for hd in range(RWKV_HEADS):
        sl = slice(hd * RWKV_HEAD_DIM, (hd + 1) * RWKV_HEAD_DIM)
        for ref, val in zip(hm_refs, ops):
            ref[0, hd] = val[:, sl].astype(BF16)
        wc_ref[0, hd] = wc[:, sl]
    gate_r = _silu(zc[:, C_RG:C_RG + 256])
    bonus = _dot_mask(r * kmod * rk_ref[...], seg_ref[...]) * v
    gr_ref[...] = gate_r
    bg_ref[...] = bonus * gate_r

    ck = ck_ref[...]
    sk = sk_ref[...]
    qn = _rms(zc[:, C_QA:C_QA + Q_LORA_RANK], qg_ref[...]).astype(BF16)
    qm = jnp.dot(qn, wq_ref[...], preferred_element_type=F32)
    qs = jnp.dot(qn, wqs_ref[...], preferred_element_type=F32)
    scale = 1.0 / math.sqrt(QK_NOPE_DIM + QK_ROPE_DIM)
    kvn = _rms(zc[:, C_KVA:C_KVA + KV_LORA_RANK], kvg_ref[...]).astype(BF16)
    kv = jnp.dot(kvn, wkv_ref[...], preferred_element_type=F32)
    kr = zc[:, C_KR:C_KR + LANE] * ck + zc[:, C_KRS:C_KRS + LANE] * sk
    for hd in range(MLA_HEADS):
        o = hd * QK_PAD
        qr = qm[:, o + LANE:o + QK_PAD] * ck + qs[:, hd * LANE:(hd + 1) * LANE] * sk
        q_ref[:, o:o + LANE] = (qm[:, o:o + LANE] * scale).astype(BF16)
        q_ref[:, o + LANE:o + QK_PAD] = (qr * scale).astype(BF16)
        kq_ref[:, o:o + LANE] = kv[:, hd * LANE:(hd + 1) * LANE].astype(BF16)
        kq_ref[:, o + LANE:o + QK_PAD] = kr.astype(BF16)
    vq_ref[...] = kv[:, MLA_HEADS * QK_NOPE_DIM:].astype(BF16)
    gm_ref[...] = _silu(zc[:, C_MG:C_MG + MLA_WIDTH])


def _const_spec(shape):
    return pl.BlockSpec(shape, lambda i: (0,) * len(shape))


def _proj_call(h, lw, ck, sk, batch, seq):
    t = h.shape[0]
    tiles_per_seq = seq // TM
    row = lambda w: pl.BlockSpec((TM, w), lambda i: (i, 0))
    hm = pl.BlockSpec((1, RWKV_HEADS, TM, RWKV_HEAD_DIM),
                      lambda i: (i // tiles_per_seq, 0, i % tiles_per_seq, 0))
    halo = pl.BlockSpec((HALO, D_MODEL), lambda i: (jnp.maximum(i * (TM // HALO) - 1, 0), 0))
    chunk_id = np.arange(TM) // CHUNK
    same = chunk_id[:, None] == chunk_id[None, :]
    pos = np.arange(TM)
    tril = jnp.asarray(same & (pos[:, None] >= pos[None, :]), BF16)
    triu = jnp.asarray(same & (pos[:, None] < pos[None, :]), BF16)
    csum = jnp.asarray(np.arange(TM // CHUNK)[:, None] == chunk_id[None, :], BF16)
    consts = [lw["ng"], lw["win"], lw["cw"], lw["mu_rkv"], lw["mu_wa"], lw["w0"], lw["a0"], lw["wl"],
              lw["kk"], lw["ka"], lw["rk"], lw["seg"], tril, triu, csum,
              lw["qg"], lw["wq"], lw["wqs"], lw["kvg"], lw["wkv"]]
    hm_shape = jax.ShapeDtypeStruct((batch, RWKV_HEADS, seq, RWKV_HEAD_DIM), BF16)
    wc_spec = pl.BlockSpec((1, RWKV_HEADS, TM // CHUNK, RWKV_HEAD_DIM),
                           lambda i: (i // tiles_per_seq, 0, i % tiles_per_seq, 0))
    out_shape = [jax.ShapeDtypeStruct((t, CONV_WIDTH), BF16)] + [hm_shape] * 7 + [
        jax.ShapeDtypeStruct((batch, RWKV_HEADS, seq // CHUNK, RWKV_HEAD_DIM), F32),
        jax.ShapeDtypeStruct((t, RWKV_WIDTH), F32),
        jax.ShapeDtypeStruct((t, RWKV_WIDTH), F32),
        jax.ShapeDtypeStruct((t, MLA_HEADS * QK_PAD), BF16),
        jax.ShapeDtypeStruct((t, MLA_HEADS * QK_PAD), BF16),
        jax.ShapeDtypeStruct((t, MLA_WIDTH), BF16),
        jax.ShapeDtypeStruct((t, MLA_WIDTH), F32),
    ]
    out_specs = [row(CONV_WIDTH)] + [hm] * 7 + [wc_spec, row(RWKV_WIDTH), row(RWKV_WIDTH),
                                                  row(MLA_HEADS * QK_PAD), row(MLA_HEADS * QK_PAD),
                                                  row(MLA_WIDTH), row(MLA_WIDTH)]
    return pl.pallas_call(
        functools.partial(_proj_kernel, tiles_per_seq=tiles_per_seq),
        grid=(t // TM,),
        in_specs=[row(D_MODEL), halo] + [_const_spec(c.shape) for c in consts] + [row(LANE), row(LANE)],
        out_specs=out_specs,
        out_shape=out_shape,
        scratch_shapes=[pltpu.VMEM((HALO + TM, CONV_WIDTH), F32),
                        pltpu.VMEM((HALO + TM, 3 * RWKV_WIDTH), F32),
                        pltpu.VMEM((HALO + TM, 2 * LORA), F32)],
        compiler_params=pltpu.CompilerParams(dimension_semantics=("arbitrary",),
                                             vmem_limit_bytes=VMEM_LIMIT),
        name="proj",
    )(h, h, *consts, ck, sk)


def _rwkv_chunk_local(r_t, a_t, b_t, k_t, bh, kh, v, wc, tri_incl, tri_strict, eye):
    n = range(len(r_t))
    lhs = [jnp.concatenate([a_t[i], r_t[i]], axis=0) for i in n]
    ab = [_dot_nt(lhs[i], b_t[i]) for i in n]
    ak = [_dot_nt(lhs[i], k_t[i]) for i in n]
    aab = [jnp.where(tri_strict, ab[i][:CHUNK], 0.0).astype(BF16) for i in n]
    aak = [jnp.where(tri_strict, ak[i][:CHUNK], 0.0).astype(BF16) for i in n]
    arb = [jnp.where(tri_incl, ab[i][CHUNK:], 0.0).astype(BF16) for i in n]
    ark = [jnp.where(tri_incl, ak[i][CHUNK:], 0.0).astype(BF16) for i in n]

    x = [jnp.concatenate([a_t[i].astype(F32), _dot(aak[i], v[i])], axis=1) for i in n]
    p = aab
    for it in range(6):
        x = [x[i] + _dot(p[i], x[i]) for i in n]
        if it < 5:
            p = [_dot(p[i], p[i]).astype(BF16) for i in n]
    at = [x[i][:, :RWKV_HEAD_DIM].astype(BF16) for i in n]
    u = [x[i][:, RWKV_HEAD_DIM:].astype(BF16) for i in n]

    qp = [r_t[i].astype(F32) + _dot(arb[i], at[i]) for i in n]
    yloc = [_dot(arb[i], u[i]) + _dot(ark[i], v[i]) for i in n]
    mc = [jnp.where(eye, wc[i], 0.0) + _dot_tn(bh[i], at[i]) for i in n]
    nc = [_dot_tn(jnp.concatenate([bh[i], kh[i]], axis=0), jnp.concatenate([u[i], v[i]], axis=0)) for i in n]
    return qp, yloc, mc, nc


def _rwkv_kernel(rt_ref, at_ref, bt_ref, kt_ref, bh_ref, kh_ref, v_ref, wc_ref, gate_ref, bg_ref,
                 gnw_ref, gnb_ref, y_ref, state_ref):
    step = pl.program_id(1)

    @pl.when(step == 0)
    def _():
        state_ref[...] = jnp.zeros_like(state_ref)

    ri = lax.broadcasted_iota(jnp.int32, (CHUNK, CHUNK), 0)
    ci = lax.broadcasted_iota(jnp.int32, (CHUNK, CHUNK), 1)
    tri_incl = ri >= ci
    tri_strict = ri > ci
    eye = ri == ci
    items = [(c, hd) for c in range(RWKV_CHUNKS_PER_STEP) for hd in range(RWKV_HEADS)]
    load = lambda ref: [ref[0, hd, pl.ds(c * CHUNK, CHUNK), :] for c, hd in items]
    wc = [wc_ref[0, hd, pl.ds(step * RWKV_CHUNKS_PER_STEP + c, 1), :] for c, hd in items]
    qp, yloc, mc, nc = _rwkv_chunk_local(load(rt_ref), load(at_ref), load(bt_ref), load(kt_ref),
                                         load(bh_ref), load(kh_ref), load(v_ref), wc,
                                         tri_incl, tri_strict, eye)
    states = [state_ref[hd] for hd in range(RWKV_HEADS)]
    heads = range(RWKV_HEADS)
    for c in range(RWKV_CHUNKS_PER_STEP):
        rows = pl.ds(c * CHUNK, CHUNK)
        o = c * RWKV_HEADS
        y = [_dot(qp[o + hd], states[hd]) + yloc[o + hd] for hd in heads]
        states = [_dot(mc[o + hd], states[hd]) + nc[o + hd] for hd in heads]
        outs = []
        for hd in heads:
            mean = jnp.mean(y[hd], axis=-1, keepdims=True)
            yc = y[hd] - mean
            var = jnp.mean(yc * yc, axis=-1, keepdims=True)
            outs.append(yc * lax.rsqrt(var + GN_EPS) * gnw_ref[hd:hd + 1, :] + gnb_ref[hd:hd + 1, :])
        y_ref[0, rows, :] = (jnp.concatenate(outs, axis=1) * gate_ref[0, rows, :]
                             + bg_ref[0, rows, :]).astype(BF16)
    for hd in heads:
        state_ref[hd] = states[hd]


def _rwkv_call(hm, wc, gate, bg, lw, batch, seq):
    tc = RWKV_CHUNKS_PER_STEP * CHUNK
    hm_spec = pl.BlockSpec((1, RWKV_HEADS, tc, RWKV_HEAD_DIM), lambda b, c: (b, 0, c, 0))
    wc_spec = pl.BlockSpec((1, RWKV_HEADS, seq // CHUNK, RWKV_HEAD_DIM), lambda b, c: (b, 0, 0, 0))
    nat = pl.BlockSpec((1, tc, RWKV_WIDTH), lambda b, c: (b, c, 0))
    par = pl.BlockSpec((RWKV_HEADS, RWKV_HEAD_DIM), lambda b, c: (0, 0))
    return pl.pallas_call(
        _rwkv_kernel,
        grid=(batch, seq // tc),
        in_specs=[hm_spec] * 7 + [wc_spec, nat, nat, par, par],
        out_specs=nat,
        out_shape=jax.ShapeDtypeStruct((batch, seq, RWKV_WIDTH), BF16),
        scratch_shapes=[pltpu.VMEM((RWKV_HEADS, RWKV_HEAD_DIM, RWKV_HEAD_DIM), F32)],
        compiler_params=pltpu.CompilerParams(dimension_semantics=("arbitrary", "arbitrary"),
                                             vmem_limit_bytes=VMEM_LIMIT),
        name="rwkv",
    )(*hm, wc, gate.reshape(batch, seq, RWKV_WIDTH), bg.reshape(batch, seq, RWKV_WIDTH),
      lw["gnw"], lw["gnb"])


def _attn_kernel(q_ref, k_ref, v_ref, g_ref, o_ref):
    i = pl.program_id(2)
    tq = q_ref.shape[1]
    q = q_ref[0]

    def tile(j, carry, masked):
        m, l, acc = carry
        rows = pl.ds(pl.multiple_of(j * tq, tq), tq)
        s = lax.dot_general(q, k_ref[0, rows, :], (((1,), (1,)), ((), ())), preferred_element_type=F32)
        if masked:
            qc = lax.broadcasted_iota(jnp.int32, (tq, tq), 0) // CHUNK
            kc = lax.broadcasted_iota(jnp.int32, (tq, tq), 1) // CHUNK
            s = jnp.where(kc <= qc, s, -1e30)
        m_new = jnp.maximum(m, jnp.max(s, axis=-1, keepdims=True))
        alpha = jnp.exp(m - m_new)
        p = jnp.exp(s - m_new)
        l = alpha * l + jnp.sum(p, axis=-1, keepdims=True)
        acc = alpha * acc + jnp.dot(p.astype(BF16), v_ref[0, rows, :], preferred_element_type=F32)
        return m_new, l, acc

    init = (jnp.full((tq, 1), -jnp.inf, F32), jnp.zeros((tq, 1), F32), jnp.zeros((tq, V_HEAD_DIM), F32))
    carry = lax.fori_loop(0, i, lambda j, c: tile(j, c, False), init)
    _, l, acc = tile(i, carry, True)
    o_ref[0] = (acc / l * g_ref[0]).astype(BF16)


def _attn_call(q, k, v, gate, batch, seq):
    q = q.reshape(batch, seq, MLA_HEADS * QK_PAD)
    k = k.reshape(batch, seq, MLA_HEADS * QK_PAD)
    v = v.reshape(batch, seq, MLA_WIDTH)
    gate = gate.reshape(batch, seq, MLA_WIDTH)
    return pl.pallas_call(
        _attn_kernel,
        grid=(batch, MLA_HEADS, seq // TQ),
        in_specs=[pl.BlockSpec((1, TQ, QK_PAD), lambda b, h, i: (b, i, h)),
                  pl.BlockSpec((1, seq, QK_PAD), lambda b, h, i: (b, 0, h)),
                  pl.BlockSpec((1, seq, V_HEAD_DIM), lambda b, h, i: (b, 0, h)),
                  pl.BlockSpec((1, TQ, V_HEAD_DIM), lambda b, h, i: (b, i, h))],
        out_specs=pl.BlockSpec((1, TQ, V_HEAD_DIM), lambda b, h, i: (b, i, h)),
        out_shape=jax.ShapeDtypeStruct((batch, seq, MLA_WIDTH), BF16),
        compiler_params=pltpu.CompilerParams(
            dimension_semantics=("arbitrary", "arbitrary", "arbitrary"), vmem_limit_bytes=VMEM_LIMIT),
        name="attn",
    )(q, k, v, gate)


def _out_kernel(h_ref, yc_ref, yr_ref, ym_ref, wo_ref, p_ref, wple_ref, png_ref, wpg_ref, fg_ref,
                o_ref, *, final):
    ycat = jnp.concatenate([yc_ref[...], yr_ref[...], ym_ref[...]], axis=-1)
    h = h_ref[...] + jnp.dot(ycat, wo_ref[...], preferred_element_type=F32)
    gate = jax.nn.sigmoid(jnp.dot(_rms(h, png_ref[...]).astype(BF16), wpg_ref[...],
                                  preferred_element_type=F32))
    h = h + jnp.dot(p_ref[...].astype(BF16), wple_ref[...], preferred_element_type=F32) * gate
    if final:
        h = _rms(h, fg_ref[...])
    o_ref[...] = h


def _out_call(h, yc, yr, ym, p, layer, lw, final_g, final):
    t = h.shape[0]
    row = lambda w: pl.BlockSpec((TM, w), lambda i: (i, 0))
    consts_a = [lw["wo"]]
    consts_b = [lw["wple"], lw["png"], lw["wpg"], final_g]
    return pl.pallas_call(
        functools.partial(_out_kernel, final=final),
        grid=(t // TM,),
        in_specs=[row(D_MODEL), row(CONV_WIDTH), row(RWKV_WIDTH), row(MLA_WIDTH)]
        + [_const_spec(c.shape) for c in consts_a]
        + [pl.BlockSpec((None, TM, PLE_DIM), lambda i: (layer, i, 0))]
        + [_const_spec(c.shape) for c in consts_b],
        out_specs=row(D_MODEL),
        out_shape=jax.ShapeDtypeStruct((t, D_MODEL), F32),
        compiler_params=pltpu.CompilerParams(dimension_semantics=("arbitrary",),
                                             vmem_limit_bytes=VMEM_LIMIT),
        name="out",
    )(h, yc, yr, ym, *consts_a, p, *consts_b)


def _swap_halves(w):
    half = w.shape[-1] // 2
    return jnp.concatenate([w[..., half:], w[..., :half]], axis=-1)


def _prep_layer(i, norm_mix_g, w_in, conv_w, rwkv_mu, rwkv_w0, rwkv_w2, rwkv_a0, rwkv_a2, rwkv_kk,
                rwkv_ka, rwkv_rk, rwkv_gn_w, rwkv_gn_b, mla_q_norm_g, mla_w_qb, mla_kv_norm_g,
                mla_w_kvb, w_out, ple_w, ple_norm_g, ple_gate_w):
    w = w_in[i]
    o_rs = 4 * CONV_WIDTH
    o_rg = o_rs + 3 * RWKV_WIDTH + 2 * LORA
    o_qa = o_rg + RWKV_WIDTH
    o_kva = o_qa + Q_LORA_RANK
    o_kr = o_kva + KV_LORA_RANK
    o_mg = o_kr + QK_ROPE_DIM
    zeros = jnp.zeros((D_MODEL, QK_ROPE_DIM), w.dtype)
    krope = w[:, o_kr:o_mg]
    win = jnp.concatenate([
        w[:, :o_rs], w[:, o_rs:o_rs + 3 * RWKV_WIDTH], w[:, o_rg:o_qa], w[:, o_mg:],
        w[:, o_qa:o_kva], w[:, o_kva:o_kr], w[:, o_rs + 3 * RWKV_WIDTH:o_rg],
        krope, zeros, _swap_halves(krope), zeros], axis=1).astype(BF16)

    wl = jnp.zeros((2 * LORA, 2 * RWKV_WIDTH), F32)
    wl = wl.at[:LORA, :RWKV_WIDTH].set(rwkv_w2[i]).at[LORA:, RWKV_WIDTH:].set(rwkv_a2[i]).astype(BF16)
    head = np.arange(RWKV_WIDTH) // RWKV_HEAD_DIM
    seg = jnp.asarray(head[:, None] == head[None, :], BF16)

    wqb = mla_w_qb[i].reshape(Q_LORA_RANK, MLA_HEADS, QK_NOPE_DIM + QK_ROPE_DIM)
    zq = jnp.zeros((Q_LORA_RANK, MLA_HEADS, QK_ROPE_DIM), F32)
    wq = jnp.concatenate([wqb, zq], axis=-1).reshape(Q_LORA_RANK, MLA_HEADS * QK_PAD).astype(BF16)
    wqs = jnp.concatenate([_swap_halves(wqb[..., QK_NOPE_DIM:]), zq], axis=-1)
    wqs = wqs.reshape(Q_LORA_RANK, MLA_HEADS * LANE).astype(BF16)
    wkvb = mla_w_kvb[i].reshape(KV_LORA_RANK, MLA_HEADS, QK_NOPE_DIM + V_HEAD_DIM)
    wkv = jnp.concatenate([wkvb[..., :QK_NOPE_DIM].reshape(KV_LORA_RANK, -1),
                           wkvb[..., QK_NOPE_DIM:].reshape(KV_LORA_RANK, -1)], axis=1).astype(BF16)
    row = lambda x: x.reshape(1, -1)
    hd = lambda x: x.reshape(RWKV_HEADS, RWKV_HEAD_DIM)
    return dict(
        ng=row(norm_mix_g[i]), win=win, cw=conv_w[i],
        mu_rkv=row(rwkv_mu[i, :3 * RWKV_WIDTH]), mu_wa=row(rwkv_mu[i, 3 * RWKV_WIDTH:]),
        w0=row(rwkv_w0[i]), a0=row(rwkv_a0[i]), wl=wl, kk=row(rwkv_kk[i]), ka=row(rwkv_ka[i]), seg=seg,
        qg=row(mla_q_norm_g[i]), wq=wq, wqs=wqs, kvg=row(mla_kv_norm_g[i]), wkv=wkv,
        rk=row(rwkv_rk[i]), gnw=hd(rwkv_gn_w[i]), gnb=hd(rwkv_gn_b[i]),
        wo=w_out[i].astype(BF16), wple=ple_w[i].astype(BF16), png=row(ple_norm_g[i]),
        wpg=ple_gate_w[i].astype(BF16))


def kernel(x, p, positions, norm_mix_g, w_in, conv_w, rwkv_mu, rwkv_w0, rwkv_w2, rwkv_a0, rwkv_a2,
           rwkv_kk, rwkv_ka, rwkv_rk, rwkv_gn_w, rwkv_gn_b, mla_q_norm_g, mla_w_qb, mla_kv_norm_g,
           mla_w_kvb, w_out, ple_w, ple_norm_g, ple_gate_w, final_norm_g):
    batch, seq, _ = x.shape
    depth = w_in.shape[0]
    t = batch * seq

    inv_freq = 1.0 / (ROPE_THETA ** (jnp.arange(0, QK_ROPE_DIM, 2, dtype=F32) / QK_ROPE_DIM))
    ang = positions.astype(F32).reshape(t, 1) * inv_freq
    cos, sin = jnp.cos(ang), jnp.sin(ang)
    pad = jnp.zeros((t, LANE - QK_ROPE_DIM), F32)
    ck = jnp.concatenate([cos, cos, pad], axis=1)
    sk = jnp.concatenate([-sin, sin, pad], axis=1)

    h = x.reshape(t, D_MODEL)
    p = p.reshape(depth, t, PLE_DIM)
    final_g = final_norm_g.reshape(1, D_MODEL)
    for i in range(depth):
        lw = _prep_layer(i, norm_mix_g, w_in, conv_w, rwkv_mu, rwkv_w0, rwkv_w2, rwkv_a0, rwkv_a2,
                         rwkv_kk, rwkv_ka, rwkv_rk, rwkv_gn_w, rwkv_gn_b, mla_q_norm_g, mla_w_qb,
                         mla_kv_norm_g, mla_w_kvb, w_out, ple_w, ple_norm_g, ple_gate_w)
        (yconv, *hm, wc, gate_r, bg, q, kq, vq, gate_m) = _proj_call(h, lw, ck, sk, batch, seq)
        y_rwkv = _rwkv_call(hm, wc, gate_r, bg, lw, batch, seq).reshape(t, RWKV_WIDTH)
        y_mla = _attn_call(q, kq, vq, gate_m, batch, seq).reshape(t, MLA_WIDTH)
        h = _out_call(h, yconv, y_rwkv, y_mla, p, i, lw, final_g, final=(i == depth - 1))
    return h.reshape(batch, seq, D_MODEL)
```

```python
import functools
import math

import jax
import jax.numpy as jnp
import numpy as np
from jax import lax
from jax.experimental import pallas as pl
from jax.experimental.pallas import tpu as pltpu

F32 = jnp.float32
BF16 = jnp.bfloat16

D_MODEL = 1024
CHUNK = 64
PLE_DIM = 256
NORM_EPS = 1e-6
CONV_WIDTH = 256
RWKV_HEADS = 4
RWKV_HEAD_DIM = 64
RWKV_WIDTH = RWKV_HEADS * RWKV_HEAD_DIM
LORA = 64
DECAY_SCALE = math.exp(-0.5)
GN_EPS = 64e-5
MLA_HEADS = 4
QK_NOPE_DIM = 128
QK_ROPE_DIM = 64
V_HEAD_DIM = 128
Q_LORA_RANK = 384
KV_LORA_RANK = 256
MLA_WIDTH = MLA_HEADS * V_HEAD_DIM
ROPE_THETA = 10000.0
D_MIX = CONV_WIDTH + RWKV_WIDTH + MLA_WIDTH

C_CONV = 0
C_RKV = 1024
C_RG = 1792
C_MG = 2048
C_QA = 2560
C_KVA = 2944
C_WA = 3200
C_KR = 3328
C_KRS = 3456
D_IN_P = 3584
QK_PAD = 256

LANE = 128
HALO = 16
TM = 512
TQ = 512
RWKV_CHUNKS_PER_STEP = 8
VMEM_LIMIT = 56 * 1024 * 1024


def _dot(a, b):
    return jnp.dot(a.astype(BF16), b.astype(BF16), preferred_element_type=F32)


def _dot_nt(a, b):
    return lax.dot_general(a.astype(BF16), b.astype(BF16), (((1,), (1,)), ((), ())),
                           preferred_element_type=F32)


def _dot_tn(a, b):
    return lax.dot_general(a.astype(BF16), b.astype(BF16), (((0,), (0,)), ((), ())),
                           preferred_element_type=F32)


def _split2(x):
    hi = x.astype(BF16)
    return hi, (x - hi.astype(F32)).astype(BF16)


def _dot_mask(x, m01, left=False):
    hi, lo = _split2(x)
    if left:
        return jnp.dot(m01, hi, preferred_element_type=F32) + jnp.dot(m01, lo, preferred_element_type=F32)
    return jnp.dot(hi, m01, preferred_element_type=F32) + jnp.dot(lo, m01, preferred_element_type=F32)


def _rms(x, g):
    return x * lax.rsqrt(jnp.mean(x * x, axis=-1, keepdims=True) + NORM_EPS) * g


def _silu(x):
    return x * jax.nn.sigmoid(x)


def _proj_kernel(h_ref, halo_ref, ng_ref, win_ref, cw_ref, mu_rkv_ref, mu_wa_ref, w0_ref, a0_ref,
                 wl_ref, kk_ref, ka_ref, rk_ref, seg_ref, tril_ref, triu_ref, csum_ref,
                 qg_ref, wq_ref, wqs_ref, kvg_ref, wkv_ref, ck_ref, sk_ref,
                 yconv_ref, rt_ref, at_ref, bt_ref, kt_ref, bh_ref, kh_ref, v_ref, wc_ref, gr_ref, bg_ref,
                 q_ref, kq_ref, vq_ref, gm_ref,
                 conv_scr, rkv_scr, wa_scr, *, tiles_per_seq):
    hm_refs = (rt_ref, at_ref, bt_ref, kt_ref, bh_ref, kh_ref, v_ref)
    i = pl.program_id(0)
    tm = h_ref.shape[0]
    hx = jnp.concatenate([halo_ref[...], h_ref[...]], axis=0)
    u = _rms(hx, ng_ref[...]).astype(BF16)
    z = jnp.dot(u, win_ref[...], preferred_element_type=F32)

    row = lax.broadcasted_iota(jnp.int32, (HALO + tm, 1), 0)
    keep = row >= jnp.where(i % tiles_per_seq == 0, HALO, 0)

    zc = z[HALO:]
    conv_scr[...] = jnp.where(keep, z[:, C_CONV + 256:C_CONV + 512] * z[:, C_CONV + 512:C_CONV + 768], 0.0)
    cw = cw_ref[...]
    conv = (conv_scr[pl.ds(HALO - 2, tm), :] * cw[0:1, :]
            + conv_scr[pl.ds(HALO - 1, tm), :] * cw[1:2, :]
            + conv_scr[pl.ds(HALO, tm), :] * cw[2:3, :])
    yconv_ref[...] = (zc[:, C_CONV:C_CONV + 256] * conv * _silu(zc[:, C_CONV + 768:C_CONV + 1024])).astype(BF16)

    rkv_scr[...] = jnp.where(keep, z[:, C_RKV:C_RKV + 768], 0.0)
    wa_scr[...] = jnp.where(keep, z[:, C_WA:C_WA + 128], 0.0)
    cur = rkv_scr[pl.ds(HALO, tm), :]
    rkv = cur + (rkv_scr[pl.ds(HALO - 1, tm), :] - cur) * mu_rkv_ref[...]
    cur = wa_scr[pl.ds(HALO, tm), :]
    wa = cur + (wa_scr[pl.ds(HALO - 1, tm), :] - cur) * mu_wa_ref[...]
    r = rkv[:, 0:256]
    k = rkv[:, 256:512]
    v = rkv[:, 512:768]
    lane = lax.broadcasted_iota(jnp.int32, wa.shape, 1)
    lora_in = jnp.where(lane < LORA, jnp.tanh(wa), wa)
    lora = jnp.dot(lora_in.astype(BF16), wl_ref[...], preferred_element_type=F32)
    wlog = -DECAY_SCALE * jax.nn.sigmoid(w0_ref[...] + lora[:, :RWKV_WIDTH])
    a = jax.nn.sigmoid(a0_ref[...] + lora[:, RWKV_WIDTH:])
    kk = k * kk_ref[...]
    kk = kk * lax.rsqrt(_dot_mask(kk * kk, seg_ref[...]) + 1e-12)
    kmod = k * (1.0 + (a - 1.0) * ka_ref[...])
    zb = kk * a
    cw = _dot_mask(wlog, tril_ref[...], left=True)
    rest = _dot_mask(wlog, triu_ref[...], left=True)
    inv = jnp.exp(-cw)
    to_end = jnp.exp(rest)
    wc = jnp.exp(_dot_mask(wlog, csum_ref[...], left=True))
    ops = (r * jnp.exp(cw), -kk * jnp.exp(cw - wlog), zb * inv, kmod * inv, zb * to_end, kmod * to_end, v)
    for hd in range(RWKV_HEADS):
        sl = slice(hd * RWKV_HEAD_DIM, (hd + 1) * RWKV_HEAD_DIM)
        for ref, val in zip(hm_refs, ops):
            ref[0, hd] = val[:, sl].astype(BF16)
        wc_ref[0, hd] = wc[:, sl]
    gate_r = _silu(zc[:, C_RG:C_RG + 256])
    bonus = _dot_mask(r * kmod * rk_ref[...], seg_ref[...]) * v
    gr_ref[...] = gate_r
    bg_ref[...] = bonus * gate_r

    ck = ck_ref[...]
    sk = sk_ref[...]
    qn = _rms(zc[:, C_QA:C_QA + Q_LORA_RANK], qg_ref[...]).astype(BF16)
    qm = jnp.dot(qn, wq_ref[...], preferred_element_type=F32)
    qs = jnp.dot(qn, wqs_ref[...], preferred_element_type=F32)
    scale = math.log2(math.e) / math.sqrt(QK_NOPE_DIM + QK_ROPE_DIM)
    kvn = _rms(zc[:, C_KVA:C_KVA + KV_LORA_RANK], kvg_ref[...]).astype(BF16)
    kv = jnp.dot(kvn, wkv_ref[...], preferred_element_type=F32)
    kr = zc[:, C_KR:C_KR + LANE] * ck + zc[:, C_KRS:C_KRS + LANE] * sk
    for hd in range(MLA_HEADS):
        o = hd * QK_PAD
        qr = qm[:, o + LANE:o + QK_PAD] * ck + qs[:, hd * LANE:(hd + 1) * LANE] * sk
        q_ref[:, o:o + LANE] = (qm[:, o:o + LANE] * scale).astype(BF16)
        q_ref[:, o + LANE:o + QK_PAD] = (qr * scale).astype(BF16)
        kq_ref[:, o:o + LANE] = kv[:, hd * LANE:(hd + 1) * LANE].astype(BF16)
        kq_ref[:, o + LANE:o + QK_PAD] = kr.astype(BF16)
    vq_ref[0] = kv[:, MLA_HEADS * QK_NOPE_DIM:].T.astype(BF16)
    gm_ref[...] = _silu(zc[:, C_MG:C_MG + MLA_WIDTH])


def _const_spec(shape):
    return pl.BlockSpec(shape, lambda i: (0,) * len(shape))


def _proj_call(h, lw, ck, sk, batch, seq):
    t = h.shape[0]
    tiles_per_seq = seq // TM
    row = lambda w: pl.BlockSpec((TM, w), lambda i: (i, 0))
    hm = pl.BlockSpec((1, RWKV_HEADS, TM, RWKV_HEAD_DIM),
                      lambda i: (i // tiles_per_seq, 0, i % tiles_per_seq, 0))
    halo = pl.BlockSpec((HALO, D_MODEL), lambda i: (jnp.maximum(i * (TM // HALO) - 1, 0), 0))
    chunk_id = np.arange(TM) // CHUNK
    same = chunk_id[:, None] == chunk_id[None, :]
    pos = np.arange(TM)
    tril = jnp.asarray(same & (pos[:, None] >= pos[None, :]), BF16)
    triu = jnp.asarray(same & (pos[:, None] < pos[None, :]), BF16)
    csum = jnp.asarray(np.arange(TM // CHUNK)[:, None] == chunk_id[None, :], BF16)
    consts = [lw["ng"], lw["win"], lw["cw"], lw["mu_rkv"], lw["mu_wa"], lw["w0"], lw["a0"], lw["wl"],
              lw["kk"], lw["ka"], lw["rk"], lw["seg"], tril, triu, csum,
              lw["qg"], lw["wq"], lw["wqs"], lw["kvg"], lw["wkv"]]
    hm_shape = jax.ShapeDtypeStruct((batch, RWKV_HEADS, seq, RWKV_HEAD_DIM), BF16)
    wc_spec = pl.BlockSpec((1, RWKV_HEADS, TM // CHUNK, RWKV_HEAD_DIM),
                           lambda i: (i // tiles_per_seq, 0, i % tiles_per_seq, 0))
    out_shape = [jax.ShapeDtypeStruct((t, CONV_WIDTH), BF16)] + [hm_shape] * 7 + [
        jax.ShapeDtypeStruct((batch, RWKV_HEADS, seq // CHUNK, RWKV_HEAD_DIM), F32),
        jax.ShapeDtypeStruct((t, RWKV_WIDTH), F32),
        jax.ShapeDtypeStruct((t, RWKV_WIDTH), F32),
        jax.ShapeDtypeStruct((t, MLA_HEADS * QK_PAD), BF16),
        jax.ShapeDtypeStruct((t, MLA_HEADS * QK_PAD), BF16),
        jax.ShapeDtypeStruct((t // TM, MLA_WIDTH, TM), BF16),
        jax.ShapeDtypeStruct((t, MLA_WIDTH), F32),
    ]
    vt_spec = pl.BlockSpec((1, MLA_WIDTH, TM), lambda i: (i, 0, 0))
    out_specs = [row(CONV_WIDTH)] + [hm] * 7 + [wc_spec, row(RWKV_WIDTH), row(RWKV_WIDTH),
                                                  row(MLA_HEADS * QK_PAD), row(MLA_HEADS * QK_PAD),
                                                  vt_spec, row(MLA_WIDTH)]
    return pl.pallas_call(
        functools.partial(_proj_kernel, tiles_per_seq=tiles_per_seq),
        grid=(t // TM,),
        in_specs=[row(D_MODEL), halo] + [_const_spec(c.shape) for c in consts] + [row(LANE), row(LANE)],
        out_specs=out_specs,
        out_shape=out_shape,
        scratch_shapes=[pltpu.VMEM((HALO + TM, CONV_WIDTH), F32),
                        pltpu.VMEM((HALO + TM, 3 * RWKV_WIDTH), F32),
                        pltpu.VMEM((HALO + TM, 2 * LORA), F32)],
        compiler_params=pltpu.CompilerParams(dimension_semantics=("arbitrary",),
                                             vmem_limit_bytes=VMEM_LIMIT),
        name="proj",
    )(h, h, *consts, ck, sk)


def _rwkv_chunk_local(r_t, a_t, b_t, k_t, bh, kh, v, wc, tri_incl, tri_strict, eye):
    n = range(len(r_t))
    lhs = [jnp.concatenate([a_t[i], r_t[i]], axis=0) for i in n]
    ab = [_dot_nt(lhs[i], b_t[i]) for i in n]
    ak = [_dot_nt(lhs[i], k_t[i]) for i in n]
    aab = [jnp.where(tri_strict, ab[i][:CHUNK], 0.0).astype(BF16) for i in n]
    aak = [jnp.where(tri_strict, ak[i][:CHUNK], 0.0).astype(BF16) for i in n]
    arb = [jnp.where(tri_incl, ab[i][CHUNK:], 0.0).astype(BF16) for i in n]
    ark = [jnp.where(tri_incl, ak[i][CHUNK:], 0.0).astype(BF16) for i in n]

    x = [jnp.concatenate([a_t[i].astype(F32), _dot(aak[i], v[i])], axis=1) for i in n]
    p = aab
    for it in range(6):
        x = [x[i] + _dot(p[i], x[i]) for i in n]
        if it < 5:
            p = [_dot(p[i], p[i]).astype(BF16) for i in n]
    at = [x[i][:, :RWKV_HEAD_DIM].astype(BF16) for i in n]
    u = [x[i][:, RWKV_HEAD_DIM:].astype(BF16) for i in n]

    qp = [r_t[i].astype(F32) + _dot(arb[i], at[i]) for i in n]
    yloc = [_dot(arb[i], u[i]) + _dot(ark[i], v[i]) for i in n]
    mc = [jnp.where(eye, wc[i], 0.0) + _dot_tn(bh[i], at[i]) for i in n]
    nc = [_dot_tn(jnp.concatenate([bh[i], kh[i]], axis=0), jnp.concatenate([u[i], v[i]], axis=0)) for i in n]
    return qp, yloc, mc, nc


def _rwkv_kernel(rt_ref, at_ref, bt_ref, kt_ref, bh_ref, kh_ref, v_ref, wc_ref, gate_ref, bg_ref,
                 gnw_ref, gnb_ref, y_ref, state_ref):
    step = pl.program_id(1)

    @pl.when(step == 0)
    def _():
        state_ref[...] = jnp.zeros_like(state_ref)

    ri = lax.broadcasted_iota(jnp.int32, (CHUNK, CHUNK), 0)
    ci = lax.broadcasted_iota(jnp.int32, (CHUNK, CHUNK), 1)
    tri_incl = ri >= ci
    tri_strict = ri > ci
    eye = ri == ci
    items = [(c, hd) for c in range(RWKV_CHUNKS_PER_STEP) for hd in range(RWKV_HEADS)]
    load = lambda ref: [ref[0, hd, pl.ds(c * CHUNK, CHUNK), :] for c, hd in items]
    wc = [wc_ref[0, hd, pl.ds(step * RWKV_CHUNKS_PER_STEP + c, 1), :] for c, hd in items]
    qp, yloc, mc, nc = _rwkv_chunk_local(load(rt_ref), load(at_ref), load(bt_ref), load(kt_ref),
                                         load(bh_ref), load(kh_ref), load(v_ref), wc,
                                         tri_incl, tri_strict, eye)
    states = [state_ref[hd] for hd in range(RWKV_HEADS)]
    heads = range(RWKV_HEADS)
    for c in range(RWKV_CHUNKS_PER_STEP):
        rows = pl.ds(c * CHUNK, CHUNK)
        o = c * RWKV_HEADS
        y = [_dot(qp[o + hd], states[hd]) + yloc[o + hd] for hd in heads]
        states = [_dot(mc[o + hd], states[hd]) + nc[o + hd] for hd in heads]
        outs = []
        for hd in heads:
            mean = jnp.mean(y[hd], axis=-1, keepdims=True)
            yc = y[hd] - mean
            var = jnp.mean(yc * yc, axis=-1, keepdims=True)
            outs.append(yc * lax.rsqrt(var + GN_EPS) * gnw_ref[hd:hd + 1, :] + gnb_ref[hd:hd + 1, :])
        y_ref[0, rows, :] = (jnp.concatenate(outs, axis=1) * gate_ref[0, rows, :]
                             + bg_ref[0, rows, :]).astype(BF16)
    for hd in heads:
        state_ref[hd] = states[hd]


def _rwkv_call(hm, wc, gate, bg, lw, batch, seq):
    tc = RWKV_CHUNKS_PER_STEP * CHUNK
    hm_spec = pl.BlockSpec((1, RWKV_HEADS, tc, RWKV_HEAD_DIM), lambda b, c: (b, 0, c, 0))
    wc_spec = pl.BlockSpec((1, RWKV_HEADS, seq // CHUNK, RWKV_HEAD_DIM), lambda b, c: (b, 0, 0, 0))
    nat = pl.BlockSpec((1, tc, RWKV_WIDTH), lambda b, c: (b, c, 0))
    par = pl.BlockSpec((RWKV_HEADS, RWKV_HEAD_DIM), lambda b, c: (0, 0))
    return pl.pallas_call(
        _rwkv_kernel,
        grid=(batch, seq // tc),
        in_specs=[hm_spec] * 7 + [wc_spec, nat, nat, par, par],
        out_specs=nat,
        out_shape=jax.ShapeDtypeStruct((batch, seq, RWKV_WIDTH), BF16),
        scratch_shapes=[pltpu.VMEM((RWKV_HEADS, RWKV_HEAD_DIM, RWKV_HEAD_DIM), F32)],
        compiler_params=pltpu.CompilerParams(dimension_semantics=("arbitrary", "arbitrary"),
                                             vmem_limit_bytes=VMEM_LIMIT),
        name="rwkv",
    )(*hm, wc, gate.reshape(batch, seq, RWKV_WIDTH), bg.reshape(batch, seq, RWKV_WIDTH),
      lw["gnw"], lw["gnb"])


def _attn_kernel(q_ref, k_ref, vt_ref, g_ref, o_ref, m_scr, l_scr, acc_scr, s0_scr, s1_scr, c0_scr, c1_scr):
    i = pl.program_id(2)
    tq = q_ref.shape[1]
    q = q_ref[0]
    bufs = ((s0_scr, c0_scr), (s1_scr, c1_scr))
    m_scr[...] = jnp.full(m_scr.shape, -jnp.inf, F32)
    l_scr[...] = jnp.zeros(l_scr.shape, F32)
    acc_scr[...] = jnp.zeros(acc_scr.shape, F32)

    def scores(j, slot, masked):
        s_ref, c_ref = bufs[slot]
        k = k_ref[0, pl.ds(pl.multiple_of(j * tq, tq), tq), :]
        s = lax.dot_general(k, q, (((1,), (1,)), ((), ())), preferred_element_type=F32)
        if masked:
            kc = lax.broadcasted_iota(jnp.int32, s.shape, 0) // CHUNK
            qc = lax.broadcasted_iota(jnp.int32, s.shape, 1) // CHUNK
            s = jnp.where(kc <= qc, s, -1e30)
        s_ref[...] = s
        c_ref[...] = jnp.max(s, axis=0, keepdims=True)

    def accumulate(j, slot):
        s_ref, c_ref = bufs[slot]
        m_old = m_scr[...]
        m_new = jnp.maximum(m_old, c_ref[...])
        alpha = jnp.exp2(m_old - m_new)
        p = jnp.exp2(s_ref[...] - m_new)
        m_scr[...] = m_new
        l_scr[...] = alpha * l_scr[...] + jnp.sum(p, axis=0, keepdims=True)
        acc_scr[...] = alpha * acc_scr[...] + jnp.dot(vt_ref[0, j], p.astype(BF16),
                                                      preferred_element_type=F32)

    def by_parity(j, fn):
        lax.cond(j % 2 == 0, lambda: fn(0), lambda: fn(1))

    lax.cond(i == 0, lambda: scores(0, 0, True), lambda: scores(0, 0, False))

    def body(j, carry):
        def step(slot):
            scores(j + 1, 1 - slot, False)
            accumulate(j, slot)
        by_parity(j, step)
        return carry

    lax.fori_loop(0, i - 1, body, 0)

    @pl.when(i > 0)
    def _():
        def step(slot):
            scores(i, 1 - slot, True)
            accumulate(i - 1, slot)
        by_parity(i - 1, step)

    by_parity(i, lambda slot: accumulate(i, slot))
    o_ref[0] = ((acc_scr[...] / l_scr[...]).T * g_ref[0]).astype(BF16)


def _attn_call(q, k, vt, gate, batch, seq):
    q = q.reshape(batch, seq, MLA_HEADS * QK_PAD)
    k = k.reshape(batch, seq, MLA_HEADS * QK_PAD)
    vt = vt.reshape(batch, seq // TQ, MLA_WIDTH, TQ)
    gate = gate.reshape(batch, seq, MLA_WIDTH)
    return pl.pallas_call(
        _attn_kernel,
        grid=(batch, MLA_HEADS, seq // TQ),
        in_specs=[pl.BlockSpec((1, TQ, QK_PAD), lambda b, h, i: (b, i, h)),
                  pl.BlockSpec((1, seq, QK_PAD), lambda b, h, i: (b, 0, h)),
                  pl.BlockSpec((1, seq // TQ, V_HEAD_DIM, TQ), lambda b, h, i: (b, 0, h, 0)),
                  pl.BlockSpec((1, TQ, V_HEAD_DIM), lambda b, h, i: (b, i, h))],
        out_specs=pl.BlockSpec((1, TQ, V_HEAD_DIM), lambda b, h, i: (b, i, h)),
        out_shape=jax.ShapeDtypeStruct((batch, seq, MLA_WIDTH), BF16),
        scratch_shapes=[pltpu.VMEM((1, TQ), F32), pltpu.VMEM((1, TQ), F32),
                        pltpu.VMEM((V_HEAD_DIM, TQ), F32),
                        pltpu.VMEM((TQ, TQ), F32), pltpu.VMEM((TQ, TQ), F32),
                        pltpu.VMEM((1, TQ), F32), pltpu.VMEM((1, TQ), F32)],
        compiler_params=pltpu.CompilerParams(
            dimension_semantics=("arbitrary", "arbitrary", "arbitrary"), vmem_limit_bytes=VMEM_LIMIT),
        name="attn",
    )(q, k, vt, gate)


def _out_kernel(h_ref, yc_ref, yr_ref, ym_ref, wo_ref, p_ref, wple_ref, png_ref, wpg_ref, fg_ref,
                o_ref, *, final):
    ycat = jnp.concatenate([yc_ref[...], yr_ref[...], ym_ref[...]], axis=-1)
    h = h_ref[...] + jnp.dot(ycat, wo_ref[...], preferred_element_type=F32)
    gate = jax.nn.sigmoid(jnp.dot(_rms(h, png_ref[...]).astype(BF16), wpg_ref[...],
                                  preferred_element_type=F32))
    h = h + jnp.dot(p_ref[...].astype(BF16), wple_ref[...], preferred_element_type=F32) * gate
    if final:
        h = _rms(h, fg_ref[...])
    o_ref[...] = h


def _out_call(h, yc, yr, ym, p, layer, lw, final_g, final):
    t = h.shape[0]
    row = lambda w: pl.BlockSpec((TM, w), lambda i: (i, 0))
    consts_a = [lw["wo"]]
    consts_b = [lw["wple"], lw["png"], lw["wpg"], final_g]
    return pl.pallas_call(
        functools.partial(_out_kernel, final=final),
        grid=(t // TM,),
        in_specs=[row(D_MODEL), row(CONV_WIDTH), row(RWKV_WIDTH), row(MLA_WIDTH)]
        + [_const_spec(c.shape) for c in consts_a]
        + [pl.BlockSpec((None, TM, PLE_DIM), lambda i: (layer, i, 0))]
        + [_const_spec(c.shape) for c in consts_b],
        out_specs=row(D_MODEL),
        out_shape=jax.ShapeDtypeStruct((t, D_MODEL), F32),
        compiler_params=pltpu.CompilerParams(dimension_semantics=("arbitrary",),
                                             vmem_limit_bytes=VMEM_LIMIT),
        name="out",
    )(h, yc, yr, ym, *consts_a, p, *consts_b)


def _swap_halves(w):
    half = w.shape[-1] // 2
    return jnp.concatenate([w[..., half:], w[..., :half]], axis=-1)


def _prep_layer(i, norm_mix_g, w_in, conv_w, rwkv_mu, rwkv_w0, rwkv_w2, rwkv_a0, rwkv_a2, rwkv_kk,
                rwkv_ka, rwkv_rk, rwkv_gn_w, rwkv_gn_b, mla_q_norm_g, mla_w_qb, mla_kv_norm_g,
                mla_w_kvb, w_out, ple_w, ple_norm_g, ple_gate_w):
    w = w_in[i]
    o_rs = 4 * CONV_WIDTH
    o_rg = o_rs + 3 * RWKV_WIDTH + 2 * LORA
    o_qa = o_rg + RWKV_WIDTH
    o_kva = o_qa + Q_LORA_RANK
    o_kr = o_kva + KV_LORA_RANK
    o_mg = o_kr + QK_ROPE_DIM
    zeros = jnp.zeros((D_MODEL, QK_ROPE_DIM), w.dtype)
    krope = w[:, o_kr:o_mg]
    win = jnp.concatenate([
        w[:, :o_rs], w[:, o_rs:o_rs + 3 * RWKV_WIDTH], w[:, o_rg:o_qa], w[:, o_mg:],
        w[:, o_qa:o_kva], w[:, o_kva:o_kr], w[:, o_rs + 3 * RWKV_WIDTH:o_rg],
        krope, zeros, _swap_halves(krope), zeros], axis=1).astype(BF16)

    wl = jnp.zeros((2 * LORA, 2 * RWKV_WIDTH), F32)
    wl = wl.at[:LORA, :RWKV_WIDTH].set(rwkv_w2[i]).at[LORA:, RWKV_WIDTH:].set(rwkv_a2[i]).astype(BF16)
    head = np.arange(RWKV_WIDTH) // RWKV_HEAD_DIM
    seg = jnp.asarray(head[:, None] == head[None, :], BF16)

    wqb = mla_w_qb[i].reshape(Q_LORA_RANK, MLA_HEADS, QK_NOPE_DIM + QK_ROPE_DIM)
    zq = jnp.zeros((Q_LORA_RANK, MLA_HEADS, QK_ROPE_DIM), F32)
    wq = jnp.concatenate([wqb, zq], axis=-1).reshape(Q_LORA_RANK, MLA_HEADS * QK_PAD).astype(BF16)
    wqs = jnp.concatenate([_swap_halves(wqb[..., QK_NOPE_DIM:]), zq], axis=-1)
    wqs = wqs.reshape(Q_LORA_RANK, MLA_HEADS * LANE).astype(BF16)
    wkvb = mla_w_kvb[i].reshape(KV_LORA_RANK, MLA_HEADS, QK_NOPE_DIM + V_HEAD_DIM)
    wkv = jnp.concatenate([wkvb[..., :QK_NOPE_DIM].reshape(KV_LORA_RANK, -1),
                           wkvb[..., QK_NOPE_DIM:].reshape(KV_LORA_RANK, -1)], axis=1).astype(BF16)
    row = lambda x: x.reshape(1, -1)
    hd = lambda x: x.reshape(RWKV_HEADS, RWKV_HEAD_DIM)
    return dict(
        ng=row(norm_mix_g[i]), win=win, cw=conv_w[i],
        mu_rkv=row(rwkv_mu[i, :3 * RWKV_WIDTH]), mu_wa=row(rwkv_mu[i, 3 * RWKV_WIDTH:]),
        w0=row(rwkv_w0[i]), a0=row(rwkv_a0[i]), wl=wl, kk=row(rwkv_kk[i]), ka=row(rwkv_ka[i]), seg=seg,
        qg=row(mla_q_norm_g[i]), wq=wq, wqs=wqs, kvg=row(mla_kv_norm_g[i]), wkv=wkv,
        rk=row(rwkv_rk[i]), gnw=hd(rwkv_gn_w[i]), gnb=hd(rwkv_gn_b[i]),
        wo=w_out[i].astype(BF16), wple=ple_w[i].astype(BF16), png=row(ple_norm_g[i]),
        wpg=ple_gate_w[i].astype(BF16))


def kernel(x, p, positions, norm_mix_g, w_in, conv_w, rwkv_mu, rwkv_w0, rwkv_w2, rwkv_a0, rwkv_a2,
           rwkv_kk, rwkv_ka, rwkv_rk, rwkv_gn_w, rwkv_gn_b, mla_q_norm_g, mla_w_qb, mla_kv_norm_g,
           mla_w_kvb, w_out, ple_w, ple_norm_g, ple_gate_w, final_norm_g):
    batch, seq, _ = x.shape
    depth = w_in.shape[0]
    t = batch * seq

    inv_freq = 1.0 / (ROPE_THETA ** (jnp.arange(0, QK_ROPE_DIM, 2, dtype=F32) / QK_ROPE_DIM))
    ang = positions.astype(F32).reshape(t, 1) * inv_freq
    cos, sin = jnp.cos(ang), jnp.sin(ang)
    pad = jnp.zeros((t, LANE - QK_ROPE_DIM), F32)
    ck = jnp.concatenate([cos, cos, pad], axis=1)
    sk = jnp.concatenate([-sin, sin, pad], axis=1)

    h = x.reshape(t, D_MODEL)
    p = p.reshape(depth, t, PLE_DIM)
    final_g = final_norm_g.reshape(1, D_MODEL)
    for i in range(depth):
        lw = _prep_layer(i, norm_mix_g, w_in, conv_w, rwkv_mu, rwkv_w0, rwkv_w2, rwkv_a0, rwkv_a2,
                         rwkv_kk, rwkv_ka, rwkv_rk, rwkv_gn_w, rwkv_gn_b, mla_q_norm_g, mla_w_qb,
                         mla_kv_norm_g, mla_w_kvb, w_out, ple_w, ple_norm_g, ple_gate_w)
        (yconv, *hm, wc, gate_r, bg, q, kq, vq, gate_m) = _proj_call(h, lw, ck, sk, batch, seq)
        y_rwkv = _rwkv_call(hm, wc, gate_r, bg, lw, batch, seq).reshape(t, RWKV_WIDTH)
        y_mla = _attn_call(q, kq, vq, gate_m, batch, seq).reshape(t, MLA_WIDTH)
        h = _out_call(h, yconv, y_rwkv, y_mla, p, i, lw, final_g, final=(i == depth - 1))
    return h.reshape(batch, seq, D_MODEL)
```

```python
import functools
import math

import jax
import jax.numpy as jnp
import numpy as np
from jax import lax
from jax.experimental import pallas as pl
from jax.experimental.pallas import tpu as pltpu

F32 = jnp.float32
BF16 = jnp.bfloat16

D_MODEL = 1024
CHUNK = 64
PLE_DIM = 256
NORM_EPS = 1e-6
CONV_WIDTH = 256
RWKV_HEADS = 4
RWKV_HEAD_DIM = 64
RWKV_WIDTH = RWKV_HEADS * RWKV_HEAD_DIM
LORA = 64
DECAY_SCALE = math.exp(-0.5)
GN_EPS = 64e-5
MLA_HEADS = 4
QK_NOPE_DIM = 128
QK_ROPE_DIM = 64
V_HEAD_DIM = 128
Q_LORA_RANK = 384
KV_LORA_RANK = 256
MLA_WIDTH = MLA_HEADS * V_HEAD_DIM
ROPE_THETA = 10000.0
D_MIX = CONV_WIDTH + RWKV_WIDTH + MLA_WIDTH

C_CONV = 0
C_RKV = 1024
C_RG = 1792
C_MG = 2048
C_QA = 2560
C_KVA = 2944
C_WA = 3200
C_KR = 3328
C_KRS = 3456
D_IN_P = 3584
QK_PAD = 256

LANE = 128
HALO = 16
TM = 512
TQ = 512
RWKV_CHUNKS_PER_STEP = 8
VMEM_LIMIT = 56 * 1024 * 1024


def _dot(a, b):
    return jnp.dot(a.astype(BF16), b.astype(BF16), preferred_element_type=F32)


def _dot_nt(a, b):
    return lax.dot_general(a.astype(BF16), b.astype(BF16), (((1,), (1,)), ((), ())),
                           preferred_element_type=F32)


def _dot_tn(a, b):
    return lax.dot_general(a.astype(BF16), b.astype(BF16), (((0,), (0,)), ((), ())),
                           preferred_element_type=F32)


def _split2(x):
    hi = x.astype(BF16)
    return hi, (x - hi.astype(F32)).astype(BF16)


def _dot_mask(x, m01, left=False):
    hi, lo = _split2(x)
    if left:
        return jnp.dot(m01, hi, preferred_element_type=F32) + jnp.dot(m01, lo, preferred_element_type=F32)
    return jnp.dot(hi, m01, preferred_element_type=F32) + jnp.dot(lo, m01, preferred_element_type=F32)


def _rms(x, g):
    return x * lax.rsqrt(jnp.mean(x * x, axis=-1, keepdims=True) + NORM_EPS) * g


def _silu(x):
    return x * jax.nn.sigmoid(x)


def _proj_kernel(h_ref, halo_ref, ng_ref, win_ref, cw_ref, mu_rkv_ref, mu_wa_ref, w0_ref, a0_ref,
                 wl_ref, kk_ref, ka_ref, rk_ref, seg_ref, tril_ref, triu_ref, csum_ref,
                 qg_ref, wq_ref, wqs_ref, kvg_ref, wkv_ref, ck_ref, sk_ref,
                 yconv_ref, rt_ref, at_ref, bt_ref, kt_ref, bh_ref, kh_ref, v_ref, wc_ref, gr_ref, bg_ref,
                 q_ref, kq_ref, vq_ref, gm_ref,
                 conv_scr, rkv_scr, wa_scr, *, tiles_per_seq):
    hm_refs = (rt_ref, at_ref, bt_ref, kt_ref, bh_ref, kh_ref, v_ref)
    i = pl.program_id(0)
    tm = h_ref.shape[0]
    hx = jnp.concatenate([halo_ref[...], h_ref[...]], axis=0)
    u = _rms(hx, ng_ref[...]).astype(BF16)
    z = jnp.dot(u, win_ref[...], preferred_element_type=F32)

    row = lax.broadcasted_iota(jnp.int32, (HALO + tm, 1), 0)
    keep = row >= jnp.where(i % tiles_per_seq == 0, HALO, 0)

    zc = z[HALO:]
    conv_scr[...] = jnp.where(keep, z[:, C_CONV + 256:C_CONV + 512] * z[:, C_CONV + 512:C_CONV + 768], 0.0)
    cw = cw_ref[...]
    conv = (conv_scr[pl.ds(HALO - 2, tm), :] * cw[0:1, :]
            + conv_scr[pl.ds(HALO - 1, tm), :] * cw[1:2, :]
            + conv_scr[pl.ds(HALO, tm), :] * cw[2:3, :])
    yconv_ref[...] = (zc[:, C_CONV:C_CONV + 256] * conv * _silu(zc[:, C_CONV + 768:C_CONV + 1024])).astype(BF16)

    rkv_scr[...] = jnp.where(keep, z[:, C_RKV:C_RKV + 768], 0.0)
    wa_scr[...] = jnp.where(keep, z[:, C_WA:C_WA + 128], 0.0)
    cur = rkv_scr[pl.ds(HALO, tm), :]
    rkv = cur + (rkv_scr[pl.ds(HALO - 1, tm), :] - cur) * mu_rkv_ref[...]
    cur = wa_scr[pl.ds(HALO, tm), :]
    wa = cur + (wa_scr[pl.ds(HALO - 1, tm), :] - cur) * mu_wa_ref[...]
    r = rkv[:, 0:256]
    k = rkv[:, 256:512]
    v = rkv[:, 512:768]
    lane = lax.broadcasted_iota(jnp.int32, wa.shape, 1)
    lora_in = jnp.where(lane < LORA, jnp.tanh(wa), wa)
    lora = jnp.dot(lora_in.astype(BF16), wl_ref[...], preferred_element_type=F32)
    wlog = -DECAY_SCALE * jax.nn.sigmoid(w0_ref[...] + lora[:, :RWKV_WIDTH])
    a = jax.nn.sigmoid(a0_ref[...] + lora[:, RWKV_WIDTH:])
    kk = k * kk_ref[...]
    kk = kk * lax.rsqrt(_dot_mask(kk * kk, seg_ref[...]) + 1e-12)
    kmod = k * (1.0 + (a - 1.0) * ka_ref[...])
    zb = kk * a
    cw = _dot_mask(wlog, tril_ref[...], left=True)
    rest = _dot_mask(wlog, triu_ref[...], left=True)
    inv = jnp.exp(-cw)
    to_end = jnp.exp(rest)
    wc = jnp.exp(_dot_mask(wlog, csum_ref[...], left=True))
    ops = (r * jnp.exp(cw), -kk * jnp.exp(cw - wlog), zb * inv, kmod * inv, zb * to_end, kmod * to_end, v)
    for hd in range(RWKV_HEADS):
        sl = slice(hd * RWKV_HEAD_DIM, (hd + 1) * RWKV_HEAD_DIM)
        for ref, val in zip(hm_refs, ops):
            ref[0, hd] = val[:, sl].astype(BF16)
        wc_ref[0, hd] = wc[:, sl]
    gate_r = _silu(zc[:, C_RG:C_RG + 256])
    bonus = _dot_mask(r * kmod * rk_ref[...], seg_ref[...]) * v
    gr_ref[...] = gate_r
    bg_ref[...] = bonus * gate_r

    ck = ck_ref[...]
    sk = sk_ref[...]
    qn = _rms(zc[:, C_QA:C_QA + Q_LORA_RANK], qg_ref[...]).astype(BF16)
    qm = jnp.dot(qn, wq_ref[...], preferred_element_type=F32)
    qs = jnp.dot(qn, wqs_ref[...], preferred_element_type=F32)
    scale = math.log2(math.e) / math.sqrt(QK_NOPE_DIM + QK_ROPE_DIM)
    kvn = _rms(zc[:, C_KVA:C_KVA + KV_LORA_RANK], kvg_ref[...]).astype(BF16)
    kv = jnp.dot(kvn, wkv_ref[...], preferred_element_type=F32)
    kr = zc[:, C_KR:C_KR + LANE] * ck + zc[:, C_KRS:C_KRS + LANE] * sk
    for hd in range(MLA_HEADS):
        o = hd * QK_PAD
        qr = qm[:, o + LANE:o + QK_PAD] * ck + qs[:, hd * LANE:(hd + 1) * LANE] * sk
        q_ref[:, o:o + LANE] = (qm[:, o:o + LANE] * scale).astype(BF16)
        q_ref[:, o + LANE:o + QK_PAD] = (qr * scale).astype(BF16)
        kq_ref[:, o:o + LANE] = kv[:, hd * LANE:(hd + 1) * LANE].astype(BF16)
        kq_ref[:, o + LANE:o + QK_PAD] = kr.astype(BF16)
    vq_ref[0] = kv[:, MLA_HEADS * QK_NOPE_DIM:].T.astype(BF16)
    gm_ref[...] = _silu(zc[:, C_MG:C_MG + MLA_WIDTH])


def _const_spec(shape):
    return pl.BlockSpec(shape, lambda i: (0,) * len(shape))


def _proj_call(h, lw, ck, sk, batch, seq):
    t = h.shape[0]
    tiles_per_seq = seq // TM
    row = lambda w: pl.BlockSpec((TM, w), lambda i: (i, 0))
    hm = pl.BlockSpec((1, RWKV_HEADS, TM, RWKV_HEAD_DIM),
                      lambda i: (i // tiles_per_seq, 0, i % tiles_per_seq, 0))
    halo = pl.BlockSpec((HALO, D_MODEL), lambda i: (jnp.maximum(i * (TM // HALO) - 1, 0), 0))
    chunk_id = np.arange(TM) // CHUNK
    same = chunk_id[:, None] == chunk_id[None, :]
    pos = np.arange(TM)
    tril = jnp.asarray(same & (pos[:, None] >= pos[None, :]), BF16)
    triu = jnp.asarray(same & (pos[:, None] < pos[None, :]), BF16)
    csum = jnp.asarray(np.arange(TM // CHUNK)[:, None] == chunk_id[None, :], BF16)
    consts = [lw["ng"], lw["win"], lw["cw"], lw["mu_rkv"], lw["mu_wa"], lw["w0"], lw["a0"], lw["wl"],
              lw["kk"], lw["ka"], lw["rk"], lw["seg"], tril, triu, csum,
              lw["qg"], lw["wq"], lw["wqs"], lw["kvg"], lw["wkv"]]
    hm_shape = jax.ShapeDtypeStruct((batch, RWKV_HEADS, seq, RWKV_HEAD_DIM), BF16)
    wc_spec = pl.BlockSpec((1, RWKV_HEADS, TM // CHUNK, RWKV_HEAD_DIM),
                           lambda i: (i // tiles_per_seq, 0, i % tiles_per_seq, 0))
    out_shape = [jax.ShapeDtypeStruct((t, CONV_WIDTH), BF16)] + [hm_shape] * 7 + [
        jax.ShapeDtypeStruct((batch, RWKV_HEADS, seq // CHUNK, RWKV_HEAD_DIM), F32),
        jax.ShapeDtypeStruct((t, RWKV_WIDTH), F32),
        jax.ShapeDtypeStruct((t, RWKV_WIDTH), F32),
        jax.ShapeDtypeStruct((t, MLA_HEADS * QK_PAD), BF16),
        jax.ShapeDtypeStruct((t, MLA_HEADS * QK_PAD), BF16),
        jax.ShapeDtypeStruct((t // TM, MLA_WIDTH, TM), BF16),
        jax.ShapeDtypeStruct((t, MLA_WIDTH), F32),
    ]
    vt_spec = pl.BlockSpec((1, MLA_WIDTH, TM), lambda i: (i, 0, 0))
    out_specs = [row(CONV_WIDTH)] + [hm] * 7 + [wc_spec, row(RWKV_WIDTH), row(RWKV_WIDTH),
                                                  row(MLA_HEADS * QK_PAD), row(MLA_HEADS * QK_PAD),
                                                  vt_spec, row(MLA_WIDTH)]
    return pl.pallas_call(
        functools.partial(_proj_kernel, tiles_per_seq=tiles_per_seq),
        grid=(t // TM,),
        in_specs=[row(D_MODEL), halo] + [_const_spec(c.shape) for c in consts] + [row(LANE), row(LANE)],
        out_specs=out_specs,
        out_shape=out_shape,
        scratch_shapes=[pltpu.VMEM((HALO + TM, CONV_WIDTH), F32),
                        pltpu.VMEM((HALO + TM, 3 * RWKV_WIDTH), F32),
                        pltpu.VMEM((HALO + TM, 2 * LORA), F32)],
        compiler_params=pltpu.CompilerParams(dimension_semantics=("arbitrary",),
                                             vmem_limit_bytes=VMEM_LIMIT),
        name="proj",
    )(h, h, *consts, ck, sk)


def _rwkv_chunk_local(r_t, a_t, b_t, k_t, bh, kh, v, wc, tri_incl, tri_strict, eye):
    n = range(len(r_t))
    lhs = [jnp.concatenate([a_t[i], r_t[i]], axis=0) for i in n]
    ab = [_dot_nt(lhs[i], b_t[i]) for i in n]
    ak = [_dot_nt(lhs[i], k_t[i]) for i in n]
    aab = [jnp.where(tri_strict, ab[i][:CHUNK], 0.0).astype(BF16) for i in n]
    aak = [jnp.where(tri_strict, ak[i][:CHUNK], 0.0).astype(BF16) for i in n]
    arb = [jnp.where(tri_incl, ab[i][CHUNK:], 0.0).astype(BF16) for i in n]
    ark = [jnp.where(tri_incl, ak[i][CHUNK:], 0.0).astype(BF16) for i in n]

    x = [jnp.concatenate([a_t[i].astype(F32), _dot(aak[i], v[i])], axis=1) for i in n]
    p = aab
    for it in range(6):
        x = [x[i] + _dot(p[i], x[i]) for i in n]
        if it < 5:
            p = [_dot(p[i], p[i]).astype(BF16) for i in n]
    at = [x[i][:, :RWKV_HEAD_DIM].astype(BF16) for i in n]
    u = [x[i][:, RWKV_HEAD_DIM:].astype(BF16) for i in n]

    qp = [r_t[i].astype(F32) + _dot(arb[i], at[i]) for i in n]
    yloc = [_dot(arb[i], u[i]) + _dot(ark[i], v[i]) for i in n]
    mc = [jnp.where(eye, wc[i], 0.0) + _dot_tn(bh[i], at[i]) for i in n]
    nc = [_dot_tn(jnp.concatenate([bh[i], kh[i]], axis=0), jnp.concatenate([u[i], v[i]], axis=0)) for i in n]
    return qp, yloc, mc, nc


def _rwkv_kernel(rt_ref, at_ref, bt_ref, kt_ref, bh_ref, kh_ref, v_ref, wc_ref, gate_ref, bg_ref,
                 gnw_ref, gnb_ref, y_ref, state_ref):
    step = pl.program_id(1)

    @pl.when(step == 0)
    def _():
        state_ref[...] = jnp.zeros_like(state_ref)

    ri = lax.broadcasted_iota(jnp.int32, (CHUNK, CHUNK), 0)
    ci = lax.broadcasted_iota(jnp.int32, (CHUNK, CHUNK), 1)
    tri_incl = ri >= ci
    tri_strict = ri > ci
    eye = ri == ci
    items = [(c, hd) for c in range(RWKV_CHUNKS_PER_STEP) for hd in range(RWKV_HEADS)]
    load = lambda ref: [ref[0, hd, pl.ds(c * CHUNK, CHUNK), :] for c, hd in items]
    wc = [wc_ref[0, hd, pl.ds(step * RWKV_CHUNKS_PER_STEP + c, 1), :] for c, hd in items]
    qp, yloc, mc, nc = _rwkv_chunk_local(load(rt_ref), load(at_ref), load(bt_ref), load(kt_ref),
                                         load(bh_ref), load(kh_ref), load(v_ref), wc,
                                         tri_incl, tri_strict, eye)
    states = [state_ref[hd] for hd in range(RWKV_HEADS)]
    heads = range(RWKV_HEADS)
    for c in range(RWKV_CHUNKS_PER_STEP):
        rows = pl.ds(c * CHUNK, CHUNK)
        o = c * RWKV_HEADS
        y = [_dot(qp[o + hd], states[hd]) + yloc[o + hd] for hd in heads]
        states = [_dot(mc[o + hd], states[hd]) + nc[o + hd] for hd in heads]
        outs = []
        for hd in heads:
            mean = jnp.mean(y[hd], axis=-1, keepdims=True)
            yc = y[hd] - mean
            var = jnp.mean(yc * yc, axis=-1, keepdims=True)
            outs.append(yc * lax.rsqrt(var + GN_EPS) * gnw_ref[hd:hd + 1, :] + gnb_ref[hd:hd + 1, :])
        y_ref[0, rows, :] = (jnp.concatenate(outs, axis=1) * gate_ref[0, rows, :]
                             + bg_ref[0, rows, :]).astype(BF16)
    for hd in heads:
        state_ref[hd] = states[hd]


def _rwkv_call(hm, wc, gate, bg, lw, batch, seq):
    tc = RWKV_CHUNKS_PER_STEP * CHUNK
    hm_spec = pl.BlockSpec((1, RWKV_HEADS, tc, RWKV_HEAD_DIM), lambda b, c: (b, 0, c, 0))
    wc_spec = pl.BlockSpec((1, RWKV_HEADS, seq // CHUNK, RWKV_HEAD_DIM), lambda b, c: (b, 0, 0, 0))
    nat = pl.BlockSpec((1, tc, RWKV_WIDTH), lambda b, c: (b, c, 0))
    par = pl.BlockSpec((RWKV_HEADS, RWKV_HEAD_DIM), lambda b, c: (0, 0))
    return pl.pallas_call(
        _rwkv_kernel,
        grid=(batch, seq // tc),
        in_specs=[hm_spec] * 7 + [wc_spec, nat, nat, par, par],
        out_specs=nat,
        out_shape=jax.ShapeDtypeStruct((batch, seq, RWKV_WIDTH), BF16),
        scratch_shapes=[pltpu.VMEM((RWKV_HEADS, RWKV_HEAD_DIM, RWKV_HEAD_DIM), F32)],
        compiler_params=pltpu.CompilerParams(dimension_semantics=("arbitrary", "arbitrary"),
                                             vmem_limit_bytes=VMEM_LIMIT),
        name="rwkv",
    )(*hm, wc, gate.reshape(batch, seq, RWKV_WIDTH), bg.reshape(batch, seq, RWKV_WIDTH),
      lw["gnw"], lw["gnb"])


def _attn_kernel(q_ref, k_ref, vt_ref, g_ref, o_ref, m_scr, l_scr, acc_scr, s0_scr, s1_scr, c0_scr, c1_scr):
    tq = s0_scr.shape[1]
    n_q = q_ref.shape[1] // tq
    bufs = ((s0_scr, c0_scr), (s1_scr, c1_scr))
    tiles = [(i, j) for i in range(n_q) for j in range(i + 1)]

    def scores(i, j, slot):
        s_ref, c_ref = bufs[slot]
        k = k_ref[0, j * tq:(j + 1) * tq, :]
        q = q_ref[0, i * tq:(i + 1) * tq, :]
        s = lax.dot_general(k, q, (((1,), (1,)), ((), ())), preferred_element_type=F32)
        if i == j:
            kc = lax.broadcasted_iota(jnp.int32, s.shape, 0) // CHUNK
            qc = lax.broadcasted_iota(jnp.int32, s.shape, 1) // CHUNK
            s = jnp.where(kc <= qc, s, -1e30)
        s_ref[...] = s
        c_ref[...] = jnp.max(s, axis=0, keepdims=True)

    def accumulate(i, j, slot):
        s_ref, c_ref = bufs[slot]
        if j == 0:
            m_new = c_ref[...]
            p = jnp.exp2(s_ref[...] - m_new)
            l_new = jnp.sum(p, axis=0, keepdims=True)
            acc = jnp.dot(vt_ref[0, j], p.astype(BF16), preferred_element_type=F32)
        else:
            m_old = m_scr[...]
            m_new = jnp.maximum(m_old, c_ref[...])
            alpha = jnp.exp2(m_old - m_new)
            p = jnp.exp2(s_ref[...] - m_new)
            l_new = alpha * l_scr[...] + jnp.sum(p, axis=0, keepdims=True)
            acc = alpha * acc_scr[...] + jnp.dot(vt_ref[0, j], p.astype(BF16), preferred_element_type=F32)
        if j == i:
            rows = slice(i * tq, (i + 1) * tq)
            o_ref[0, rows, :] = ((acc / l_new).T * g_ref[0, rows, :]).astype(BF16)
        else:
            m_scr[...] = m_new
            l_scr[...] = l_new
            acc_scr[...] = acc

    scores(*tiles[0], 0)
    for t, (i, j) in enumerate(tiles):
        if t + 1 < len(tiles):
            scores(*tiles[t + 1], (t + 1) % 2)
        accumulate(i, j, t % 2)


def _attn_call(q, k, vt, gate, batch, seq):
    q = q.reshape(batch, seq, MLA_HEADS * QK_PAD)
    k = k.reshape(batch, seq, MLA_HEADS * QK_PAD)
    vt = vt.reshape(batch, seq // TQ, MLA_WIDTH, TQ)
    gate = gate.reshape(batch, seq, MLA_WIDTH)
    return pl.pallas_call(
        _attn_kernel,
        grid=(batch, MLA_HEADS),
        in_specs=[pl.BlockSpec((1, seq, QK_PAD), lambda b, h: (b, 0, h)),
                  pl.BlockSpec((1, seq, QK_PAD), lambda b, h: (b, 0, h)),
                  pl.BlockSpec((1, seq // TQ, V_HEAD_DIM, TQ), lambda b, h: (b, 0, h, 0)),
                  pl.BlockSpec((1, seq, V_HEAD_DIM), lambda b, h: (b, 0, h))],
        out_specs=pl.BlockSpec((1, seq, V_HEAD_DIM), lambda b, h: (b, 0, h)),
        out_shape=jax.ShapeDtypeStruct((batch, seq, MLA_WIDTH), BF16),
        scratch_shapes=[pltpu.VMEM((1, TQ), F32), pltpu.VMEM((1, TQ), F32),
                        pltpu.VMEM((V_HEAD_DIM, TQ), F32),
                        pltpu.VMEM((TQ, TQ), F32), pltpu.VMEM((TQ, TQ), F32),
                        pltpu.VMEM((1, TQ), F32), pltpu.VMEM((1, TQ), F32)],
        compiler_params=pltpu.CompilerParams(
            dimension_semantics=("arbitrary", "arbitrary"), vmem_limit_bytes=VMEM_LIMIT),
        name="attn",
    )(q, k, vt, gate)


def _out_kernel(h_ref, yc_ref, yr_ref, ym_ref, wo_ref, p_ref, wple_ref, png_ref, wpg_ref, fg_ref,
                o_ref, *, final):
    ycat = jnp.concatenate([yc_ref[...], yr_ref[...], ym_ref[...]], axis=-1)
    h = h_ref[...] + jnp.dot(ycat, wo_ref[...], preferred_element_type=F32)
    gate = jax.nn.sigmoid(jnp.dot(_rms(h, png_ref[...]).astype(BF16), wpg_ref[...],
                                  preferred_element_type=F32))
    h = h + jnp.dot(p_ref[...].astype(BF16), wple_ref[...], preferred_element_type=F32) * gate
    if final:
        h = _rms(h, fg_ref[...])
    o_ref[...] = h


def _out_call(h, yc, yr, ym, p, layer, lw, final_g, final):
    t = h.shape[0]
    row = lambda w: pl.BlockSpec((TM, w), lambda i: (i, 0))
    consts_a = [lw["wo"]]
    consts_b = [lw["wple"], lw["png"], lw["wpg"], final_g]
    return pl.pallas_call(
        functools.partial(_out_kernel, final=final),
        grid=(t // TM,),
        in_specs=[row(D_MODEL), row(CONV_WIDTH), row(RWKV_WIDTH), row(MLA_WIDTH)]
        + [_const_spec(c.shape) for c in consts_a]
        + [pl.BlockSpec((None, TM, PLE_DIM), lambda i: (layer, i, 0))]
        + [_const_spec(c.shape) for c in consts_b],
        out_specs=row(D_MODEL),
        out_shape=jax.ShapeDtypeStruct((t, D_MODEL), F32),
        compiler_params=pltpu.CompilerParams(dimension_semantics=("arbitrary",),
                                             vmem_limit_bytes=VMEM_LIMIT),
        name="out",
    )(h, yc, yr, ym, *consts_a, p, *consts_b)


def _swap_halves(w):
    half = w.shape[-1] // 2
    return jnp.concatenate([w[..., half:], w[..., :half]], axis=-1)


def _prep_layer(i, norm_mix_g, w_in, conv_w, rwkv_mu, rwkv_w0, rwkv_w2, rwkv_a0, rwkv_a2, rwkv_kk,
                rwkv_ka, rwkv_rk, rwkv_gn_w, rwkv_gn_b, mla_q_norm_g, mla_w_qb, mla_kv_norm_g,
                mla_w_kvb, w_out, ple_w, ple_norm_g, ple_gate_w):
    w = w_in[i]
    o_rs = 4 * CONV_WIDTH
    o_rg = o_rs + 3 * RWKV_WIDTH + 2 * LORA
    o_qa = o_rg + RWKV_WIDTH
    o_kva = o_qa + Q_LORA_RANK
    o_kr = o_kva + KV_LORA_RANK
    o_mg = o_kr + QK_ROPE_DIM
    zeros = jnp.zeros((D_MODEL, QK_ROPE_DIM), w.dtype)
    krope = w[:, o_kr:o_mg]
    win = jnp.concatenate([
        w[:, :o_rs], w[:, o_rs:o_rs + 3 * RWKV_WIDTH], w[:, o_rg:o_qa], w[:, o_mg:],
        w[:, o_qa:o_kva], w[:, o_kva:o_kr], w[:, o_rs + 3 * RWKV_WIDTH:o_rg],
        krope, zeros, _swap_halves(krope), zeros], axis=1).astype(BF16)

    wl = jnp.zeros((2 * LORA, 2 * RWKV_WIDTH), F32)
    wl = wl.at[:LORA, :RWKV_WIDTH].set(rwkv_w2[i]).at[LORA:, RWKV_WIDTH:].set(rwkv_a2[i]).astype(BF16)
    head = np.arange(RWKV_WIDTH) // RWKV_HEAD_DIM
    seg = jnp.asarray(head[:, None] == head[None, :], BF16)

    wqb = mla_w_qb[i].reshape(Q_LORA_RANK, MLA_HEADS, QK_NOPE_DIM + QK_ROPE_DIM)
    zq = jnp.zeros((Q_LORA_RANK, MLA_HEADS, QK_ROPE_DIM), F32)
    wq = jnp.concatenate([wqb, zq], axis=-1).reshape(Q_LORA_RANK, MLA_HEADS * QK_PAD).astype(BF16)
    wqs = jnp.concatenate([_swap_halves(wqb[..., QK_NOPE_DIM:]), zq], axis=-1)
    wqs = wqs.reshape(Q_LORA_RANK, MLA_HEADS * LANE).astype(BF16)
    wkvb = mla_w_kvb[i].reshape(KV_LORA_RANK, MLA_HEADS, QK_NOPE_DIM + V_HEAD_DIM)
    wkv = jnp.concatenate([wkvb[..., :QK_NOPE_DIM].reshape(KV_LORA_RANK, -1),
                           wkvb[..., QK_NOPE_DIM:].reshape(KV_LORA_RANK, -1)], axis=1).astype(BF16)
    row = lambda x: x.reshape(1, -1)
    hd = lambda x: x.reshape(RWKV_HEADS, RWKV_HEAD_DIM)
    return dict(
        ng=row(norm_mix_g[i]), win=win, cw=conv_w[i],
        mu_rkv=row(rwkv_mu[i, :3 * RWKV_WIDTH]), mu_wa=row(rwkv_mu[i, 3 * RWKV_WIDTH:]),
        w0=row(rwkv_w0[i]), a0=row(rwkv_a0[i]), wl=wl, kk=row(rwkv_kk[i]), ka=row(rwkv_ka[i]), seg=seg,
        qg=row(mla_q_norm_g[i]), wq=wq, wqs=wqs, kvg=row(mla_kv_norm_g[i]), wkv=wkv,
        rk=row(rwkv_rk[i]), gnw=hd(rwkv_gn_w[i]), gnb=hd(rwkv_gn_b[i]),
        wo=w_out[i].astype(BF16), wple=ple_w[i].astype(BF16), png=row(ple_norm_g[i]),
        wpg=ple_gate_w[i].astype(BF16))


def kernel(x, p, positions, norm_mix_g, w_in, conv_w, rwkv_mu, rwkv_w0, rwkv_w2, rwkv_a0, rwkv_a2,
           rwkv_kk, rwkv_ka, rwkv_rk, rwkv_gn_w, rwkv_gn_b, mla_q_norm_g, mla_w_qb, mla_kv_norm_g,
           mla_w_kvb, w_out, ple_w, ple_norm_g, ple_gate_w, final_norm_g):
    batch, seq, _ = x.shape
    depth = w_in.shape[0]
    t = batch * seq

    inv_freq = 1.0 / (ROPE_THETA ** (jnp.arange(0, QK_ROPE_DIM, 2, dtype=F32) / QK_ROPE_DIM))
    ang = positions.astype(F32).reshape(t, 1) * inv_freq
    cos, sin = jnp.cos(ang), jnp.sin(ang)
    pad = jnp.zeros((t, LANE - QK_ROPE_DIM), F32)
    ck = jnp.concatenate([cos, cos, pad], axis=1)
    sk = jnp.concatenate([-sin, sin, pad], axis=1)

    h = x.reshape(t, D_MODEL)
    p = p.reshape(depth, t, PLE_DIM)
    final_g = final_norm_g.reshape(1, D_MODEL)
    for i in range(depth):
        lw = _prep_layer(i, norm_mix_g, w_in, conv_w, rwkv_mu, rwkv_w0, rwkv_w2, rwkv_a0, rwkv_a2,
                         rwkv_kk, rwkv_ka, rwkv_rk, rwkv_gn_w, rwkv_gn_b, mla_q_norm_g, mla_w_qb,
                         mla_kv_norm_g, mla_w_kvb, w_out, ple_w, ple_norm_g, ple_gate_w)
        (yconv, *hm, wc, gate_r, bg, q, kq, vq, gate_m) = _proj_call(h, lw, ck, sk, batch, seq)
        y_rwkv = _rwkv_call(hm, wc, gate_r, bg, lw, batch, seq).reshape(t, RWKV_WIDTH)
        y_mla = _attn_call(q, kq, vq, gate_m, batch, seq).reshape(t, MLA_WIDTH)
        h = _out_call(h, yconv, y_rwkv, y_mla, p, i, lw, final_g, final=(i == depth - 1))
    return h.reshape(batch, seq, D_MODEL)
```

```python
import functools
import math

import jax
import jax.numpy as jnp
import numpy as np
from jax import lax
from jax.experimental import pallas as pl
from jax.experimental.pallas import tpu as pltpu

F32 = jnp.float32
BF16 = jnp.bfloat16

D_MODEL = 1024
CHUNK = 64
PLE_DIM = 256
NORM_EPS = 1e-6
CONV_WIDTH = 256
RWKV_HEADS = 4
RWKV_HEAD_DIM = 64
RWKV_WIDTH = RWKV_HEADS * RWKV_HEAD_DIM
LORA = 64
DECAY_SCALE = math.exp(-0.5)
GN_EPS = 64e-5
MLA_HEADS = 4
QK_NOPE_DIM = 128
QK_ROPE_DIM = 64
V_HEAD_DIM = 128
Q_LORA_RANK = 384
KV_LORA_RANK = 256
MLA_WIDTH = MLA_HEADS * V_HEAD_DIM
ROPE_THETA = 10000.0
D_MIX = CONV_WIDTH + RWKV_WIDTH + MLA_WIDTH

C_CONV = 0
C_RKV = 1024
C_WA = 1792
C_RG = 1920
C_QA = 2176
C_KVA = 2560
C_MG = 2816
C_KR = 3328
D_IN_P = 3456
QK_PAD = 256

LANE = 128
HALO = 16
TM = 512
TQ = 512
RWKV_CHUNKS_PER_STEP = 8
VMEM_LIMIT = 56 * 1024 * 1024


def _dot(a, b):
    return jnp.dot(a.astype(BF16), b.astype(BF16), preferred_element_type=F32)


def _dot_nt(a, b):
    return lax.dot_general(a.astype(BF16), b.astype(BF16), (((1,), (1,)), ((), ())),
                           preferred_element_type=F32)


def _dot_tn(a, b):
    return lax.dot_general(a.astype(BF16), b.astype(BF16), (((0,), (0,)), ((), ())),
                           preferred_element_type=F32)


def _split2(x):
    hi = x.astype(BF16)
    return hi, (x - hi.astype(F32)).astype(BF16)


def _dot_mask(x, m01, left=False):
    hi, lo = _split2(x)
    if left:
        return jnp.dot(m01, hi, preferred_element_type=F32) + jnp.dot(m01, lo, preferred_element_type=F32)
    return jnp.dot(hi, m01, preferred_element_type=F32) + jnp.dot(lo, m01, preferred_element_type=F32)


def _chunk_cumsum(x):
    pos = lax.broadcasted_iota(jnp.int32, x.shape, 0) % CHUNK
    d = 1
    while d < CHUNK:
        x = x + jnp.where(pos >= d, pltpu.roll(x, d, axis=0), 0.0)
        d *= 2
    return x


def _swap_rope_halves(x):
    half = QK_ROPE_DIM // 2
    return pltpu.roll(x, half, axis=1) + pltpu.roll(x, LANE - half, axis=1)


def _rms(x, g):
    return x * lax.rsqrt(jnp.mean(x * x, axis=-1, keepdims=True) + NORM_EPS) * g


def _silu(x):
    return x * jax.nn.sigmoid(x)


def _proj_kernel(h_ref, halo_ref, ng_ref, win_ref, cw_ref, mu_rkv_ref, mu_wa_ref, w0_ref, a0_ref,
                 wl_ref, kk_ref, ka_ref, rk_ref, seg_ref, csum_ref,
                 qg_ref, wq_ref, kvg_ref, wkv_ref, ck_ref, sk_ref,
                 yconv_ref, rt_ref, at_ref, bt_ref, kt_ref, bh_ref, kh_ref, v_ref, wc_ref, gr_ref, bg_ref,
                 q_ref, kq_ref, vq_ref, gm_ref,
                 conv_scr, rkv_scr, wa_scr, *, tiles_per_seq):
    hm_refs = (rt_ref, at_ref, bt_ref, kt_ref, bh_ref, kh_ref, v_ref)
    i = pl.program_id(0)
    tm = h_ref.shape[0]
    hx = jnp.concatenate([halo_ref[...], h_ref[...]], axis=0)
    u = _rms(hx, ng_ref[...]).astype(BF16)
    z = jnp.dot(u, win_ref[...], preferred_element_type=F32)

    row = lax.broadcasted_iota(jnp.int32, (HALO + tm, 1), 0)
    keep = row >= jnp.where(i % tiles_per_seq == 0, HALO, 0)

    zc = z[HALO:]
    conv_scr[...] = jnp.where(keep, z[:, C_CONV + 256:C_CONV + 512] * z[:, C_CONV + 512:C_CONV + 768], 0.0)
    cw = cw_ref[...]
    conv = (conv_scr[pl.ds(HALO - 2, tm), :] * cw[0:1, :]
            + conv_scr[pl.ds(HALO - 1, tm), :] * cw[1:2, :]
            + conv_scr[pl.ds(HALO, tm), :] * cw[2:3, :])
    yconv_ref[...] = (zc[:, C_CONV:C_CONV + 256] * conv * _silu(zc[:, C_CONV + 768:C_CONV + 1024])).astype(BF16)

    rkv_scr[...] = jnp.where(keep, z[:, C_RKV:C_RKV + 768], 0.0)
    wa_scr[...] = jnp.where(keep, z[:, C_WA:C_WA + 128], 0.0)
    cur = rkv_scr[pl.ds(HALO, tm), :]
    rkv = cur + (rkv_scr[pl.ds(HALO - 1, tm), :] - cur) * mu_rkv_ref[...]
    cur = wa_scr[pl.ds(HALO, tm), :]
    wa = cur + (wa_scr[pl.ds(HALO - 1, tm), :] - cur) * mu_wa_ref[...]
    r = rkv[:, 0:256]
    k = rkv[:, 256:512]
    v = rkv[:, 512:768]
    lane = lax.broadcasted_iota(jnp.int32, wa.shape, 1)
    lora_in = jnp.where(lane < LORA, jnp.tanh(wa), wa)
    lora = jnp.dot(lora_in.astype(BF16), wl_ref[...], preferred_element_type=F32)
    wlog = -DECAY_SCALE * jax.nn.sigmoid(w0_ref[...] + lora[:, :RWKV_WIDTH])
    a = jax.nn.sigmoid(a0_ref[...] + lora[:, RWKV_WIDTH:])
    kk = k * kk_ref[...]
    kk = kk * lax.rsqrt(_dot_mask(kk * kk, seg_ref[...]) + 1e-12)
    kmod = k * (1.0 + (a - 1.0) * ka_ref[...])
    zb = kk * a
    cw = _chunk_cumsum(wlog)
    total = _dot_mask(wlog, csum_ref[...], left=True)
    nchunk = tm // CHUNK
    rest = jnp.broadcast_to(total[:, None, :], (nchunk, CHUNK, RWKV_WIDTH)).reshape(tm, RWKV_WIDTH) - cw
    inv = jnp.exp(-cw)
    to_end = jnp.exp(rest)
    wc = jnp.exp(total)
    ops = (r * jnp.exp(cw), -kk * jnp.exp(cw - wlog), zb * inv, kmod * inv, zb * to_end, kmod * to_end, v)
    for hd in range(RWKV_HEADS):
        sl = slice(hd * RWKV_HEAD_DIM, (hd + 1) * RWKV_HEAD_DIM)
        for ref, val in zip(hm_refs, ops):
            ref[0, hd] = val[:, sl].astype(BF16)
        wc_ref[0, hd] = wc[:, sl]
    gate_r = _silu(zc[:, C_RG:C_RG + 256])
    bonus = _dot_mask(r * kmod * rk_ref[...], seg_ref[...]) * v
    gr_ref[...] = gate_r
    bg_ref[...] = bonus * gate_r

    ck = ck_ref[...]
    sk = sk_ref[...]
    qn = _rms(zc[:, C_QA:C_QA + Q_LORA_RANK], qg_ref[...]).astype(BF16)
    qm = jnp.dot(qn, wq_ref[...], preferred_element_type=F32)
    scale = math.log2(math.e) / math.sqrt(QK_NOPE_DIM + QK_ROPE_DIM)
    kvn = _rms(zc[:, C_KVA:C_KVA + KV_LORA_RANK], kvg_ref[...]).astype(BF16)
    kv = jnp.dot(kvn, wkv_ref[...], preferred_element_type=F32)
    kx = zc[:, C_KR:C_KR + LANE]
    kr = kx * ck + _swap_rope_halves(kx) * sk
    for hd in range(MLA_HEADS):
        o = hd * QK_PAD
        qx = qm[:, o + LANE:o + QK_PAD]
        qr = qx * ck + _swap_rope_halves(qx) * sk
        q_ref[:, o:o + LANE] = (qm[:, o:o + LANE] * scale).astype(BF16)
        q_ref[:, o + LANE:o + QK_PAD] = (qr * scale).astype(BF16)
        kq_ref[:, o:o + LANE] = kv[:, hd * LANE:(hd + 1) * LANE].astype(BF16)
        kq_ref[:, o + LANE:o + QK_PAD] = kr.astype(BF16)
    vq_ref[0] = kv[:, MLA_HEADS * QK_NOPE_DIM:].T.astype(BF16)
    gm_ref[...] = _silu(zc[:, C_MG:C_MG + MLA_WIDTH])


def _const_spec(arr):
    return pl.BlockSpec(arr.shape, lambda *_: (0,) * arr.ndim)


def _layer_spec(arr, layer):
    zeros = (0,) * (arr.ndim - 1)
    return pl.BlockSpec((None,) + arr.shape[1:], lambda *_: (layer,) + zeros)


def _proj_call(h, lw, layer, ck, sk, batch, seq):
    t = h.shape[0]
    tiles_per_seq = seq // TM
    row = lambda w: pl.BlockSpec((TM, w), lambda i: (i, 0))
    hm = pl.BlockSpec((1, RWKV_HEADS, TM, RWKV_HEAD_DIM),
                      lambda i: (i // tiles_per_seq, 0, i % tiles_per_seq, 0))
    halo = pl.BlockSpec((HALO, D_MODEL), lambda i: (jnp.maximum(i * (TM // HALO) - 1, 0), 0))
    chunk_id = np.arange(TM) // CHUNK
    csum = jnp.asarray(np.arange(TM // CHUNK)[:, None] == chunk_id[None, :], BF16)
    head = np.arange(RWKV_WIDTH) // RWKV_HEAD_DIM
    seg = jnp.asarray(head[:, None] == head[None, :], BF16)
    per_layer = lambda *names: [(lw[n], _layer_spec(lw[n], layer)) for n in names]
    shared = lambda *arrs: [(a, _const_spec(a)) for a in arrs]
    consts = (per_layer("ng", "win", "cw", "mu_rkv", "mu_wa", "w0", "a0", "wl", "kk", "ka", "rk")
              + shared(seg, csum) + per_layer("qg", "wq", "kvg", "wkv"))
    hm_shape = jax.ShapeDtypeStruct((batch, RWKV_HEADS, seq, RWKV_HEAD_DIM), BF16)
    wc_spec = pl.BlockSpec((1, RWKV_HEADS, TM // CHUNK, RWKV_HEAD_DIM),
                           lambda i: (i // tiles_per_seq, 0, i % tiles_per_seq, 0))
    out_shape = [jax.ShapeDtypeStruct((t, CONV_WIDTH), BF16)] + [hm_shape] * 7 + [
        jax.ShapeDtypeStruct((batch, RWKV_HEADS, seq // CHUNK, RWKV_HEAD_DIM), F32),
        jax.ShapeDtypeStruct((t, RWKV_WIDTH), F32),
        jax.ShapeDtypeStruct((t, RWKV_WIDTH), F32),
        jax.ShapeDtypeStruct((t, MLA_HEADS * QK_PAD), BF16),
        jax.ShapeDtypeStruct((t, MLA_HEADS * QK_PAD), BF16),
        jax.ShapeDtypeStruct((t // TM, MLA_WIDTH, TM), BF16),
        jax.ShapeDtypeStruct((t, MLA_WIDTH), F32),
    ]
    vt_spec = pl.BlockSpec((1, MLA_WIDTH, TM), lambda i: (i, 0, 0))
    out_specs = [row(CONV_WIDTH)] + [hm] * 7 + [wc_spec, row(RWKV_WIDTH), row(RWKV_WIDTH),
                                                  row(MLA_HEADS * QK_PAD), row(MLA_HEADS * QK_PAD),
                                                  vt_spec, row(MLA_WIDTH)]
    return pl.pallas_call(
        functools.partial(_proj_kernel, tiles_per_seq=tiles_per_seq),
        grid=(t // TM,),
        in_specs=[row(D_MODEL), halo] + [s for _, s in consts] + [row(LANE), row(LANE)],
        out_specs=out_specs,
        out_shape=out_shape,
        scratch_shapes=[pltpu.VMEM((HALO + TM, CONV_WIDTH), F32),
                        pltpu.VMEM((HALO + TM, 3 * RWKV_WIDTH), F32),
                        pltpu.VMEM((HALO + TM, 2 * LORA), F32)],
        compiler_params=pltpu.CompilerParams(dimension_semantics=("arbitrary",),
                                             vmem_limit_bytes=VMEM_LIMIT),
        name="proj",
    )(h, h, *[a for a, _ in consts], ck, sk)


def _rwkv_chunk_local(r_t, a_t, b_t, k_t, bh, kh, v, wc, tri_incl, tri_strict, eye):
    n = range(len(r_t))
    lhs = [jnp.concatenate([a_t[i], r_t[i]], axis=0) for i in n]
    ab = [_dot_nt(lhs[i], b_t[i]) for i in n]
    ak = [_dot_nt(lhs[i], k_t[i]) for i in n]
    aab = [jnp.where(tri_strict, ab[i][:CHUNK], 0.0).astype(BF16) for i in n]
    aak = [jnp.where(tri_strict, ak[i][:CHUNK], 0.0).astype(BF16) for i in n]
    arb = [jnp.where(tri_incl, ab[i][CHUNK:], 0.0).astype(BF16) for i in n]
    ark = [jnp.where(tri_incl, ak[i][CHUNK:], 0.0).astype(BF16) for i in n]

    x = [jnp.concatenate([a_t[i].astype(F32), _dot(aak[i], v[i])], axis=1) for i in n]
    p = aab
    for it in range(6):
        x = [x[i] + _dot(p[i], x[i]) for i in n]
        if it < 5:
            p = [_dot(p[i], p[i]).astype(BF16) for i in n]
    at = [x[i][:, :RWKV_HEAD_DIM].astype(BF16) for i in n]
    u = [x[i][:, RWKV_HEAD_DIM:].astype(BF16) for i in n]

    qp = [r_t[i].astype(F32) + _dot(arb[i], at[i]) for i in n]
    yloc = [_dot(arb[i], u[i]) + _dot(ark[i], v[i]) for i in n]
    mc = [jnp.where(eye, wc[i], 0.0) + _dot_tn(bh[i], at[i]) for i in n]
    nc = [_dot_tn(jnp.concatenate([bh[i], kh[i]], axis=0), jnp.concatenate([u[i], v[i]], axis=0)) for i in n]
    return qp, yloc, mc, nc


def _rwkv_kernel(rt_ref, at_ref, bt_ref, kt_ref, bh_ref, kh_ref, v_ref, wc_ref, gate_ref, bg_ref,
                 gnw_ref, gnb_ref, y_ref, state_ref):
    step = pl.program_id(1)

    @pl.when(step == 0)
    def _():
        state_ref[...] = jnp.zeros_like(state_ref)

    ri = lax.broadcasted_iota(jnp.int32, (CHUNK, CHUNK), 0)
    ci = lax.broadcasted_iota(jnp.int32, (CHUNK, CHUNK), 1)
    tri_incl = ri >= ci
    tri_strict = ri > ci
    eye = ri == ci
    items = [(c, hd) for c in range(RWKV_CHUNKS_PER_STEP) for hd in range(RWKV_HEADS)]
    load = lambda ref: [ref[0, hd, pl.ds(c * CHUNK, CHUNK), :] for c, hd in items]
    wc = [wc_ref[0, hd, pl.ds(step * RWKV_CHUNKS_PER_STEP + c, 1), :] for c, hd in items]
    qp, yloc, mc, nc = _rwkv_chunk_local(load(rt_ref), load(at_ref), load(bt_ref), load(kt_ref),
                                         load(bh_ref), load(kh_ref), load(v_ref), wc,
                                         tri_incl, tri_strict, eye)
    states = [state_ref[hd] for hd in range(RWKV_HEADS)]
    heads = range(RWKV_HEADS)
    for c in range(RWKV_CHUNKS_PER_STEP):
        rows = pl.ds(c * CHUNK, CHUNK)
        o = c * RWKV_HEADS
        y = [_dot(qp[o + hd], states[hd]) + yloc[o + hd] for hd in heads]
        states = [_dot(mc[o + hd], states[hd]) + nc[o + hd] for hd in heads]
        outs = []
        for hd in heads:
            mean = jnp.mean(y[hd], axis=-1, keepdims=True)
            yc = y[hd] - mean
            var = jnp.mean(yc * yc, axis=-1, keepdims=True)
            outs.append(yc * lax.rsqrt(var + GN_EPS) * gnw_ref[hd:hd + 1, :] + gnb_ref[hd:hd + 1, :])
        y_ref[0, rows, :] = (jnp.concatenate(outs, axis=1) * gate_ref[0, rows, :]
                             + bg_ref[0, rows, :]).astype(BF16)
    for hd in heads:
        state_ref[hd] = states[hd]


def _rwkv_call(hm, wc, gate, bg, lw, layer, batch, seq):
    tc = RWKV_CHUNKS_PER_STEP * CHUNK
    hm_spec = pl.BlockSpec((1, RWKV_HEADS, tc, RWKV_HEAD_DIM), lambda b, c: (b, 0, c, 0))
    wc_spec = pl.BlockSpec((1, RWKV_HEADS, seq // CHUNK, RWKV_HEAD_DIM), lambda b, c: (b, 0, 0, 0))
    nat = pl.BlockSpec((1, tc, RWKV_WIDTH), lambda b, c: (b, c, 0))
    return pl.pallas_call(
        _rwkv_kernel,
        grid=(batch, seq // tc),
        in_specs=[hm_spec] * 7 + [wc_spec, nat, nat,
                                  _layer_spec(lw["gnw"], layer), _layer_spec(lw["gnb"], layer)],
        out_specs=nat,
        out_shape=jax.ShapeDtypeStruct((batch, seq, RWKV_WIDTH), BF16),
        scratch_shapes=[pltpu.VMEM((RWKV_HEADS, RWKV_HEAD_DIM, RWKV_HEAD_DIM), F32)],
        compiler_params=pltpu.CompilerParams(dimension_semantics=("arbitrary", "arbitrary"),
                                             vmem_limit_bytes=VMEM_LIMIT),
        name="rwkv",
    )(*hm, wc, gate.reshape(batch, seq, RWKV_WIDTH), bg.reshape(batch, seq, RWKV_WIDTH),
      lw["gnw"], lw["gnb"])


def _attn_kernel(q_ref, k_ref, vt_ref, g_ref, o_ref, m_scr, l_scr, acc_scr, s0_scr, s1_scr, c0_scr, c1_scr):
    tq = s0_scr.shape[1]
    n_q = q_ref.shape[1] // tq
    bufs = ((s0_scr, c0_scr), (s1_scr, c1_scr))
    tiles = [(i, j) for i in range(n_q) for j in range(i + 1)]

    def scores(i, j, slot):
        s_ref, c_ref = bufs[slot]
        k = k_ref[0, j * tq:(j + 1) * tq, :]
        q = q_ref[0, i * tq:(i + 1) * tq, :]
        s = lax.dot_general(k, q, (((1,), (1,)), ((), ())), preferred_element_type=F32)
        if i == j:
            kc = lax.broadcasted_iota(jnp.int32, s.shape, 0) // CHUNK
            qc = lax.broadcasted_iota(jnp.int32, s.shape, 1) // CHUNK
            s = jnp.where(kc <= qc, s, -1e30)
        s_ref[...] = s
        c_ref[...] = jnp.max(s, axis=0, keepdims=True)

    def accumulate(i, j, slot):
        s_ref, c_ref = bufs[slot]
        if j == 0:
            m_new = c_ref[...]
            p = jnp.exp2(s_ref[...] - m_new)
            l_new = jnp.sum(p, axis=0, keepdims=True)
            acc = jnp.dot(vt_ref[0, j], p.astype(BF16), preferred_element_type=F32)
        else:
            m_old = m_scr[...]
            m_new = jnp.maximum(m_old, c_ref[...])
            alpha = jnp.exp2(m_old - m_new)
            p = jnp.exp2(s_ref[...] - m_new)
            l_new = alpha * l_scr[...] + jnp.sum(p, axis=0, keepdims=True)
            acc = alpha * acc_scr[...] + jnp.dot(vt_ref[0, j], p.astype(BF16), preferred_element_type=F32)
        if j == i:
            rows = slice(i * tq, (i + 1) * tq)
            o_ref[0, rows, :] = ((acc / l_new).T * g_ref[0, rows, :]).astype(BF16)
        else:
            m_scr[...] = m_new
            l_scr[...] = l_new
            acc_scr[...] = acc

    scores(*tiles[0], 0)
    for t, (i, j) in enumerate(tiles):
        if t + 1 < len(tiles):
            scores(*tiles[t + 1], (t + 1) % 2)
        accumulate(i, j, t % 2)


def _attn_call(q, k, vt, gate, batch, seq):
    q = q.reshape(batch, seq, MLA_HEADS * QK_PAD)
    k = k.reshape(batch, seq, MLA_HEADS * QK_PAD)
    vt = vt.reshape(batch, seq // TQ, MLA_WIDTH, TQ)
    gate = gate.reshape(batch, seq, MLA_WIDTH)
    return pl.pallas_call(
        _attn_kernel,
        grid=(batch, MLA_HEADS),
        in_specs=[pl.BlockSpec((1, seq, QK_PAD), lambda b, h: (b, 0, h)),
                  pl.BlockSpec((1, seq, QK_PAD), lambda b, h: (b, 0, h)),
                  pl.BlockSpec((1, seq // TQ, V_HEAD_DIM, TQ), lambda b, h: (b, 0, h, 0)),
                  pl.BlockSpec((1, seq, V_HEAD_DIM), lambda b, h: (b, 0, h))],
        out_specs=pl.BlockSpec((1, seq, V_HEAD_DIM), lambda b, h: (b, 0, h)),
        out_shape=jax.ShapeDtypeStruct((batch, seq, MLA_WIDTH), BF16),
        scratch_shapes=[pltpu.VMEM((1, TQ), F32), pltpu.VMEM((1, TQ), F32),
                        pltpu.VMEM((V_HEAD_DIM, TQ), F32),
                        pltpu.VMEM((TQ, TQ), F32), pltpu.VMEM((TQ, TQ), F32),
                        pltpu.VMEM((1, TQ), F32), pltpu.VMEM((1, TQ), F32)],
        compiler_params=pltpu.CompilerParams(
            dimension_semantics=("arbitrary", "arbitrary"), vmem_limit_bytes=VMEM_LIMIT),
        name="attn",
    )(q, k, vt, gate)


def _out_kernel(h_ref, yc_ref, yr_ref, ym_ref, wo_ref, p_ref, wple_ref, png_ref, wpg_ref, fg_ref,
                o_ref, *, final):
    ycat = jnp.concatenate([yc_ref[...], yr_ref[...], ym_ref[...]], axis=-1)
    h = h_ref[...] + jnp.dot(ycat, wo_ref[...], preferred_element_type=F32)
    gate = jax.nn.sigmoid(jnp.dot(_rms(h, png_ref[...]).astype(BF16), wpg_ref[...],
                                  preferred_element_type=F32))
    h = h + jnp.dot(p_ref[...].astype(BF16), wple_ref[...], preferred_element_type=F32) * gate
    if final:
        h = _rms(h, fg_ref[...])
    o_ref[...] = h


def _out_call(h, yc, yr, ym, p, layer, lw, final_g, final):
    t = h.shape[0]
    row = lambda w: pl.BlockSpec((TM, w), lambda i: (i, 0))
    return pl.pallas_call(
        functools.partial(_out_kernel, final=final),
        grid=(t // TM,),
        in_specs=[row(D_MODEL), row(CONV_WIDTH), row(RWKV_WIDTH), row(MLA_WIDTH),
                  _layer_spec(lw["wo"], layer),
                  pl.BlockSpec((None, TM, PLE_DIM), lambda i: (layer, i, 0)),
                  _layer_spec(lw["wple"], layer), _layer_spec(lw["png"], layer),
                  _layer_spec(lw["wpg"], layer), _const_spec(final_g)],
        out_specs=row(D_MODEL),
        out_shape=jax.ShapeDtypeStruct((t, D_MODEL), F32),
        compiler_params=pltpu.CompilerParams(dimension_semantics=("arbitrary",),
                                             vmem_limit_bytes=VMEM_LIMIT),
        name="out",
    )(h, yc, yr, ym, lw["wo"], p, lw["wple"], lw["png"], lw["wpg"], final_g)


def _swap_halves(w):
    half = w.shape[-1] // 2
    return jnp.concatenate([w[..., half:], w[..., :half]], axis=-1)


def _prep(norm_mix_g, w_in, conv_w, rwkv_mu, rwkv_w0, rwkv_w2, rwkv_a0, rwkv_a2, rwkv_kk, rwkv_ka, rwkv_rk,
          rwkv_gn_w, rwkv_gn_b, mla_q_norm_g, mla_w_qb, mla_kv_norm_g, mla_w_kvb, w_out, ple_w, ple_norm_g,
          ple_gate_w):
    depth = w_in.shape[0]
    w = w_in.astype(BF16)
    o_kr = C_MG
    o_mg = o_kr + QK_ROPE_DIM
    zeros = jnp.zeros((depth, D_MODEL, QK_ROPE_DIM), BF16)
    win = jnp.concatenate([w[:, :, :o_kr], w[:, :, o_mg:], w[:, :, o_kr:o_mg], zeros], axis=2)

    zl = jnp.zeros((depth, LORA, RWKV_WIDTH), F32)
    wl = jnp.concatenate([jnp.concatenate([rwkv_w2, zl], axis=2),
                          jnp.concatenate([zl, rwkv_a2], axis=2)], axis=1).astype(BF16)

    wqb = mla_w_qb.reshape(depth, Q_LORA_RANK, MLA_HEADS, QK_NOPE_DIM + QK_ROPE_DIM)
    zq = jnp.zeros((depth, Q_LORA_RANK, MLA_HEADS, QK_ROPE_DIM), F32)
    wq = jnp.concatenate([wqb, zq], axis=-1).reshape(depth, Q_LORA_RANK, MLA_HEADS * QK_PAD).astype(BF16)
    wkvb = mla_w_kvb.reshape(depth, KV_LORA_RANK, MLA_HEADS, QK_NOPE_DIM + V_HEAD_DIM)
    wkv = jnp.concatenate([wkvb[..., :QK_NOPE_DIM].reshape(depth, KV_LORA_RANK, -1),
                           wkvb[..., QK_NOPE_DIM:].reshape(depth, KV_LORA_RANK, -1)], axis=2).astype(BF16)
    row = lambda x: x.reshape(depth, 1, -1)
    hd = lambda x: x.reshape(depth, RWKV_HEADS, RWKV_HEAD_DIM)
    return dict(
        ng=row(norm_mix_g), win=win, cw=conv_w,
        mu_rkv=row(rwkv_mu[:, :3 * RWKV_WIDTH]), mu_wa=row(rwkv_mu[:, 3 * RWKV_WIDTH:]),
        w0=row(rwkv_w0), a0=row(rwkv_a0), wl=wl, kk=row(rwkv_kk), ka=row(rwkv_ka),
        qg=row(mla_q_norm_g), wq=wq, kvg=row(mla_kv_norm_g), wkv=wkv,
        rk=row(rwkv_rk), gnw=hd(rwkv_gn_w), gnb=hd(rwkv_gn_b),
        wo=w_out.astype(BF16), wple=ple_w.astype(BF16), png=row(ple_norm_g), wpg=ple_gate_w.astype(BF16))


def kernel(x, p, positions, norm_mix_g, w_in, conv_w, rwkv_mu, rwkv_w0, rwkv_w2, rwkv_a0, rwkv_a2,
           rwkv_kk, rwkv_ka, rwkv_rk, rwkv_gn_w, rwkv_gn_b, mla_q_norm_g, mla_w_qb, mla_kv_norm_g,
           mla_w_kvb, w_out, ple_w, ple_norm_g, ple_gate_w, final_norm_g):
    batch, seq, _ = x.shape
    depth = w_in.shape[0]
    t = batch * seq

    half = QK_ROPE_DIM // 2
    inv_freq = 1.0 / (ROPE_THETA ** (jnp.arange(0, QK_ROPE_DIM, 2, dtype=F32) / QK_ROPE_DIM))
    inv_tile = jnp.concatenate([inv_freq, inv_freq, jnp.zeros((LANE - QK_ROPE_DIM,), F32)])
    lane = np.arange(LANE)
    ang = positions.astype(F32).reshape(t, 1) * inv_tile
    ck = jnp.where(lane < QK_ROPE_DIM, jnp.cos(ang), 0.0)
    sk = jnp.sin(ang) * jnp.asarray(np.where(lane < half, -1.0, 1.0), F32)

    h = x.reshape(t, D_MODEL)
    p = p.reshape(depth, t, PLE_DIM)
    final_g = final_norm_g.reshape(1, D_MODEL)
    lw = _prep(norm_mix_g, w_in, conv_w, rwkv_mu, rwkv_w0, rwkv_w2, rwkv_a0, rwkv_a2, rwkv_kk, rwkv_ka,
               rwkv_rk, rwkv_gn_w, rwkv_gn_b, mla_q_norm_g, mla_w_qb, mla_kv_norm_g, mla_w_kvb, w_out,
               ple_w, ple_norm_g, ple_gate_w)
    for i in range(depth):
        (yconv, *hm, wc, gate_r, bg, q, kq, vq, gate_m) = _proj_call(h, lw, i, ck, sk, batch, seq)
        y_rwkv = _rwkv_call(hm, wc, gate_r, bg, lw, i, batch, seq).reshape(t, RWKV_WIDTH)
        y_mla = _attn_call(q, kq, vq, gate_m, batch, seq).reshape(t, MLA_WIDTH)
        h = _out_call(h, yconv, y_rwkv, y_mla, p, i, lw, final_g, final=(i == depth - 1))
    return h.reshape(batch, seq, D_MODEL)
```

```python
import functools
import math

import jax
import jax.numpy as jnp
import numpy as np
from jax import lax
from jax.experimental import pallas as pl
from jax.experimental.pallas import tpu as pltpu

F32 = jnp.float32
BF16 = jnp.bfloat16

D_MODEL = 1024
CHUNK = 64
PLE_DIM = 256
NORM_EPS = 1e-6
CONV_WIDTH = 256
RWKV_HEADS = 4
RWKV_HEAD_DIM = 64
RWKV_WIDTH = RWKV_HEADS * RWKV_HEAD_DIM
LORA = 64
DECAY_SCALE = math.exp(-0.5)
GN_EPS = 64e-5
MLA_HEADS = 4
QK_NOPE_DIM = 128
QK_ROPE_DIM = 64
V_HEAD_DIM = 128
Q_LORA_RANK = 384
KV_LORA_RANK = 256
MLA_WIDTH = MLA_HEADS * V_HEAD_DIM
ROPE_THETA = 10000.0
D_MIX = CONV_WIDTH + RWKV_WIDTH + MLA_WIDTH

C_CONV = 0
C_RKV = 1024
C_WA = 1792
C_RG = 1920
C_QA = 2176
C_KVA = 2560
C_MG = 2816
C_KR = 3328
D_IN_P = 3456
QK_PAD = 256

LANE = 128
BF16_ROWS = 16
HALO = BF16_ROWS
TM = 512
TQ = 512
RWKV_CHUNKS_PER_STEP = 8
VMEM_LIMIT = 56 * 1024 * 1024


def _dot(a, b):
    return jnp.dot(a.astype(BF16), b.astype(BF16), preferred_element_type=F32)


def _dot_nt(a, b):
    return lax.dot_general(a.astype(BF16), b.astype(BF16), (((1,), (1,)), ((), ())),
                           preferred_element_type=F32)


def _dot_tn(a, b):
    return lax.dot_general(a.astype(BF16), b.astype(BF16), (((0,), (0,)), ((), ())),
                           preferred_element_type=F32)


def _split2(x):
    hi = x.astype(BF16)
    return hi, (x - hi.astype(F32)).astype(BF16)


def _dot_mask(x, m01, left=False):
    hi, lo = _split2(x)
    if left:
        return jnp.dot(m01, hi, preferred_element_type=F32) + jnp.dot(m01, lo, preferred_element_type=F32)
    return jnp.dot(hi, m01, preferred_element_type=F32) + jnp.dot(lo, m01, preferred_element_type=F32)


def _chunk_cumsum(x):
    pos = lax.broadcasted_iota(jnp.int32, x.shape, 0) % CHUNK
    d = 1
    while d < CHUNK:
        x = x + jnp.where(pos >= d, pltpu.roll(x, d, axis=0), 0.0)
        d *= 2
    return x


def _swap_rope_halves(x):
    half = QK_ROPE_DIM // 2
    return pltpu.roll(x, half, axis=1) + pltpu.roll(x, LANE - half, axis=1)


def _rms(x, g):
    return x * lax.rsqrt(jnp.mean(x * x, axis=-1, keepdims=True) + NORM_EPS) * g


def _silu(x):
    return x * jax.nn.sigmoid(x)


def _proj_kernel(h_ref, halo_ref, ng_ref, wa_ref, wb_ref, cwt_ref, mu_rkv_ref, mu_wa_ref, w0_ref, a0_ref,
                 wl_ref, kk_ref, ka_ref, rk_ref, seg_ref, csum_ref, qg_ref, wq_ref, kvg_ref, wkv_ref,
                 ck_ref, sk_ref,
                 yconv_ref, rt_ref, at_ref, bt_ref, kt_ref, bh_ref, kh_ref, v_ref, wc_ref, gr_ref, bg_ref,
                 q_ref, kq_ref, vq_ref, gm_ref, z_scr, *, tiles_per_seq):
    hm_refs = (rt_ref, at_ref, bt_ref, kt_ref, bh_ref, kh_ref, v_ref)
    tm = h_ref.shape[0]
    hx = jnp.concatenate([halo_ref[...], h_ref[...]], axis=0)
    u = _rms(hx, ng_ref[...]).astype(BF16)
    z_scr[:, :C_MG] = jnp.dot(u, wa_ref[...], preferred_element_type=F32)
    z_scr[:, C_MG:] = jnp.dot(u, wb_ref[...], preferred_element_type=F32)

    first = pl.program_id(0) % tiles_per_seq == 0
    row = lax.broadcasted_iota(jnp.int32, (tm, 1), 0)

    def cols(c, w, back=0):
        x = z_scr[pl.ds(HALO - back, tm), c:c + w]
        return jnp.where(row >= jnp.where(first, back, 0), x, 0.0) if back else x

    cwt = cwt_ref[...]
    conv = sum(cols(C_CONV + 256, 256, back) * cols(C_CONV + 512, 256, back) * cwt[2 - back:3 - back, :]
               for back in range(3))
    yconv_ref[...] = (cols(C_CONV, 256) * conv * _silu(cols(C_CONV + 768, 256))).astype(BF16)

    cur = cols(C_RKV, 768)
    rkv = cur + (cols(C_RKV, 768, 1) - cur) * mu_rkv_ref[...]
    cur = cols(C_WA, 128)
    wa = cur + (cols(C_WA, 128, 1) - cur) * mu_wa_ref[...]
    r = rkv[:, 0:256]
    k = rkv[:, 256:512]
    v = rkv[:, 512:768]
    lane = lax.broadcasted_iota(jnp.int32, wa.shape, 1)
    lora_in = jnp.where(lane < LORA, jnp.tanh(wa), wa)
    lora = jnp.dot(lora_in.astype(BF16), wl_ref[...], preferred_element_type=F32)
    wlog = -DECAY_SCALE * jax.nn.sigmoid(w0_ref[...] + lora[:, :RWKV_WIDTH])
    a = jax.nn.sigmoid(a0_ref[...] + lora[:, RWKV_WIDTH:])
    kk = k * kk_ref[...]
    kk = kk * lax.rsqrt(_dot_mask(kk * kk, seg_ref[...]) + 1e-12)
    kmod = k * (1.0 + (a - 1.0) * ka_ref[...])
    zb = kk * a
    cw = _chunk_cumsum(wlog)
    total = _dot_mask(wlog, csum_ref[...], left=True)
    nchunk = tm // CHUNK
    rest = jnp.broadcast_to(total[:, None, :], (nchunk, CHUNK, RWKV_WIDTH)).reshape(tm, RWKV_WIDTH) - cw
    inv = jnp.exp(-cw)
    to_end = jnp.exp(rest)
    wc = jnp.exp(total)
    ops = (r * jnp.exp(cw), -kk * jnp.exp(cw - wlog), zb * inv, kmod * inv, zb * to_end, kmod * to_end, v)
    for hd in range(RWKV_HEADS):
        sl = slice(hd * RWKV_HEAD_DIM, (hd + 1) * RWKV_HEAD_DIM)
        for ref, val in zip(hm_refs, ops):
            ref[0, hd] = val[:, sl].astype(BF16)
        wc_ref[0, hd] = wc[:, sl]
    gate_r = _silu(cols(C_RG, 256))
    bonus = _dot_mask(r * kmod * rk_ref[...], seg_ref[...]) * v
    gr_ref[...] = gate_r
    bg_ref[...] = bonus * gate_r

    ck = ck_ref[...]
    sk = sk_ref[...]
    qn = _rms(cols(C_QA, Q_LORA_RANK), qg_ref[...]).astype(BF16)
    qm = jnp.dot(qn, wq_ref[...], preferred_element_type=F32)
    scale = math.log2(math.e) / math.sqrt(QK_NOPE_DIM + QK_ROPE_DIM)
    kvn = _rms(cols(C_KVA, KV_LORA_RANK), kvg_ref[...]).astype(BF16)
    kv = jnp.dot(kvn, wkv_ref[...], preferred_element_type=F32)
    kx = cols(C_KR, LANE)
    kr = kx * ck + _swap_rope_halves(kx) * sk
    for hd in range(MLA_HEADS):
        o = hd * QK_PAD
        qx = qm[:, o + LANE:o + QK_PAD]
        qr = qx * ck + _swap_rope_halves(qx) * sk
        q_ref[:, o:o + LANE] = (qm[:, o:o + LANE] * scale).astype(BF16)
        q_ref[:, o + LANE:o + QK_PAD] = (qr * scale).astype(BF16)
        kq_ref[:, o:o + LANE] = kv[:, hd * LANE:(hd + 1) * LANE].astype(BF16)
        kq_ref[:, o + LANE:o + QK_PAD] = kr.astype(BF16)
    vq_ref[0] = kv[:, MLA_HEADS * QK_NOPE_DIM:].T.astype(BF16)
    gm_ref[...] = _silu(cols(C_MG, MLA_WIDTH))


def _const_spec(arr):
    return pl.BlockSpec(arr.shape, lambda *_: (0,) * arr.ndim)


def _layer_spec(arr, layer):
    zeros = (0,) * (arr.ndim - 1)
    return pl.BlockSpec((None,) + arr.shape[1:], lambda *_: (layer,) + zeros)


def _proj_call(h, lw, layer, ck, sk, batch, seq):
    t = h.shape[0]
    tiles_per_seq = seq // TM
    row = lambda w: pl.BlockSpec((TM, w), lambda i: (i, 0))
    hm = pl.BlockSpec((1, RWKV_HEADS, TM, RWKV_HEAD_DIM),
                      lambda i: (i // tiles_per_seq, 0, i % tiles_per_seq, 0))
    halo = pl.BlockSpec((HALO, D_MODEL), lambda i: (jnp.maximum(i * (TM // HALO) - 1, 0), 0))
    chunk_id = np.arange(TM) // CHUNK
    csum = jnp.asarray(np.arange(TM // CHUNK)[:, None] == chunk_id[None, :], BF16)
    head = np.arange(RWKV_WIDTH) // RWKV_HEAD_DIM
    seg = jnp.asarray(head[:, None] == head[None, :], BF16)
    per_layer = lambda *names: [(lw[n], _layer_spec(lw[n], layer)) for n in names]
    shared = lambda *arrs: [(a, _const_spec(a)) for a in arrs]
    consts = (per_layer("ng", "win_a", "win_b", "cw", "mu_rkv", "mu_wa", "w0", "a0", "wl", "kk", "ka", "rk")
              + shared(seg, csum) + per_layer("qg", "wq", "kvg", "wkv"))
    hm_shape = jax.ShapeDtypeStruct((batch, RWKV_HEADS, seq, RWKV_HEAD_DIM), BF16)
    wc_spec = pl.BlockSpec((1, RWKV_HEADS, TM // CHUNK, RWKV_HEAD_DIM),
                           lambda i: (i // tiles_per_seq, 0, i % tiles_per_seq, 0))
    out_shape = [jax.ShapeDtypeStruct((t, CONV_WIDTH), BF16)] + [hm_shape] * 7 + [
        jax.ShapeDtypeStruct((batch, RWKV_HEADS, seq // CHUNK, RWKV_HEAD_DIM), F32),
        jax.ShapeDtypeStruct((t, RWKV_WIDTH), F32),
        jax.ShapeDtypeStruct((t, RWKV_WIDTH), F32),
        jax.ShapeDtypeStruct((t, MLA_HEADS * QK_PAD), BF16),
        jax.ShapeDtypeStruct((t, MLA_HEADS * QK_PAD), BF16),
        jax.ShapeDtypeStruct((t // TM, MLA_WIDTH, TM), BF16),
        jax.ShapeDtypeStruct((t, MLA_WIDTH), F32),
    ]
    vt_spec = pl.BlockSpec((1, MLA_WIDTH, TM), lambda i: (i, 0, 0))
    out_specs = [row(CONV_WIDTH)] + [hm] * 7 + [wc_spec, row(RWKV_WIDTH), row(RWKV_WIDTH),
                                                  row(MLA_HEADS * QK_PAD), row(MLA_HEADS * QK_PAD),
                                                  vt_spec, row(MLA_WIDTH)]
    return pl.pallas_call(
        functools.partial(_proj_kernel, tiles_per_seq=tiles_per_seq),
        grid=(t // TM,),
        in_specs=[row(D_MODEL), halo] + [s for _, s in consts] + [row(LANE), row(LANE)],
        out_specs=out_specs,
        out_shape=out_shape,
        scratch_shapes=[pltpu.VMEM((HALO + TM, D_IN_P), F32)],
        compiler_params=pltpu.CompilerParams(dimension_semantics=("arbitrary",),
                                             vmem_limit_bytes=VMEM_LIMIT),
        name="proj",
    )(h, h, *[a for a, _ in consts], ck, sk)


def _rwkv_chunk_local(r_t, a_t, b_t, k_t, bh, kh, v, wc, tri_incl, tri_strict, eye):
    n = range(len(r_t))
    lhs = [jnp.concatenate([a_t[i], r_t[i]], axis=0) for i in n]
    ab = [_dot_nt(lhs[i], b_t[i]) for i in n]
    ak = [_dot_nt(lhs[i], k_t[i]) for i in n]
    aab = [jnp.where(tri_strict, ab[i][:CHUNK], 0.0).astype(BF16) for i in n]
    aak = [jnp.where(tri_strict, ak[i][:CHUNK], 0.0).astype(BF16) for i in n]
    arb = [jnp.where(tri_incl, ab[i][CHUNK:], 0.0).astype(BF16) for i in n]
    ark = [jnp.where(tri_incl, ak[i][CHUNK:], 0.0).astype(BF16) for i in n]

    x = [jnp.concatenate([a_t[i].astype(F32), _dot(aak[i], v[i])], axis=1) for i in n]
    p = aab
    for it in range(6):
        x = [x[i] + _dot(p[i], x[i]) for i in n]
        if it < 5:
            p = [_dot(p[i], p[i]).astype(BF16) for i in n]
    at = [x[i][:, :RWKV_HEAD_DIM].astype(BF16) for i in n]
    u = [x[i][:, RWKV_HEAD_DIM:].astype(BF16) for i in n]

    qp = [r_t[i].astype(F32) + _dot(arb[i], at[i]) for i in n]
    yloc = [_dot(arb[i], u[i]) + _dot(ark[i], v[i]) for i in n]
    mc = [jnp.where(eye, wc[i], 0.0) + _dot_tn(bh[i], at[i]) for i in n]
    nc = [_dot_tn(jnp.concatenate([bh[i], kh[i]], axis=0), jnp.concatenate([u[i], v[i]], axis=0)) for i in n]
    return qp, yloc, mc, nc


def _rwkv_kernel(rt_ref, at_ref, bt_ref, kt_ref, bh_ref, kh_ref, v_ref, wc_ref, gate_ref, bg_ref,
                 gnw_ref, gnb_ref, y_ref, state_ref):
    step = pl.program_id(1)

    @pl.when(step == 0)
    def _():
        state_ref[...] = jnp.zeros_like(state_ref)

    ri = lax.broadcasted_iota(jnp.int32, (CHUNK, CHUNK), 0)
    ci = lax.broadcasted_iota(jnp.int32, (CHUNK, CHUNK), 1)
    tri_incl = ri >= ci
    tri_strict = ri > ci
    eye = ri == ci
    items = [(c, hd) for c in range(RWKV_CHUNKS_PER_STEP) for hd in range(RWKV_HEADS)]
    load = lambda ref: [ref[0, hd, pl.ds(c * CHUNK, CHUNK), :] for c, hd in items]
    wc = [wc_ref[0, hd, pl.ds(step * RWKV_CHUNKS_PER_STEP + c, 1), :] for c, hd in items]
    qp, yloc, mc, nc = _rwkv_chunk_local(load(rt_ref), load(at_ref), load(bt_ref), load(kt_ref),
                                         load(bh_ref), load(kh_ref), load(v_ref), wc,
                                         tri_incl, tri_strict, eye)
    states = [state_ref[hd] for hd in range(RWKV_HEADS)]
    heads = range(RWKV_HEADS)
    for c in range(RWKV_CHUNKS_PER_STEP):
        rows = pl.ds(c * CHUNK, CHUNK)
        o = c * RWKV_HEADS
        y = [_dot(qp[o + hd], states[hd]) + yloc[o + hd] for hd in heads]
        states = [_dot(mc[o + hd], states[hd]) + nc[o + hd] for hd in heads]
        outs = []
        for hd in heads:
            mean = jnp.mean(y[hd], axis=-1, keepdims=True)
            yc = y[hd] - mean
            var = jnp.mean(yc * yc, axis=-1, keepdims=True)
            outs.append(yc * lax.rsqrt(var + GN_EPS) * gnw_ref[hd:hd + 1, :] + gnb_ref[hd:hd + 1, :])
        y_ref[0, rows, :] = (jnp.concatenate(outs, axis=1) * gate_ref[0, rows, :]
                             + bg_ref[0, rows, :]).astype(BF16)
    for hd in heads:
        state_ref[hd] = states[hd]


def _rwkv_call(hm, wc, gate, bg, lw, layer, batch, seq):
    tc = RWKV_CHUNKS_PER_STEP * CHUNK
    hm_spec = pl.BlockSpec((1, RWKV_HEADS, tc, RWKV_HEAD_DIM), lambda b, c: (b, 0, c, 0))
    wc_spec = pl.BlockSpec((1, RWKV_HEADS, seq // CHUNK, RWKV_HEAD_DIM), lambda b, c: (b, 0, 0, 0))
    nat = pl.BlockSpec((1, tc, RWKV_WIDTH), lambda b, c: (b, c, 0))
    return pl.pallas_call(
        _rwkv_kernel,
        grid=(batch, seq // tc),
        in_specs=[hm_spec] * 7 + [wc_spec, nat, nat,
                                  _layer_spec(lw["gnw"], layer), _layer_spec(lw["gnb"], layer)],
        out_specs=nat,
        out_shape=jax.ShapeDtypeStruct((batch, seq, RWKV_WIDTH), BF16),
        scratch_shapes=[pltpu.VMEM((RWKV_HEADS, RWKV_HEAD_DIM, RWKV_HEAD_DIM), F32)],
        compiler_params=pltpu.CompilerParams(dimension_semantics=("arbitrary", "arbitrary"),
                                             vmem_limit_bytes=VMEM_LIMIT),
        name="rwkv",
    )(*hm, wc, gate.reshape(batch, seq, RWKV_WIDTH), bg.reshape(batch, seq, RWKV_WIDTH),
      lw["gnw"], lw["gnb"])


def _attn_kernel(q_ref, k_ref, vt_ref, g_ref, o_ref, m_scr, acc_scr, s0_scr, s1_scr, c0_scr, c1_scr):
    tq = s0_scr.shape[1]
    n_q = q_ref.shape[1] // tq
    bufs = ((s0_scr, c0_scr), (s1_scr, c1_scr))
    ones = jnp.ones((BF16_ROWS, tq), BF16)
    tiles = [(i, j) for i in range(n_q) for j in range(i + 1)]

    def scores(i, j, slot):
        s_ref, c_ref = bufs[slot]
        k = k_ref[0, j * tq:(j + 1) * tq, :]
        q = q_ref[0, i * tq:(i + 1) * tq, :]
        s = lax.dot_general(k, q, (((1,), (1,)), ((), ())), preferred_element_type=F32)
        if i == j:
            kc = lax.broadcasted_iota(jnp.int32, s.shape, 0) // CHUNK
            qc = lax.broadcasted_iota(jnp.int32, s.shape, 1) // CHUNK
            s = jnp.where(kc <= qc, s, -1e30)
        s_ref[...] = s
        c_ref[...] = jnp.max(s, axis=0, keepdims=True)

    def accumulate(i, j, slot):
        s_ref, c_ref = bufs[slot]
        m_new = c_ref[...] if j == 0 else jnp.maximum(m_scr[...], c_ref[...])
        p = jnp.exp2((s_ref[...] - m_new).astype(BF16))
        vt1 = jnp.concatenate([vt_ref[0, j], ones], axis=0)
        acc = jnp.dot(vt1, p, preferred_element_type=F32)
        if j > 0:
            acc = jnp.exp2(m_scr[...] - m_new) * acc_scr[...] + acc
        if j == i:
            rows = slice(i * tq, (i + 1) * tq)
            o = acc[:V_HEAD_DIM] / acc[V_HEAD_DIM:V_HEAD_DIM + 1]
            o_ref[0, rows, :] = (o.T * g_ref[0, rows, :]).astype(BF16)
        else:
            m_scr[...] = m_new
            acc_scr[...] = acc

    scores(*tiles[0], 0)
    for t, (i, j) in enumerate(tiles):
        if t + 1 < len(tiles):
            scores(*tiles[t + 1], (t + 1) % 2)
        accumulate(i, j, t % 2)


def _attn_call(q, k, vt, gate, batch, seq):
    q = q.reshape(batch, seq, MLA_HEADS * QK_PAD)
    k = k.reshape(batch, seq, MLA_HEADS * QK_PAD)
    vt = vt.reshape(batch, seq // TQ, MLA_WIDTH, TQ)
    gate = gate.reshape(batch, seq, MLA_WIDTH)
    return pl.pallas_call(
        _attn_kernel,
        grid=(batch, MLA_HEADS),
        in_specs=[pl.BlockSpec((1, seq, QK_PAD), lambda b, h: (b, 0, h)),
                  pl.BlockSpec((1, seq, QK_PAD), lambda b, h: (b, 0, h)),
                  pl.BlockSpec((1, seq // TQ, V_HEAD_DIM, TQ), lambda b, h: (b, 0, h, 0)),
                  pl.BlockSpec((1, seq, V_HEAD_DIM), lambda b, h: (b, 0, h))],
        out_specs=pl.BlockSpec((1, seq, V_HEAD_DIM), lambda b, h: (b, 0, h)),
        out_shape=jax.ShapeDtypeStruct((batch, seq, MLA_WIDTH), BF16),
        scratch_shapes=[pltpu.VMEM((1, TQ), F32),
                        pltpu.VMEM((V_HEAD_DIM + BF16_ROWS, TQ), F32),
                        pltpu.VMEM((TQ, TQ), F32), pltpu.VMEM((TQ, TQ), F32),
                        pltpu.VMEM((1, TQ), F32), pltpu.VMEM((1, TQ), F32)],
        compiler_params=pltpu.CompilerParams(
            dimension_semantics=("arbitrary", "arbitrary"), vmem_limit_bytes=VMEM_LIMIT),
        name="attn",
    )(q, k, vt, gate)


def _out_kernel(h_ref, yc_ref, yr_ref, ym_ref, wo_ref, p_ref, wple_ref, png_ref, wpg_ref, fg_ref,
                o_ref, *, final):
    ycat = jnp.concatenate([yc_ref[...], yr_ref[...], ym_ref[...]], axis=-1)
    h = h_ref[...] + jnp.dot(ycat, wo_ref[...], preferred_element_type=F32)
    gate = jax.nn.sigmoid(jnp.dot(_rms(h, png_ref[...]).astype(BF16), wpg_ref[...],
                                  preferred_element_type=F32))
    h = h + jnp.dot(p_ref[...].astype(BF16), wple_ref[...], preferred_element_type=F32) * gate
    if final:
        h = _rms(h, fg_ref[...])
    o_ref[...] = h


def _out_call(h, yc, yr, ym, p, layer, lw, final_g, final):
    t = h.shape[0]
    row = lambda w: pl.BlockSpec((TM, w), lambda i: (i, 0))
    return pl.pallas_call(
        functools.partial(_out_kernel, final=final),
        grid=(t // TM,),
        in_specs=[row(D_MODEL), row(CONV_WIDTH), row(RWKV_WIDTH), row(MLA_WIDTH),
                  _layer_spec(lw["wo"], layer),
                  pl.BlockSpec((None, TM, PLE_DIM), lambda i: (layer, i, 0)),
                  _layer_spec(lw["wple"], layer), _layer_spec(lw["png"], layer),
                  _layer_spec(lw["wpg"], layer), _const_spec(final_g)],
        out_specs=row(D_MODEL),
        out_shape=jax.ShapeDtypeStruct((t, D_MODEL), F32),
        compiler_params=pltpu.CompilerParams(dimension_semantics=("arbitrary",),
                                             vmem_limit_bytes=VMEM_LIMIT),
        name="out",
    )(h, yc, yr, ym, lw["wo"], p, lw["wple"], lw["png"], lw["wpg"], final_g)


def _prep(norm_mix_g, w_in, conv_w, rwkv_mu, rwkv_w0, rwkv_w2, rwkv_a0, rwkv_a2, rwkv_kk, rwkv_ka, rwkv_rk,
          rwkv_gn_w, rwkv_gn_b, mla_q_norm_g, mla_w_qb, mla_kv_norm_g, mla_w_kvb, w_out, ple_w, ple_norm_g,
          ple_gate_w):
    depth = w_in.shape[0]
    o_kr = C_MG
    o_mg = o_kr + QK_ROPE_DIM
    zeros = jnp.zeros((depth, D_MODEL, QK_ROPE_DIM), F32)
    win_a = w_in[:, :, :o_kr].astype(BF16)
    win_b = jnp.concatenate([w_in[:, :, o_mg:], w_in[:, :, o_kr:o_mg], zeros], axis=2).astype(BF16)

    zl = jnp.zeros((depth, LORA, RWKV_WIDTH), F32)
    wl = jnp.concatenate([jnp.concatenate([rwkv_w2, zl], axis=2),
                          jnp.concatenate([zl, rwkv_a2], axis=2)], axis=1).astype(BF16)

    wqb = mla_w_qb.reshape(depth, Q_LORA_RANK, MLA_HEADS, QK_NOPE_DIM + QK_ROPE_DIM)
    zq = jnp.zeros((depth, Q_LORA_RANK, MLA_HEADS, QK_ROPE_DIM), F32)
    wq = jnp.concatenate([wqb, zq], axis=-1).reshape(depth, Q_LORA_RANK, MLA_HEADS * QK_PAD).astype(BF16)
    wkvb = mla_w_kvb.reshape(depth, KV_LORA_RANK, MLA_HEADS, QK_NOPE_DIM + V_HEAD_DIM)
    wkv = jnp.concatenate([wkvb[..., :QK_NOPE_DIM].reshape(depth, KV_LORA_RANK, -1),
                           wkvb[..., QK_NOPE_DIM:].reshape(depth, KV_LORA_RANK, -1)], axis=2).astype(BF16)
    row = lambda x: x.reshape(depth, 1, -1)
    hd = lambda x: x.reshape(depth, RWKV_HEADS, RWKV_HEAD_DIM)
    return dict(
        ng=row(norm_mix_g), win_a=win_a, win_b=win_b, cw=conv_w,
        mu_rkv=row(rwkv_mu[:, :3 * RWKV_WIDTH]), mu_wa=row(rwkv_mu[:, 3 * RWKV_WIDTH:]),
        w0=row(rwkv_w0), a0=row(rwkv_a0), wl=wl, kk=row(rwkv_kk), ka=row(rwkv_ka),
        qg=row(mla_q_norm_g), wq=wq, kvg=row(mla_kv_norm_g), wkv=wkv,
        rk=row(rwkv_rk), gnw=hd(rwkv_gn_w), gnb=hd(rwkv_gn_b),
        wo=w_out.astype(BF16), wple=ple_w.astype(BF16), png=row(ple_norm_g), wpg=ple_gate_w.astype(BF16))


def kernel(x, p, positions, norm_mix_g, w_in, conv_w, rwkv_mu, rwkv_w0, rwkv_w2, rwkv_a0, rwkv_a2,
           rwkv_kk, rwkv_ka, rwkv_rk, rwkv_gn_w, rwkv_gn_b, mla_q_norm_g, mla_w_qb, mla_kv_norm_g,
           mla_w_kvb, w_out, ple_w, ple_norm_g, ple_gate_w, final_norm_g):
    batch, seq, _ = x.shape
    depth = w_in.shape[0]
    t = batch * seq

    half = QK_ROPE_DIM // 2
    inv_freq = 1.0 / (ROPE_THETA ** (jnp.arange(0, QK_ROPE_DIM, 2, dtype=F32) / QK_ROPE_DIM))
    inv_tile = jnp.concatenate([inv_freq, inv_freq, jnp.zeros((LANE - QK_ROPE_DIM,), F32)])
    lane = np.arange(LANE)
    ang = positions.astype(F32).reshape(t, 1) * inv_tile
    ck = jnp.where(lane < QK_ROPE_DIM, jnp.cos(ang), 0.0)
    sk = jnp.sin(ang) * jnp.asarray(np.where(lane < half, -1.0, 1.0), F32)

    h = x.reshape(t, D_MODEL)
    p = p.reshape(depth, t, PLE_DIM)
    final_g = final_norm_g.reshape(1, D_MODEL)
    lw = _prep(norm_mix_g, w_in, conv_w, rwkv_mu, rwkv_w0, rwkv_w2, rwkv_a0, rwkv_a2, rwkv_kk, rwkv_ka,
               rwkv_rk, rwkv_gn_w, rwkv_gn_b, mla_q_norm_g, mla_w_qb, mla_kv_norm_g, mla_w_kvb, w_out,
               ple_w, ple_norm_g, ple_gate_w)
    for i in range(depth):
        (yconv, *hm, wc, gate_r, bg, q, kq, vq, gate_m) = _proj_call(h, lw, i, ck, sk, batch, seq)
        y_rwkv = _rwkv_call(hm, wc, gate_r, bg, lw, i, batch, seq).reshape(t, RWKV_WIDTH)
        y_mla = _attn_call(q, kq, vq, gate_m, batch, seq).reshape(t, MLA_WIDTH)
        h = _out_call(h, yconv, y_rwkv, y_mla, p, i, lw, final_g, final=(i == depth - 1))
    return h.reshape(batch, seq, D_MODEL)
```

```python
import functools
import math

import jax
import jax.numpy as jnp
import numpy as np
from jax import lax
from jax.experimental import pallas as pl
from jax.experimental.pallas import tpu as pltpu

F32 = jnp.float32
BF16 = jnp.bfloat16

D_MODEL = 1024
CHUNK = 64
PLE_DIM = 256
NORM_EPS = 1e-6
CONV_WIDTH = 256
RWKV_HEADS = 4
RWKV_HEAD_DIM = 64
RWKV_WIDTH = RWKV_HEADS * RWKV_HEAD_DIM
LORA = 64
DECAY_SCALE = math.exp(-0.5)
GN_EPS = 64e-5
MLA_HEADS = 4
QK_NOPE_DIM = 128
QK_ROPE_DIM = 64
V_HEAD_DIM = 128
Q_LORA_RANK = 384
KV_LORA_RANK = 256
MLA_WIDTH = MLA_HEADS * V_HEAD_DIM
ROPE_THETA = 10000.0
D_MIX = CONV_WIDTH + RWKV_WIDTH + MLA_WIDTH

C_CONV = 0
C_RKV = 1024
C_WA = 1792
C_RG = 1920
C_QA = 2176
C_KVA = 2560
C_MG = 2816
C_KR = 3328
D_IN_P = 3456
QK_PAD = 256

LANE = 128
BF16_ROWS = 16
HALO = BF16_ROWS
TM = 512
TQ = 512
RWKV_CHUNKS_PER_STEP = 4
VMEM_LIMIT = 56 * 1024 * 1024


def _dot(a, b):
    return jnp.dot(a.astype(BF16), b.astype(BF16), preferred_element_type=F32)


def _dot_nt(a, b):
    return lax.dot_general(a.astype(BF16), b.astype(BF16), (((1,), (1,)), ((), ())),
                           preferred_element_type=F32)


def _dot_tn(a, b):
    return lax.dot_general(a.astype(BF16), b.astype(BF16), (((0,), (0,)), ((), ())),
                           preferred_element_type=F32)


def _split2(x):
    hi = x.astype(BF16)
    return hi, (x - hi.astype(F32)).astype(BF16)


def _dot_mask(x, m01, left=False):
    hi, lo = _split2(x)
    if left:
        return jnp.dot(m01, hi, preferred_element_type=F32) + jnp.dot(m01, lo, preferred_element_type=F32)
    return jnp.dot(hi, m01, preferred_element_type=F32) + jnp.dot(lo, m01, preferred_element_type=F32)


def _chunk_cumsum(x):
    pos = lax.broadcasted_iota(jnp.int32, x.shape, 0) % CHUNK
    d = 1
    while d < CHUNK:
        x = x + jnp.where(pos >= d, pltpu.roll(x, d, axis=0), 0.0)
        d *= 2
    return x


def _swap_rope_halves(x):
    half = QK_ROPE_DIM // 2
    return pltpu.roll(x, half, axis=1) + pltpu.roll(x, LANE - half, axis=1)


def _rms(x, g):
    return x * lax.rsqrt(jnp.mean(x * x, axis=-1, keepdims=True) + NORM_EPS) * g


def _silu(x):
    return x * jax.nn.sigmoid(x)


def _proj_kernel(h_ref, halo_ref, ng_ref, wa_ref, wb_ref, cwt_ref, mu_rkv_ref, mu_wa_ref, w0_ref, a0_ref,
                 wl_ref, kk_ref, ka_ref, rk_ref, seg_ref, csum_ref, qg_ref, wq_ref, kvg_ref, wkv_ref,
                 ck_ref, sk_ref,
                 yconv_ref, rt_ref, at_ref, bt_ref, kt_ref, bh_ref, kh_ref, v_ref, wc_ref, gr_ref, bg_ref,
                 q_ref, kq_ref, vq_ref, gm_ref, z_scr, *, tiles_per_seq):
    hm_refs = (rt_ref, at_ref, bt_ref, kt_ref, bh_ref, kh_ref, v_ref)
    tm = h_ref.shape[0]
    hx = jnp.concatenate([halo_ref[...], h_ref[...]], axis=0)
    u = _rms(hx, ng_ref[...]).astype(BF16)
    z_scr[:, :C_MG] = jnp.dot(u, wa_ref[...], preferred_element_type=F32)
    z_scr[:, C_MG:] = jnp.dot(u, wb_ref[...], preferred_element_type=F32)

    first = pl.program_id(0) % tiles_per_seq == 0
    row = lax.broadcasted_iota(jnp.int32, (tm, 1), 0)

    def cols(c, w, back=0):
        x = z_scr[pl.ds(HALO - back, tm), c:c + w]
        return jnp.where(row >= jnp.where(first, back, 0), x, 0.0) if back else x

    cwt = cwt_ref[...]
    conv = sum(cols(C_CONV + 256, 256, back) * cols(C_CONV + 512, 256, back) * cwt[2 - back:3 - back, :]
               for back in range(3))
    yconv_ref[...] = (cols(C_CONV, 256) * conv * _silu(cols(C_CONV + 768, 256))).astype(BF16)

    cur = cols(C_RKV, 768)
    rkv = cur + (cols(C_RKV, 768, 1) - cur) * mu_rkv_ref[...]
    cur = cols(C_WA, 128)
    wa = cur + (cols(C_WA, 128, 1) - cur) * mu_wa_ref[...]
    r = rkv[:, 0:256]
    k = rkv[:, 256:512]
    v = rkv[:, 512:768]
    lane = lax.broadcasted_iota(jnp.int32, wa.shape, 1)
    lora_in = jnp.where(lane < LORA, jnp.tanh(wa), wa)
    lora = jnp.dot(lora_in.astype(BF16), wl_ref[...], preferred_element_type=F32)
    wlog = -DECAY_SCALE * jax.nn.sigmoid(w0_ref[...] + lora[:, :RWKV_WIDTH])
    a = jax.nn.sigmoid(a0_ref[...] + lora[:, RWKV_WIDTH:])
    kk = k * kk_ref[...]
    kk = kk * lax.rsqrt(_dot_mask(kk * kk, seg_ref[...]) + 1e-12)
    kmod = k * (1.0 + (a - 1.0) * ka_ref[...])
    zb = kk * a
    cw = _chunk_cumsum(wlog)
    total = _dot_mask(wlog, csum_ref[...], left=True)
    nchunk = tm // CHUNK
    rest = jnp.broadcast_to(total[:, None, :], (nchunk, CHUNK, RWKV_WIDTH)).reshape(tm, RWKV_WIDTH) - cw
    inv = jnp.exp(-cw)
    to_end = jnp.exp(rest)
    wc = jnp.exp(total)
    ops = (r * jnp.exp(cw), -kk * jnp.exp(cw - wlog), zb * inv, kmod * inv, zb * to_end, kmod * to_end, v)
    zpad = jnp.zeros((tm, LANE - RWKV_HEAD_DIM), BF16)
    for hd in range(RWKV_HEADS):
        sl = slice(hd * RWKV_HEAD_DIM, (hd + 1) * RWKV_HEAD_DIM)
        for ref, val in zip(hm_refs, ops):
            piece = val[:, sl].astype(BF16)
            if ref is at_ref:
                piece = jnp.concatenate([piece, zpad], axis=1)
            elif ref is v_ref:
                piece = jnp.concatenate([zpad, piece], axis=1)
            ref[0, hd] = piece
        wc_ref[0, hd] = wc[:, sl]
    gate_r = _silu(cols(C_RG, 256))
    bonus = _dot_mask(r * kmod * rk_ref[...], seg_ref[...]) * v
    gr_ref[...] = gate_r
    bg_ref[...] = bonus * gate_r

    ck = ck_ref[...]
    sk = sk_ref[...]
    qn = _rms(cols(C_QA, Q_LORA_RANK), qg_ref[...]).astype(BF16)
    qm = jnp.dot(qn, wq_ref[...], preferred_element_type=F32)
    scale = math.log2(math.e) / math.sqrt(QK_NOPE_DIM + QK_ROPE_DIM)
    kvn = _rms(cols(C_KVA, KV_LORA_RANK), kvg_ref[...]).astype(BF16)
    kv = jnp.dot(kvn, wkv_ref[...], preferred_element_type=F32)
    kx = cols(C_KR, LANE)
    kr = kx * ck + _swap_rope_halves(kx) * sk
    for hd in range(MLA_HEADS):
        o = hd * QK_PAD
        qx = qm[:, o + LANE:o + QK_PAD]
        qr = qx * ck + _swap_rope_halves(qx) * sk
        q_ref[:, o:o + LANE] = (qm[:, o:o + LANE] * scale).astype(BF16)
        q_ref[:, o + LANE:o + QK_PAD] = (qr * scale).astype(BF16)
        kq_ref[:, o:o + LANE] = kv[:, hd * LANE:(hd + 1) * LANE].astype(BF16)
        kq_ref[:, o + LANE:o + QK_PAD] = kr.astype(BF16)
    vq_ref[0] = kv[:, MLA_HEADS * QK_NOPE_DIM:].T.astype(BF16)
    gm_ref[...] = _silu(cols(C_MG, MLA_WIDTH))


def _const_spec(arr):
    return pl.BlockSpec(arr.shape, lambda *_: (0,) * arr.ndim)


def _layer_spec(arr, layer):
    zeros = (0,) * (arr.ndim - 1)
    return pl.BlockSpec((None,) + arr.shape[1:], lambda *_: (layer,) + zeros)


def _proj_call(h, lw, layer, ck, sk, batch, seq):
    t = h.shape[0]
    tiles_per_seq = seq // TM
    row = lambda w: pl.BlockSpec((TM, w), lambda i: (i, 0))
    halo = pl.BlockSpec((HALO, D_MODEL), lambda i: (jnp.maximum(i * (TM // HALO) - 1, 0), 0))
    chunk_id = np.arange(TM) // CHUNK
    csum = jnp.asarray(np.arange(TM // CHUNK)[:, None] == chunk_id[None, :], BF16)
    head = np.arange(RWKV_WIDTH) // RWKV_HEAD_DIM
    seg = jnp.asarray(head[:, None] == head[None, :], BF16)
    per_layer = lambda *names: [(lw[n], _layer_spec(lw[n], layer)) for n in names]
    shared = lambda *arrs: [(a, _const_spec(a)) for a in arrs]
    consts = (per_layer("ng", "win_a", "win_b", "cw", "mu_rkv", "mu_wa", "w0", "a0", "wl", "kk", "ka", "rk")
              + shared(seg, csum) + per_layer("qg", "wq", "kvg", "wkv"))
    hm_widths = [RWKV_HEAD_DIM, LANE] + [RWKV_HEAD_DIM] * 4 + [LANE]
    hm = [pl.BlockSpec((1, RWKV_HEADS, TM, w), lambda i: (i // tiles_per_seq, 0, i % tiles_per_seq, 0))
          for w in hm_widths]
    hm_shapes = [jax.ShapeDtypeStruct((batch, RWKV_HEADS, seq, w), BF16) for w in hm_widths]
    wc_spec = pl.BlockSpec((1, RWKV_HEADS, TM // CHUNK, RWKV_HEAD_DIM),
                           lambda i: (i // tiles_per_seq, 0, i % tiles_per_seq, 0))
    out_shape = [jax.ShapeDtypeStruct((t, CONV_WIDTH), BF16)] + hm_shapes + [
        jax.ShapeDtypeStruct((batch, RWKV_HEADS, seq // CHUNK, RWKV_HEAD_DIM), F32),
        jax.ShapeDtypeStruct((t, RWKV_WIDTH), F32),
        jax.ShapeDtypeStruct((t, RWKV_WIDTH), F32),
        jax.ShapeDtypeStruct((t, MLA_HEADS * QK_PAD), BF16),
        jax.ShapeDtypeStruct((t, MLA_HEADS * QK_PAD), BF16),
        jax.ShapeDtypeStruct((t // TM, MLA_WIDTH, TM), BF16),
        jax.ShapeDtypeStruct((t, MLA_WIDTH), F32),
    ]
    vt_spec = pl.BlockSpec((1, MLA_WIDTH, TM), lambda i: (i, 0, 0))
    out_specs = [row(CONV_WIDTH)] + hm + [wc_spec, row(RWKV_WIDTH), row(RWKV_WIDTH),
                                                  row(MLA_HEADS * QK_PAD), row(MLA_HEADS * QK_PAD),
                                                  vt_spec, row(MLA_WIDTH)]
    return pl.pallas_call(
        functools.partial(_proj_kernel, tiles_per_seq=tiles_per_seq),
        grid=(t // TM,),
        in_specs=[row(D_MODEL), halo] + [s for _, s in consts] + [row(LANE), row(LANE)],
        out_specs=out_specs,
        out_shape=out_shape,
        scratch_shapes=[pltpu.VMEM((HALO + TM, D_IN_P), F32)],
        compiler_params=pltpu.CompilerParams(dimension_semantics=("arbitrary",),
                                             vmem_limit_bytes=VMEM_LIMIT),
        name="proj",
    )(h, h, *[a for a, _ in consts], ck, sk)


def _rwkv_chunk_local(r_t, a_t, b_t, k_t, bh, kh, v, wc, tri_incl, tri_strict, eye):
    n = range(len(r_t))
    half = RWKV_HEAD_DIM
    lhs = [jnp.concatenate([a_t[i][:, :half], r_t[i]], axis=0) for i in n]
    ab = [_dot_nt(lhs[i], b_t[i]) for i in n]
    ak = [_dot_nt(lhs[i], k_t[i]) for i in n]
    aab = [jnp.where(tri_strict, ab[i][:CHUNK], 0.0).astype(BF16) for i in n]
    aak = [jnp.where(tri_strict, ak[i][:CHUNK], 0.0).astype(BF16) for i in n]
    arb = [jnp.where(tri_incl, ab[i][CHUNK:], 0.0).astype(BF16) for i in n]
    ark = [jnp.where(tri_incl, ak[i][CHUNK:], 0.0).astype(BF16) for i in n]

    x = [a_t[i].astype(F32) + _dot(aak[i], v[i]) for i in n]
    p = aab
    for it in range(6):
        x = [x[i] + _dot(p[i], x[i]) for i in n]
        if it < 5:
            p = [_dot(p[i], p[i]).astype(BF16) for i in n]
    x = [x[i].astype(BF16) for i in n]

    qpu = [_dot(arb[i], x[i]) for i in n]
    qp = [r_t[i].astype(F32) + qpu[i][:, :half] for i in n]
    yloc = [qpu[i] + _dot(ark[i], v[i]) for i in n]
    mcn = [_dot_tn(bh[i], x[i]) for i in n]
    mc = [jnp.where(eye, wc[i], 0.0) + mcn[i][:, :half] for i in n]
    nc = [mcn[i] + _dot_tn(kh[i], v[i]) for i in n]
    return qp, yloc, mc, nc


def _rwkv_kernel(rt_ref, at_ref, bt_ref, kt_ref, bh_ref, kh_ref, v_ref, wc_ref, gate_ref, bg_ref,
                 gnw_ref, gnb_ref, gnm_ref, y_ref, state_ref):
    step = pl.program_id(0)
    batch = rt_ref.shape[0]

    @pl.when(step == 0)
    def _():
        state_ref[...] = jnp.zeros_like(state_ref)

    ri = lax.broadcasted_iota(jnp.int32, (CHUNK, CHUNK), 0)
    ci = lax.broadcasted_iota(jnp.int32, (CHUNK, CHUNK), 1)
    tri_incl = ri >= ci
    tri_strict = ri > ci
    eye = ri == ci
    value_lanes = lax.broadcasted_iota(jnp.int32, (CHUNK, LANE), 1) >= RWKV_HEAD_DIM
    seqs = [(b, hd) for b in range(batch) for hd in range(RWKV_HEADS)]
    items = [(c, b, hd) for c in range(RWKV_CHUNKS_PER_STEP) for b, hd in seqs]
    n = range(len(items))
    load = lambda ref: [ref[b, hd, pl.ds(c * CHUNK, CHUNK), :] for c, b, hd in items]
    wc = [wc_ref[b, hd, pl.ds(step * RWKV_CHUNKS_PER_STEP + c, 1), :] for c, b, hd in items]
    qp, yloc, mc, nc = _rwkv_chunk_local(load(rt_ref), load(at_ref), load(bt_ref), load(kt_ref),
                                         load(bh_ref), load(kh_ref), load(v_ref), wc,
                                         tri_incl, tri_strict, eye)
    ns = range(len(seqs))
    states = [state_ref[s] for s in ns]
    y = []
    for c in range(RWKV_CHUNKS_PER_STEP):
        o = c * len(seqs)
        y += [_dot(qp[o + s], states[s]) + yloc[o + s] for s in ns]
        states = [jnp.where(value_lanes, _dot(mc[o + s], states[s]) + nc[o + s], 0.0) for s in ns]
    for s in ns:
        state_ref[s] = states[s]
    gnm = gnm_ref[...]
    head = [hd for _, _, hd in items]
    mean = [_dot(y[i], gnm) for i in n]
    yc = [y[i] - mean[i] for i in n]
    var = [_dot(yc[i] * yc[i], gnm) for i in n]
    gn = [yc[i] * lax.rsqrt(var[i] + GN_EPS) * gnw_ref[head[i]:head[i] + 1, :]
          + gnb_ref[head[i]:head[i] + 1, :] for i in n]
    for c in range(RWKV_CHUNKS_PER_STEP):
        rows = pl.ds(c * CHUNK, CHUNK)
        for b in range(batch):
            o = (c * batch + b) * RWKV_HEADS
            pair = [pltpu.roll(gn[o + hd], RWKV_HEAD_DIM, axis=1) + gn[o + hd + 1] for hd in (0, 2)]
            y_ref[b, rows, :] = (jnp.concatenate(pair, axis=1) * gate_ref[b, rows, :]
                                 + bg_ref[b, rows, :]).astype(BF16)


def _rwkv_call(hm, wc, gate, bg, lw, layer, batch, seq):
    tc = RWKV_CHUNKS_PER_STEP * CHUNK
    hm_spec = lambda a: pl.BlockSpec((batch, RWKV_HEADS, tc, a.shape[-1]), lambda c: (0, 0, c, 0))
    nat = pl.BlockSpec((batch, tc, RWKV_WIDTH), lambda c: (0, c, 0))
    value = np.arange(LANE) >= RWKV_HEAD_DIM
    gnm = jnp.asarray((value[:, None] & value[None, :]) / RWKV_HEAD_DIM, BF16)
    return pl.pallas_call(
        _rwkv_kernel,
        grid=(seq // tc,),
        in_specs=[hm_spec(a) for a in hm] + [_const_spec(wc), nat, nat,
                                             _layer_spec(lw["gnw"], layer), _layer_spec(lw["gnb"], layer),
                                             _const_spec(gnm)],
        out_specs=nat,
        out_shape=jax.ShapeDtypeStruct((batch, seq, RWKV_WIDTH), BF16),
        scratch_shapes=[pltpu.VMEM((batch * RWKV_HEADS, RWKV_HEAD_DIM, LANE), F32)],
        compiler_params=pltpu.CompilerParams(dimension_semantics=("arbitrary",),
                                             vmem_limit_bytes=VMEM_LIMIT),
        name="rwkv",
    )(*hm, wc, gate.reshape(batch, seq, RWKV_WIDTH), bg.reshape(batch, seq, RWKV_WIDTH),
      lw["gnw"], lw["gnb"], gnm)


def _attn_kernel(q_ref, k_ref, vt_ref, g_ref, o_ref, m_scr, acc_scr, s0_scr, s1_scr, c0_scr, c1_scr):
    tq = s0_scr.shape[1]
    n_q = q_ref.shape[1] // tq
    bufs = ((s0_scr, c0_scr), (s1_scr, c1_scr))
    ones = jnp.ones((BF16_ROWS, tq), BF16)
    tiles = [(i, j) for i in range(n_q) for j in range(i + 1)]

    def scores(i, j, slot):
        s_ref, c_ref = bufs[slot]
        k = k_ref[0, j * tq:(j + 1) * tq, :]
        q = q_ref[0, i * tq:(i + 1) * tq, :]
        s = lax.dot_general(k, q, (((1,), (1,)), ((), ())), preferred_element_type=F32)
        if i == j:
            kc = lax.broadcasted_iota(jnp.int32, s.shape, 0) // CHUNK
            qc = lax.broadcasted_iota(jnp.int32, s.shape, 1) // CHUNK
            s = jnp.where(kc <= qc, s, -1e30)
        s_ref[...] = s
        c_ref[...] = jnp.max(s, axis=0, keepdims=True)

    def accumulate(i, j, slot):
        s_ref, c_ref = bufs[slot]
        m_new = c_ref[...] if j == 0 else jnp.maximum(m_scr[...], c_ref[...])
        p = jnp.exp2((s_ref[...] - m_new).astype(BF16))
        vt1 = jnp.concatenate([vt_ref[0, j], ones], axis=0)
        acc = jnp.dot(vt1, p, preferred_element_type=F32)
        if j > 0:
            acc = jnp.exp2(m_scr[...] - m_new) * acc_scr[...] + acc
        if j == i:
            rows = slice(i * tq, (i + 1) * tq)
            o = acc[:V_HEAD_DIM] / acc[V_HEAD_DIM:V_HEAD_DIM + 1]
            o_ref[0, rows, :] = (o.T * g_ref[0, rows, :]).astype(BF16)
        else:
            m_scr[...] = m_new
            acc_scr[...] = acc

    scores(*tiles[0], 0)
    for t, (i, j) in enumerate(tiles):
        if t + 1 < len(tiles):
            scores(*tiles[t + 1], (t + 1) % 2)
        accumulate(i, j, t % 2)


def _attn_call(q, k, vt, gate, batch, seq):
    q = q.reshape(batch, seq, MLA_HEADS * QK_PAD)
    k = k.reshape(batch, seq, MLA_HEADS * QK_PAD)
    vt = vt.reshape(batch, seq // TQ, MLA_WIDTH, TQ)
    gate = gate.reshape(batch, seq, MLA_WIDTH)
    return pl.pallas_call(
        _attn_kernel,
        grid=(batch, MLA_HEADS),
        in_specs=[pl.BlockSpec((1, seq, QK_PAD), lambda b, h: (b, 0, h)),
                  pl.BlockSpec((1, seq, QK_PAD), lambda b, h: (b, 0, h)),
                  pl.BlockSpec((1, seq // TQ, V_HEAD_DIM, TQ), lambda b, h: (b, 0, h, 0)),
                  pl.BlockSpec((1, seq, V_HEAD_DIM), lambda b, h: (b, 0, h))],
        out_specs=pl.BlockSpec((1, seq, V_HEAD_DIM), lambda b, h: (b, 0, h)),
        out_shape=jax.ShapeDtypeStruct((batch, seq, MLA_WIDTH), BF16),
        scratch_shapes=[pltpu.VMEM((1, TQ), F32),
                        pltpu.VMEM((V_HEAD_DIM + BF16_ROWS, TQ), F32),
                        pltpu.VMEM((TQ, TQ), F32), pltpu.VMEM((TQ, TQ), F32),
                        pltpu.VMEM((1, TQ), F32), pltpu.VMEM((1, TQ), F32)],
        compiler_params=pltpu.CompilerParams(
            dimension_semantics=("arbitrary", "arbitrary"), vmem_limit_bytes=VMEM_LIMIT),
        name="attn",
    )(q, k, vt, gate)


def _out_kernel(h_ref, yc_ref, yr_ref, ym_ref, wo_ref, p_ref, wple_ref, png_ref, wpg_ref, fg_ref,
                o_ref, *, final):
    ycat = jnp.concatenate([yc_ref[...], yr_ref[...], ym_ref[...]], axis=-1)
    h = h_ref[...] + jnp.dot(ycat, wo_ref[...], preferred_element_type=F32)
    gate = jax.nn.sigmoid(jnp.dot(_rms(h, png_ref[...]).astype(BF16), wpg_ref[...],
                                  preferred_element_type=F32))
    h = h + jnp.dot(p_ref[...].astype(BF16), wple_ref[...], preferred_element_type=F32) * gate
    if final:
        h = _rms(h, fg_ref[...])
    o_ref[...] = h


def _out_call(h, yc, yr, ym, p, layer, lw, final_g, final):
    t = h.shape[0]
    row = lambda w: pl.BlockSpec((TM, w), lambda i: (i, 0))
    return pl.pallas_call(
        functools.partial(_out_kernel, final=final),
        grid=(t // TM,),
        in_specs=[row(D_MODEL), row(CONV_WIDTH), row(RWKV_WIDTH), row(MLA_WIDTH),
                  _layer_spec(lw["wo"], layer),
                  pl.BlockSpec((None, TM, PLE_DIM), lambda i: (layer, i, 0)),
                  _layer_spec(lw["wple"], layer), _layer_spec(lw["png"], layer),
                  _layer_spec(lw["wpg"], layer), _const_spec(final_g)],
        out_specs=row(D_MODEL),
        out_shape=jax.ShapeDtypeStruct((t, D_MODEL), F32),
        compiler_params=pltpu.CompilerParams(dimension_semantics=("arbitrary",),
                                             vmem_limit_bytes=VMEM_LIMIT),
        name="out",
    )(h, yc, yr, ym, lw["wo"], p, lw["wple"], lw["png"], lw["wpg"], final_g)


def _prep(norm_mix_g, w_in, conv_w, rwkv_mu, rwkv_w0, rwkv_w2, rwkv_a0, rwkv_a2, rwkv_kk, rwkv_ka, rwkv_rk,
          rwkv_gn_w, rwkv_gn_b, mla_q_norm_g, mla_w_qb, mla_kv_norm_g, mla_w_kvb, w_out, ple_w, ple_norm_g,
          ple_gate_w):
    depth = w_in.shape[0]
    o_kr = C_MG
    o_mg = o_kr + QK_ROPE_DIM
    zeros = jnp.zeros((depth, D_MODEL, QK_ROPE_DIM), F32)
    win_a = w_in[:, :, :o_kr].astype(BF16)
    win_b = jnp.concatenate([w_in[:, :, o_mg:], w_in[:, :, o_kr:o_mg], zeros], axis=2).astype(BF16)

    zl = jnp.zeros((depth, LORA, RWKV_WIDTH), F32)
    wl = jnp.concatenate([jnp.concatenate([rwkv_w2, zl], axis=2),
                          jnp.concatenate([zl, rwkv_a2], axis=2)], axis=1).astype(BF16)

    wqb = mla_w_qb.reshape(depth, Q_LORA_RANK, MLA_HEADS, QK_NOPE_DIM + QK_ROPE_DIM)
    zq = jnp.zeros((depth, Q_LORA_RANK, MLA_HEADS, QK_ROPE_DIM), F32)
    wq = jnp.concatenate([wqb, zq], axis=-1).reshape(depth, Q_LORA_RANK, MLA_HEADS * QK_PAD).astype(BF16)
    wkvb = mla_w_kvb.reshape(depth, KV_LORA_RANK, MLA_HEADS, QK_NOPE_DIM + V_HEAD_DIM)
    wkv = jnp.concatenate([wkvb[..., :QK_NOPE_DIM].reshape(depth, KV_LORA_RANK, -1),
                           wkvb[..., QK_NOPE_DIM:].reshape(depth, KV_LORA_RANK, -1)], axis=2).astype(BF16)
    row = lambda x: x.reshape(depth, 1, -1)
    hd = lambda x: jnp.pad(x.reshape(depth, RWKV_HEADS, RWKV_HEAD_DIM),
                           ((0, 0), (0, 0), (LANE - RWKV_HEAD_DIM, 0)))
    return dict(
        ng=row(norm_mix_g), win_a=win_a, win_b=win_b, cw=conv_w,
        mu_rkv=row(rwkv_mu[:, :3 * RWKV_WIDTH]), mu_wa=row(rwkv_mu[:, 3 * RWKV_WIDTH:]),
        w0=row(rwkv_w0), a0=row(rwkv_a0), wl=wl, kk=row(rwkv_kk), ka=row(rwkv_ka),
        qg=row(mla_q_norm_g), wq=wq, kvg=row(mla_kv_norm_g), wkv=wkv,
        rk=row(rwkv_rk), gnw=hd(rwkv_gn_w), gnb=hd(rwkv_gn_b),
        wo=w_out.astype(BF16), wple=ple_w.astype(BF16), png=row(ple_norm_g), wpg=ple_gate_w.astype(BF16))


def kernel(x, p, positions, norm_mix_g, w_in, conv_w, rwkv_mu, rwkv_w0, rwkv_w2, rwkv_a0, rwkv_a2,
           rwkv_kk, rwkv_ka, rwkv_rk, rwkv_gn_w, rwkv_gn_b, mla_q_norm_g, mla_w_qb, mla_kv_norm_g,
           mla_w_kvb, w_out, ple_w, ple_norm_g, ple_gate_w, final_norm_g):
    batch, seq, _ = x.shape
    depth = w_in.shape[0]
    t = batch * seq

    half = QK_ROPE_DIM // 2
    inv_freq = 1.0 / (ROPE_THETA ** (jnp.arange(0, QK_ROPE_DIM, 2, dtype=F32) / QK_ROPE_DIM))
    inv_tile = jnp.concatenate([inv_freq, inv_freq, jnp.zeros((LANE - QK_ROPE_DIM,), F32)])
    lane = np.arange(LANE)
    ang = positions.astype(F32).reshape(t, 1) * inv_tile
    ck = jnp.where(lane < QK_ROPE_DIM, jnp.cos(ang), 0.0)
    sk = jnp.sin(ang) * jnp.asarray(np.where(lane < half, -1.0, 1.0), F32)

    h = x.reshape(t, D_MODEL)
    p = p.reshape(depth, t, PLE_DIM)
    final_g = final_norm_g.reshape(1, D_MODEL)
    lw = _prep(norm_mix_g, w_in, conv_w, rwkv_mu, rwkv_w0, rwkv_w2, rwkv_a0, rwkv_a2, rwkv_kk, rwkv_ka,
               rwkv_rk, rwkv_gn_w, rwkv_gn_b, mla_q_norm_g, mla_w_qb, mla_kv_norm_g, mla_w_kvb, w_out,
               ple_w, ple_norm_g, ple_gate_w)
    for i in range(depth):
        (yconv, *hm, wc, gate_r, bg, q, kq, vq, gate_m) = _proj_call(h, lw, i, ck, sk, batch, seq)
        y_rwkv = _rwkv_call(hm, wc, gate_r, bg, lw, i, batch, seq).reshape(t, RWKV_WIDTH)
        y_mla = _attn_call(q, kq, vq, gate_m, batch, seq).reshape(t, MLA_WIDTH)
        h = _out_call(h, yconv, y_rwkv, y_mla, p, i, lw, final_g, final=(i == depth - 1))
    return h.reshape(batch, seq, D_MODEL)
```

```python
import functools
import math

import jax
import jax.numpy as jnp
import numpy as np
from jax import lax
from jax.experimental import pallas as pl
from jax.experimental.pallas import tpu as pltpu

F32 = jnp.float32
BF16 = jnp.bfloat16

D_MODEL = 1024
CHUNK = 64
PLE_DIM = 256
NORM_EPS = 1e-6
CONV_WIDTH = 256
RWKV_HEADS = 4
RWKV_HEAD_DIM = 64
RWKV_WIDTH = RWKV_HEADS * RWKV_HEAD_DIM
LORA = 64
DECAY_SCALE = math.exp(-0.5)
GN_EPS = 64e-5
MLA_HEADS = 4
QK_NOPE_DIM = 128
QK_ROPE_DIM = 64
V_HEAD_DIM = 128
Q_LORA_RANK = 384
KV_LORA_RANK = 256
MLA_WIDTH = MLA_HEADS * V_HEAD_DIM
ROPE_THETA = 10000.0
D_MIX = CONV_WIDTH + RWKV_WIDTH + MLA_WIDTH

C_CONV = 0
C_RKV = 1024
C_WA = 1792
C_RG = 1920
C_QA = 2176
C_KVA = 2560
C_MG = 2816
C_KR = 3328
D_IN_P = 3456
QK_PAD = 256

LANE = 128
BF16_ROWS = 16
HALO = BF16_ROWS
TM = 512
TQ = 512
RWKV_CHUNKS_PER_STEP = 4
VMEM_LIMIT = 56 * 1024 * 1024


def _dot(a, b):
    return jnp.dot(a.astype(BF16), b.astype(BF16), preferred_element_type=F32)


def _dot_nt(a, b):
    return lax.dot_general(a.astype(BF16), b.astype(BF16), (((1,), (1,)), ((), ())),
                           preferred_element_type=F32)


def _dot_tn(a, b):
    return lax.dot_general(a.astype(BF16), b.astype(BF16), (((0,), (0,)), ((), ())),
                           preferred_element_type=F32)


def _split2(x):
    hi = x.astype(BF16)
    return hi, (x - hi.astype(F32)).astype(BF16)


def _dot_mask(x, m01, left=False):
    hi, lo = _split2(x)
    if left:
        return jnp.dot(m01, hi, preferred_element_type=F32) + jnp.dot(m01, lo, preferred_element_type=F32)
    return jnp.dot(hi, m01, preferred_element_type=F32) + jnp.dot(lo, m01, preferred_element_type=F32)


def _chunk_cumsum(x):
    pos = lax.broadcasted_iota(jnp.int32, x.shape, 0) % CHUNK
    d = 1
    while d < CHUNK:
        x = x + jnp.where(pos >= d, pltpu.roll(x, d, axis=0), 0.0)
        d *= 2
    return x


def _swap_rope_halves(x):
    half = QK_ROPE_DIM // 2
    return pltpu.roll(x, half, axis=1) + pltpu.roll(x, LANE - half, axis=1)


def _rms(x, g):
    return x * lax.rsqrt(jnp.mean(x * x, axis=-1, keepdims=True) + NORM_EPS) * g


def _silu(x):
    return x * jax.nn.sigmoid(x)


def _proj_kernel(h_ref, halo_ref, ng_ref, wa_ref, wb_ref, cwt_ref, mu_rkv_ref, mu_wa_ref, w0_ref, a0_ref,
                 wl_ref, kk_ref, ka_ref, rk_ref, seg_ref, csum_ref, qg_ref, wq_ref, kvg_ref, wkv_ref,
                 ck_ref, sk_ref,
                 yconv_ref, rt_ref, at_ref, bt_ref, kt_ref, bh_ref, kh_ref, v_ref, wc_ref, gr_ref, bg_ref,
                 q_ref, kq_ref, vq_ref, gm_ref, z_scr, *, tiles_per_seq):
    hm_refs = (rt_ref, at_ref, bt_ref, kt_ref, bh_ref, kh_ref, v_ref)
    tm = h_ref.shape[0]
    hx = jnp.concatenate([halo_ref[...], h_ref[...]], axis=0)
    u = _rms(hx, ng_ref[...]).astype(BF16)
    z_scr[:, :C_MG] = jnp.dot(u, wa_ref[...], preferred_element_type=F32)
    z_scr[:, C_MG:] = jnp.dot(u, wb_ref[...], preferred_element_type=F32)

    first = pl.program_id(0) % tiles_per_seq == 0
    row = lax.broadcasted_iota(jnp.int32, (tm, 1), 0)

    def cols(c, w, back=0):
        x = z_scr[pl.ds(HALO - back, tm), c:c + w]
        return jnp.where(row >= jnp.where(first, back, 0), x, 0.0) if back else x

    cwt = cwt_ref[...]
    conv = sum(cols(C_CONV + 256, 256, back) * cols(C_CONV + 512, 256, back) * cwt[2 - back:3 - back, :]
               for back in range(3))
    yconv_ref[...] = (cols(C_CONV, 256) * conv * _silu(cols(C_CONV + 768, 256))).astype(BF16)

    cur = cols(C_RKV, 768)
    rkv = cur + (cols(C_RKV, 768, 1) - cur) * mu_rkv_ref[...]
    cur = cols(C_WA, 128)
    wa = cur + (cols(C_WA, 128, 1) - cur) * mu_wa_ref[...]
    r = rkv[:, 0:256]
    k = rkv[:, 256:512]
    v = rkv[:, 512:768]
    lane = lax.broadcasted_iota(jnp.int32, wa.shape, 1)
    lora_in = jnp.where(lane < LORA, jnp.tanh(wa), wa)
    lora = jnp.dot(lora_in.astype(BF16), wl_ref[...], preferred_element_type=F32)
    wlog = -DECAY_SCALE * jax.nn.sigmoid(w0_ref[...] + lora[:, :RWKV_WIDTH])
    a = jax.nn.sigmoid(a0_ref[...] + lora[:, RWKV_WIDTH:])
    kk = k * kk_ref[...]
    kk = kk * lax.rsqrt(_dot_mask(kk * kk, seg_ref[...]) + 1e-12)
    kmod = k * (1.0 + (a - 1.0) * ka_ref[...])
    zb = kk * a
    cw = _chunk_cumsum(wlog)
    total = _dot_mask(wlog, csum_ref[...], left=True)
    nchunk = tm // CHUNK
    rest = jnp.broadcast_to(total[:, None, :], (nchunk, CHUNK, RWKV_WIDTH)).reshape(tm, RWKV_WIDTH) - cw
    inv = jnp.exp(-cw)
    to_end = jnp.exp(rest)
    wc = jnp.exp(total)
    ops = (r * jnp.exp(cw), -kk * jnp.exp(cw - wlog), zb * inv, kmod * inv, zb * to_end, kmod * to_end, v)
    zpad = jnp.zeros((tm, LANE - RWKV_HEAD_DIM), BF16)
    for hd in range(RWKV_HEADS):
        sl = slice(hd * RWKV_HEAD_DIM, (hd + 1) * RWKV_HEAD_DIM)
        for ref, val in zip(hm_refs, ops):
            piece = val[:, sl].astype(BF16)
            if ref is at_ref:
                piece = jnp.concatenate([piece, zpad], axis=1)
            elif ref is v_ref:
                piece = jnp.concatenate([zpad, piece], axis=1)
            ref[0, hd] = piece
        wc_ref[0, hd] = wc[:, sl]
    gate_r = _silu(cols(C_RG, 256))
    bonus = _dot_mask(r * kmod * rk_ref[...], seg_ref[...]) * v
    gr_ref[...] = gate_r
    bg_ref[...] = bonus * gate_r

    ck = ck_ref[...]
    sk = sk_ref[...]
    qn = _rms(cols(C_QA, Q_LORA_RANK), qg_ref[...]).astype(BF16)
    qm = jnp.dot(qn, wq_ref[...], preferred_element_type=F32)
    scale = math.log2(math.e) / math.sqrt(QK_NOPE_DIM + QK_ROPE_DIM)
    kvn = _rms(cols(C_KVA, KV_LORA_RANK), kvg_ref[...]).astype(BF16)
    kv = jnp.dot(kvn, wkv_ref[...], preferred_element_type=F32)
    kx = cols(C_KR, LANE)
    kr = kx * ck + _swap_rope_halves(kx) * sk
    for hd in range(MLA_HEADS):
        o = hd * QK_PAD
        qx = qm[:, o + LANE:o + QK_PAD]
        qr = qx * ck + _swap_rope_halves(qx) * sk
        q_ref[:, o:o + LANE] = (qm[:, o:o + LANE] * scale).astype(BF16)
        q_ref[:, o + LANE:o + QK_PAD] = (qr * scale).astype(BF16)
        kq_ref[:, o:o + LANE] = kv[:, hd * LANE:(hd + 1) * LANE].astype(BF16)
        kq_ref[:, o + LANE:o + QK_PAD] = kr.astype(BF16)
    vq_ref[0] = kv[:, MLA_HEADS * QK_NOPE_DIM:].T.astype(BF16)
    gm_ref[...] = _silu(cols(C_MG, MLA_WIDTH))


def _const_spec(arr):
    return pl.BlockSpec(arr.shape, lambda *_: (0,) * arr.ndim)


def _layer_spec(arr, layer):
    zeros = (0,) * (arr.ndim - 1)
    return pl.BlockSpec((None,) + arr.shape[1:], lambda *_: (layer,) + zeros)


def _proj_call(h, lw, layer, ck, sk, batch, seq):
    t = h.shape[0]
    tiles_per_seq = seq // TM
    row = lambda w: pl.BlockSpec((TM, w), lambda i: (i, 0))
    halo = pl.BlockSpec((HALO, D_MODEL), lambda i: (jnp.maximum(i * (TM // HALO) - 1, 0), 0))
    chunk_id = np.arange(TM) // CHUNK
    csum = jnp.asarray(np.arange(TM // CHUNK)[:, None] == chunk_id[None, :], BF16)
    head = np.arange(RWKV_WIDTH) // RWKV_HEAD_DIM
    seg = jnp.asarray(head[:, None] == head[None, :], BF16)
    per_layer = lambda *names: [(lw[n], _layer_spec(lw[n], layer)) for n in names]
    shared = lambda *arrs: [(a, _const_spec(a)) for a in arrs]
    consts = (per_layer("ng", "win_a", "win_b", "cw", "mu_rkv", "mu_wa", "w0", "a0", "wl", "kk", "ka", "rk")
              + shared(seg, csum) + per_layer("qg", "wq", "kvg", "wkv"))
    hm_widths = [RWKV_HEAD_DIM, LANE] + [RWKV_HEAD_DIM] * 4 + [LANE]
    hm = [pl.BlockSpec((1, RWKV_HEADS, TM, w), lambda i: (i // tiles_per_seq, 0, i % tiles_per_seq, 0))
          for w in hm_widths]
    hm_shapes = [jax.ShapeDtypeStruct((batch, RWKV_HEADS, seq, w), BF16) for w in hm_widths]
    wc_spec = pl.BlockSpec((1, RWKV_HEADS, TM // CHUNK, RWKV_HEAD_DIM),
                           lambda i: (i // tiles_per_seq, 0, i % tiles_per_seq, 0))
    out_shape = [jax.ShapeDtypeStruct((t, CONV_WIDTH), BF16)] + hm_shapes + [
        jax.ShapeDtypeStruct((batch, RWKV_HEADS, seq // CHUNK, RWKV_HEAD_DIM), F32),
        jax.ShapeDtypeStruct((t, RWKV_WIDTH), F32),
        jax.ShapeDtypeStruct((t, RWKV_WIDTH), F32),
        jax.ShapeDtypeStruct((t, MLA_HEADS * QK_PAD), BF16),
        jax.ShapeDtypeStruct((t, MLA_HEADS * QK_PAD), BF16),
        jax.ShapeDtypeStruct((t // TM, MLA_WIDTH, TM), BF16),
        jax.ShapeDtypeStruct((t, MLA_WIDTH), F32),
    ]
    vt_spec = pl.BlockSpec((1, MLA_WIDTH, TM), lambda i: (i, 0, 0))
    out_specs = [row(CONV_WIDTH)] + hm + [wc_spec, row(RWKV_WIDTH), row(RWKV_WIDTH),
                                                  row(MLA_HEADS * QK_PAD), row(MLA_HEADS * QK_PAD),
                                                  vt_spec, row(MLA_WIDTH)]
    return pl.pallas_call(
        functools.partial(_proj_kernel, tiles_per_seq=tiles_per_seq),
        grid=(t // TM,),
        in_specs=[row(D_MODEL), halo] + [s for _, s in consts] + [row(LANE), row(LANE)],
        out_specs=out_specs,
        out_shape=out_shape,
        scratch_shapes=[pltpu.VMEM((HALO + TM, D_IN_P), F32)],
        compiler_params=pltpu.CompilerParams(dimension_semantics=("arbitrary",),
                                             vmem_limit_bytes=VMEM_LIMIT),
        name="proj",
    )(h, h, *[a for a, _ in consts], ck, sk)


def _rwkv_chunk_local(r_t, a_t, b_t, k_t, bh, kh, v, wc, tri_incl, tri_strict, eye):
    n = range(len(r_t))
    half = RWKV_HEAD_DIM
    lhs = [jnp.concatenate([a_t[i][:, :half], r_t[i]], axis=0) for i in n]
    bk = [jnp.concatenate([b_t[i], k_t[i]], axis=0) for i in n]
    abk = [_dot_nt(lhs[i], bk[i]) for i in n]
    top = [jnp.where(tri_strict, abk[i][:CHUNK], 0.0).astype(BF16) for i in n]
    bot = [jnp.where(tri_incl, abk[i][CHUNK:], 0.0).astype(BF16) for i in n]

    zeros = jnp.zeros((CHUNK, LANE), BF16)
    x = [a_t[i].astype(F32) + _dot(top[i], jnp.concatenate([zeros, v[i]], axis=0)) for i in n]
    p = [top[i][:, :half] for i in n]
    for it in range(6):
        x = [x[i] + _dot(p[i], x[i]) for i in n]
        if it < 5:
            p = [_dot(p[i], p[i]).astype(BF16) for i in n]
    xv = [jnp.concatenate([x[i].astype(BF16), v[i]], axis=0) for i in n]

    yloc = [_dot(bot[i], xv[i]) for i in n]
    qp = [r_t[i].astype(F32) + yloc[i][:, :half] for i in n]
    nc = [_dot_tn(jnp.concatenate([bh[i], kh[i]], axis=0), xv[i]) for i in n]
    mc = [jnp.where(eye, wc[i], 0.0) + nc[i][:, :half] for i in n]
    return qp, yloc, mc, nc


def _rwkv_kernel(rt_ref, at_ref, bt_ref, kt_ref, bh_ref, kh_ref, v_ref, wc_ref, gate_ref, bg_ref,
                 gnw_ref, gnb_ref, gnm_ref, y_ref, state_ref):
    step = pl.program_id(0)
    batch = rt_ref.shape[0]

    @pl.when(step == 0)
    def _():
        state_ref[...] = jnp.zeros_like(state_ref)

    ri = lax.broadcasted_iota(jnp.int32, (CHUNK, LANE), 0)
    li = lax.broadcasted_iota(jnp.int32, (CHUNK, LANE), 1)
    ci = li % CHUNK
    tri_incl = ri >= ci
    tri_strict = ri > ci
    eye = (ri == li)[:, :RWKV_HEAD_DIM]
    value_lanes = li >= RWKV_HEAD_DIM
    seqs = [(b, hd) for b in range(batch) for hd in range(RWKV_HEADS)]
    items = [(c, b, hd) for c in range(RWKV_CHUNKS_PER_STEP) for b, hd in seqs]
    n = range(len(items))
    load = lambda ref: [ref[b, hd, pl.ds(c * CHUNK, CHUNK), :] for c, b, hd in items]
    wc = [wc_ref[b, hd, pl.ds(step * RWKV_CHUNKS_PER_STEP + c, 1), :] for c, b, hd in items]
    qp, yloc, mc, nc = _rwkv_chunk_local(load(rt_ref), load(at_ref), load(bt_ref), load(kt_ref),
                                         load(bh_ref), load(kh_ref), load(v_ref), wc,
                                         tri_incl, tri_strict, eye)
    ns = range(len(seqs))
    states = [state_ref[s] for s in ns]
    y = []
    for c in range(RWKV_CHUNKS_PER_STEP):
        o = c * len(seqs)
        y += [_dot(qp[o + s], states[s]) + yloc[o + s] for s in ns]
        states = [jnp.where(value_lanes, _dot(mc[o + s], states[s]) + nc[o + s], 0.0) for s in ns]
    for s in ns:
        state_ref[s] = states[s]
    gnm = gnm_ref[...]
    head = [hd for _, _, hd in items]
    yall = jnp.concatenate(y, axis=0)
    yc = yall - _dot(yall, gnm)
    scale = lax.rsqrt(_dot(yc * yc, gnm) + GN_EPS)
    part = lambda a, i: a[i * CHUNK:(i + 1) * CHUNK]
    gn = [part(yc, i) * part(scale, i) * gnw_ref[head[i]:head[i] + 1, :]
          + gnb_ref[head[i]:head[i] + 1, :] for i in n]
    for c in range(RWKV_CHUNKS_PER_STEP):
        rows = pl.ds(c * CHUNK, CHUNK)
        for b in range(batch):
            o = (c * batch + b) * RWKV_HEADS
            pair = [pltpu.roll(gn[o + hd], RWKV_HEAD_DIM, axis=1) + gn[o + hd + 1] for hd in (0, 2)]
            y_ref[b, rows, :] = (jnp.concatenate(pair, axis=1) * gate_ref[b, rows, :]
                                 + bg_ref[b, rows, :]).astype(BF16)


def _rwkv_call(hm, wc, gate, bg, lw, layer, batch, seq):
    tc = RWKV_CHUNKS_PER_STEP * CHUNK
    hm_spec = lambda a: pl.BlockSpec((batch, RWKV_HEADS, tc, a.shape[-1]), lambda c: (0, 0, c, 0))
    nat = pl.BlockSpec((batch, tc, RWKV_WIDTH), lambda c: (0, c, 0))
    value = np.arange(LANE) >= RWKV_HEAD_DIM
    gnm = jnp.asarray((value[:, None] & value[None, :]) / RWKV_HEAD_DIM, BF16)
    return pl.pallas_call(
        _rwkv_kernel,
        grid=(seq // tc,),
        in_specs=[hm_spec(a) for a in hm] + [_const_spec(wc), nat, nat,
                                             _layer_spec(lw["gnw"], layer), _layer_spec(lw["gnb"], layer),
                                             _const_spec(gnm)],
        out_specs=nat,
        out_shape=jax.ShapeDtypeStruct((batch, seq, RWKV_WIDTH), BF16),
        scratch_shapes=[pltpu.VMEM((batch * RWKV_HEADS, RWKV_HEAD_DIM, LANE), F32)],
        compiler_params=pltpu.CompilerParams(dimension_semantics=("arbitrary",),
                                             vmem_limit_bytes=VMEM_LIMIT),
        name="rwkv",
    )(*hm, wc, gate.reshape(batch, seq, RWKV_WIDTH), bg.reshape(batch, seq, RWKV_WIDTH),
      lw["gnw"], lw["gnb"], gnm)


def _attn_kernel(q_ref, k_ref, vt_ref, g_ref, o_ref, m_scr, acc_scr, s0_scr, s1_scr, c0_scr, c1_scr):
    tq = s0_scr.shape[1]
    n_q = q_ref.shape[1] // tq
    bufs = ((s0_scr, c0_scr), (s1_scr, c1_scr))
    ones = jnp.ones((BF16_ROWS, tq), BF16)
    tiles = [(i, j) for i in range(n_q) for j in range(i + 1)]

    def scores(i, j, slot):
        s_ref, c_ref = bufs[slot]
        k = k_ref[0, j * tq:(j + 1) * tq, :]
        q = q_ref[0, i * tq:(i + 1) * tq, :]
        s = lax.dot_general(k, q, (((1,), (1,)), ((), ())), preferred_element_type=F32)
        if i == j:
            kc = lax.broadcasted_iota(jnp.int32, s.shape, 0) // CHUNK
            qc = lax.broadcasted_iota(jnp.int32, s.shape, 1) // CHUNK
            s = jnp.where(kc <= qc, s, -1e30)
        s_ref[...] = s
        c_ref[...] = jnp.max(s, axis=0, keepdims=True)

    def accumulate(i, j, slot):
        s_ref, c_ref = bufs[slot]
        m_new = c_ref[...] if j == 0 else jnp.maximum(m_scr[...], c_ref[...])
        p = jnp.exp2((s_ref[...] - m_new).astype(BF16))
        vt1 = jnp.concatenate([vt_ref[0, j], ones], axis=0)
        acc = jnp.dot(vt1, p, preferred_element_type=F32)
        if j > 0:
            acc = jnp.exp2(m_scr[...] - m_new) * acc_scr[...] + acc
        if j == i:
            rows = slice(i * tq, (i + 1) * tq)
            o = acc[:V_HEAD_DIM] / acc[V_HEAD_DIM:V_HEAD_DIM + 1]
            o_ref[0, rows, :] = (o.T * g_ref[0, rows, :]).astype(BF16)
        else:
            m_scr[...] = m_new
            acc_scr[...] = acc

    scores(*tiles[0], 0)
    for t, (i, j) in enumerate(tiles):
        if t + 1 < len(tiles):
            scores(*tiles[t + 1], (t + 1) % 2)
        accumulate(i, j, t % 2)


def _attn_call(q, k, vt, gate, batch, seq):
    q = q.reshape(batch, seq, MLA_HEADS * QK_PAD)
    k = k.reshape(batch, seq, MLA_HEADS * QK_PAD)
    vt = vt.reshape(batch, seq // TQ, MLA_WIDTH, TQ)
    gate = gate.reshape(batch, seq, MLA_WIDTH)
    return pl.pallas_call(
        _attn_kernel,
        grid=(batch, MLA_HEADS),
        in_specs=[pl.BlockSpec((1, seq, QK_PAD), lambda b, h: (b, 0, h)),
                  pl.BlockSpec((1, seq, QK_PAD), lambda b, h: (b, 0, h)),
                  pl.BlockSpec((1, seq // TQ, V_HEAD_DIM, TQ), lambda b, h: (b, 0, h, 0)),
                  pl.BlockSpec((1, seq, V_HEAD_DIM), lambda b, h: (b, 0, h))],
        out_specs=pl.BlockSpec((1, seq, V_HEAD_DIM), lambda b, h: (b, 0, h)),
        out_shape=jax.ShapeDtypeStruct((batch, seq, MLA_WIDTH), BF16),
        scratch_shapes=[pltpu.VMEM((1, TQ), F32),
                        pltpu.VMEM((V_HEAD_DIM + BF16_ROWS, TQ), F32),
                        pltpu.VMEM((TQ, TQ), F32), pltpu.VMEM((TQ, TQ), F32),
                        pltpu.VMEM((1, TQ), F32), pltpu.VMEM((1, TQ), F32)],
        compiler_params=pltpu.CompilerParams(
            dimension_semantics=("arbitrary", "arbitrary"), vmem_limit_bytes=VMEM_LIMIT),
        name="attn",
    )(q, k, vt, gate)


def _out_kernel(h_ref, yc_ref, yr_ref, ym_ref, wo_ref, p_ref, wple_ref, png_ref, wpg_ref, fg_ref,
                o_ref, *, final):
    ycat = jnp.concatenate([yc_ref[...], yr_ref[...], ym_ref[...]], axis=-1)
    h = h_ref[...] + jnp.dot(ycat, wo_ref[...], preferred_element_type=F32)
    gate = jax.nn.sigmoid(jnp.dot(_rms(h, png_ref[...]).astype(BF16), wpg_ref[...],
                                  preferred_element_type=F32))
    h = h + jnp.dot(p_ref[...].astype(BF16), wple_ref[...], preferred_element_type=F32) * gate
    if final:
        h = _rms(h, fg_ref[...])
    o_ref[...] = h


def _out_call(h, yc, yr, ym, p, layer, lw, final_g, final):
    t = h.shape[0]
    row = lambda w: pl.BlockSpec((TM, w), lambda i: (i, 0))
    return pl.pallas_call(
        functools.partial(_out_kernel, final=final),
        grid=(t // TM,),
        in_specs=[row(D_MODEL), row(CONV_WIDTH), row(RWKV_WIDTH), row(MLA_WIDTH),
                  _layer_spec(lw["wo"], layer),
                  pl.BlockSpec((None, TM, PLE_DIM), lambda i: (layer, i, 0)),
                  _layer_spec(lw["wple"], layer), _layer_spec(lw["png"], layer),
                  _layer_spec(lw["wpg"], layer), _const_spec(final_g)],
        out_specs=row(D_MODEL),
        out_shape=jax.ShapeDtypeStruct((t, D_MODEL), F32),
        compiler_params=pltpu.CompilerParams(dimension_semantics=("arbitrary",),
                                             vmem_limit_bytes=VMEM_LIMIT),
        name="out",
    )(h, yc, yr, ym, lw["wo"], p, lw["wple"], lw["png"], lw["wpg"], final_g)


def _prep(norm_mix_g, w_in, conv_w, rwkv_mu, rwkv_w0, rwkv_w2, rwkv_a0, rwkv_a2, rwkv_kk, rwkv_ka, rwkv_rk,
          rwkv_gn_w, rwkv_gn_b, mla_q_norm_g, mla_w_qb, mla_kv_norm_g, mla_w_kvb, w_out, ple_w, ple_norm_g,
          ple_gate_w):
    depth = w_in.shape[0]
    o_kr = C_MG
    o_mg = o_kr + QK_ROPE_DIM
    zeros = jnp.zeros((depth, D_MODEL, QK_ROPE_DIM), F32)
    win_a = w_in[:, :, :o_kr].astype(BF16)
    win_b = jnp.concatenate([w_in[:, :, o_mg:], w_in[:, :, o_kr:o_mg], zeros], axis=2).astype(BF16)

    zl = jnp.zeros((depth, LORA, RWKV_WIDTH), F32)
    wl = jnp.concatenate([jnp.concatenate([rwkv_w2, zl], axis=2),
                          jnp.concatenate([zl, rwkv_a2], axis=2)], axis=1).astype(BF16)

    wqb = mla_w_qb.reshape(depth, Q_LORA_RANK, MLA_HEADS, QK_NOPE_DIM + QK_ROPE_DIM)
    zq = jnp.zeros((depth, Q_LORA_RANK, MLA_HEADS, QK_ROPE_DIM), F32)
    wq = jnp.concatenate([wqb, zq], axis=-1).reshape(depth, Q_LORA_RANK, MLA_HEADS * QK_PAD).astype(BF16)
    wkvb = mla_w_kvb.reshape(depth, KV_LORA_RANK, MLA_HEADS, QK_NOPE_DIM + V_HEAD_DIM)
    wkv = jnp.concatenate([wkvb[..., :QK_NOPE_DIM].reshape(depth, KV_LORA_RANK, -1),
                           wkvb[..., QK_NOPE_DIM:].reshape(depth, KV_LORA_RANK, -1)], axis=2).astype(BF16)
    row = lambda x: x.reshape(depth, 1, -1)
    hd = lambda x: jnp.pad(x.reshape(depth, RWKV_HEADS, RWKV_HEAD_DIM),
                           ((0, 0), (0, 0), (LANE - RWKV_HEAD_DIM, 0)))
    return dict(
        ng=row(norm_mix_g), win_a=win_a, win_b=win_b, cw=conv_w,
        mu_rkv=row(rwkv_mu[:, :3 * RWKV_WIDTH]), mu_wa=row(rwkv_mu[:, 3 * RWKV_WIDTH:]),
        w0=row(rwkv_w0), a0=row(rwkv_a0), wl=wl, kk=row(rwkv_kk), ka=row(rwkv_ka),
        qg=row(mla_q_norm_g), wq=wq, kvg=row(mla_kv_norm_g), wkv=wkv,
        rk=row(rwkv_rk), gnw=hd(rwkv_gn_w), gnb=hd(rwkv_gn_b),
        wo=w_out.astype(BF16), wple=ple_w.astype(BF16), png=row(ple_norm_g), wpg=ple_gate_w.astype(BF16))


def kernel(x, p, positions, norm_mix_g, w_in, conv_w, rwkv_mu, rwkv_w0, rwkv_w2, rwkv_a0, rwkv_a2,
           rwkv_kk, rwkv_ka, rwkv_rk, rwkv_gn_w, rwkv_gn_b, mla_q_norm_g, mla_w_qb, mla_kv_norm_g,
           mla_w_kvb, w_out, ple_w, ple_norm_g, ple_gate_w, final_norm_g):
    batch, seq, _ = x.shape
    depth = w_in.shape[0]
    t = batch * seq

    half = QK_ROPE_DIM // 2
    inv_freq = 1.0 / (ROPE_THETA ** (jnp.arange(0, QK_ROPE_DIM, 2, dtype=F32) / QK_ROPE_DIM))
    inv_tile = jnp.concatenate([inv_freq, inv_freq, jnp.zeros((LANE - QK_ROPE_DIM,), F32)])
    lane = np.arange(LANE)
    ang = positions.astype(F32).reshape(t, 1) * inv_tile
    ck = jnp.where(lane < QK_ROPE_DIM, jnp.cos(ang), 0.0)
    sk = jnp.sin(ang) * jnp.asarray(np.where(lane < half, -1.0, 1.0), F32)

    h = x.reshape(t, D_MODEL)
    p = p.reshape(depth, t, PLE_DIM)
    final_g = final_norm_g.reshape(1, D_MODEL)
    lw = _prep(norm_mix_g, w_in, conv_w, rwkv_mu, rwkv_w0, rwkv_w2, rwkv_a0, rwkv_a2, rwkv_kk, rwkv_ka,
               rwkv_rk, rwkv_gn_w, rwkv_gn_b, mla_q_norm_g, mla_w_qb, mla_kv_norm_g, mla_w_kvb, w_out,
               ple_w, ple_norm_g, ple_gate_w)
    for i in range(depth):
        (yconv, *hm, wc, gate_r, bg, q, kq, vq, gate_m) = _proj_call(h, lw, i, ck, sk, batch, seq)
        y_rwkv = _rwkv_call(hm, wc, gate_r, bg, lw, i, batch, seq).reshape(t, RWKV_WIDTH)
        y_mla = _attn_call(q, kq, vq, gate_m, batch, seq).reshape(t, MLA_WIDTH)
        h = _out_call(h, yconv, y_rwkv, y_mla, p, i, lw, final_g, final=(i == depth - 1))
    return h.reshape(batch, seq, D_MODEL)
```

```python
import functools
import math

import jax
import jax.numpy as jnp
import numpy as np
from jax import lax
from jax.experimental import pallas as pl
from jax.experimental.pallas import tpu as pltpu

F32 = jnp.float32
BF16 = jnp.bfloat16

D_MODEL = 1024
CHUNK = 64
PLE_DIM = 256
NORM_EPS = 1e-6
CONV_WIDTH = 256
RWKV_HEADS = 4
RWKV_HEAD_DIM = 64
RWKV_WIDTH = RWKV_HEADS * RWKV_HEAD_DIM
LORA = 64
DECAY_SCALE = math.exp(-0.5)
GN_EPS = 64e-5
MLA_HEADS = 4
QK_NOPE_DIM = 128
QK_ROPE_DIM = 64
V_HEAD_DIM = 128
Q_LORA_RANK = 384
KV_LORA_RANK = 256
MLA_WIDTH = MLA_HEADS * V_HEAD_DIM
ROPE_THETA = 10000.0
D_MIX = CONV_WIDTH + RWKV_WIDTH + MLA_WIDTH

C_CONV = 0
C_RKV = 1024
C_WA = 1792
C_RG = 1920
C_QA = 2176
C_KVA = 2560
C_MG = 2816
C_KR = 3328
D_IN_P = 3456
QK_PAD = 256

LANE = 128
BF16_ROWS = 16
HALO = BF16_ROWS
TM = 512
TQ = 512
RWKV_CHUNKS_PER_STEP = 4
VMEM_LIMIT = 56 * 1024 * 1024


def _dot(a, b):
    return jnp.dot(a.astype(BF16), b.astype(BF16), preferred_element_type=F32)


def _dot_nt(a, b):
    return lax.dot_general(a.astype(BF16), b.astype(BF16), (((1,), (1,)), ((), ())),
                           preferred_element_type=F32)


def _dot_tn(a, b):
    return lax.dot_general(a.astype(BF16), b.astype(BF16), (((0,), (0,)), ((), ())),
                           preferred_element_type=F32)


def _split2(x):
    hi = x.astype(BF16)
    return hi, (x - hi.astype(F32)).astype(BF16)


def _dot_mask(x, m01, left=False):
    hi, lo = _split2(x)
    if left:
        return jnp.dot(m01, hi, preferred_element_type=F32) + jnp.dot(m01, lo, preferred_element_type=F32)
    return jnp.dot(hi, m01, preferred_element_type=F32) + jnp.dot(lo, m01, preferred_element_type=F32)


def _chunk_cumsum(x):
    pos = lax.broadcasted_iota(jnp.int32, x.shape, 0) % CHUNK
    d = 1
    while d < CHUNK:
        x = x + jnp.where(pos >= d, pltpu.roll(x, d, axis=0), 0.0)
        d *= 2
    return x


def _swap_rope_halves(x):
    half = QK_ROPE_DIM // 2
    return pltpu.roll(x, half, axis=1) + pltpu.roll(x, LANE - half, axis=1)


def _rms(x, g):
    return x * lax.rsqrt(jnp.mean(x * x, axis=-1, keepdims=True) + NORM_EPS) * g


def _silu(x):
    return x * jax.nn.sigmoid(x)


def _proj_kernel(h_ref, halo_ref, ng_ref, wa_ref, wb_ref, cwt_ref, mu_rkv_ref, mu_wa_ref, w0_ref, a0_ref,
                 wl_ref, kk_ref, ka_ref, rk_ref, seg_ref, csum_ref, qg_ref, wq_ref, kvg_ref, wkv_ref,
                 ck_ref, sk_ref,
                 yconv_ref, rt_ref, at_ref, bt_ref, kt_ref, bh_ref, kh_ref, v_ref, wc_ref, gr_ref, bg_ref,
                 q_ref, kq_ref, vq_ref, gm_ref, z_scr, *, tiles_per_seq):
    hm_refs = (rt_ref, at_ref, bt_ref, kt_ref, bh_ref, kh_ref, v_ref)
    tm = h_ref.shape[0]
    hx = jnp.concatenate([halo_ref[...], h_ref[...]], axis=0)
    u = _rms(hx, ng_ref[...]).astype(BF16)
    z_scr[:, :C_MG] = jnp.dot(u, wa_ref[...], preferred_element_type=F32)
    z_scr[:, C_MG:] = jnp.dot(u, wb_ref[...], preferred_element_type=F32)

    first = pl.program_id(0) % tiles_per_seq == 0
    row = lax.broadcasted_iota(jnp.int32, (tm, 1), 0)

    def cols(c, w, back=0):
        x = z_scr[pl.ds(HALO - back, tm), c:c + w]
        return jnp.where(row >= jnp.where(first, back, 0), x, 0.0) if back else x

    cwt = cwt_ref[...]
    conv = sum(cols(C_CONV + 256, 256, back) * cols(C_CONV + 512, 256, back) * cwt[2 - back:3 - back, :]
               for back in range(3))
    yconv_ref[...] = (cols(C_CONV, 256) * conv * _silu(cols(C_CONV + 768, 256))).astype(BF16)

    cur = cols(C_RKV, 768)
    rkv = cur + (cols(C_RKV, 768, 1) - cur) * mu_rkv_ref[...]
    cur = cols(C_WA, 128)
    wa = cur + (cols(C_WA, 128, 1) - cur) * mu_wa_ref[...]
    r = rkv[:, 0:256]
    k = rkv[:, 256:512]
    v = rkv[:, 512:768]
    lane = lax.broadcasted_iota(jnp.int32, wa.shape, 1)
    lora_in = jnp.where(lane < LORA, jnp.tanh(wa), wa)
    lora = jnp.dot(lora_in.astype(BF16), wl_ref[...], preferred_element_type=F32)
    wlog = -DECAY_SCALE * jax.nn.sigmoid(w0_ref[...] + lora[:, :RWKV_WIDTH])
    a = jax.nn.sigmoid(a0_ref[...] + lora[:, RWKV_WIDTH:])
    kk = k * kk_ref[...]
    kk = kk * lax.rsqrt(_dot_mask(kk * kk, seg_ref[...]) + 1e-12)
    kmod = k * (1.0 + (a - 1.0) * ka_ref[...])
    zb = kk * a
    cw = _chunk_cumsum(wlog)
    total = _dot_mask(wlog, csum_ref[...], left=True)
    nchunk = tm // CHUNK
    rest = jnp.broadcast_to(total[:, None, :], (nchunk, CHUNK, RWKV_WIDTH)).reshape(tm, RWKV_WIDTH) - cw
    inv = jnp.exp(-cw)
    to_end = jnp.exp(rest)
    wc = jnp.exp(total)
    ops = (r * jnp.exp(cw), -kk * jnp.exp(cw - wlog), zb * inv, kmod * inv, zb * to_end, kmod * to_end, v)
    zpad = jnp.zeros((tm, LANE - RWKV_HEAD_DIM), BF16)
    for hd in range(RWKV_HEADS):
        sl = slice(hd * RWKV_HEAD_DIM, (hd + 1) * RWKV_HEAD_DIM)
        for ref, val in zip(hm_refs, ops):
            piece = val[:, sl].astype(BF16)
            if ref is at_ref:
                piece = jnp.concatenate([piece, zpad], axis=1)
            elif ref is v_ref:
                piece = jnp.concatenate([zpad, piece], axis=1)
            ref[0, hd] = piece
        wc_ref[0, hd] = wc[:, sl]
    gate_r = _silu(cols(C_RG, 256))
    bonus = _dot_mask(r * kmod * rk_ref[...], seg_ref[...]) * v
    gr_ref[...] = gate_r
    bg_ref[...] = bonus * gate_r

    ck = ck_ref[...]
    sk = sk_ref[...]
    qn = _rms(cols(C_QA, Q_LORA_RANK), qg_ref[...]).astype(BF16)
    qm = jnp.dot(qn, wq_ref[...], preferred_element_type=F32)
    scale = math.log2(math.e) / math.sqrt(QK_NOPE_DIM + QK_ROPE_DIM)
    kvn = _rms(cols(C_KVA, KV_LORA_RANK), kvg_ref[...]).astype(BF16)
    kv = jnp.dot(kvn, wkv_ref[...], preferred_element_type=F32)
    kx = cols(C_KR, LANE)
    kr = kx * ck + _swap_rope_halves(kx) * sk
    for hd in range(MLA_HEADS):
        o = hd * QK_PAD
        qx = qm[:, o + LANE:o + QK_PAD]
        qr = qx * ck + _swap_rope_halves(qx) * sk
        q_ref[0, o:o + LANE, :] = (qm[:, o:o + LANE] * scale).T.astype(BF16)
        q_ref[0, o + LANE:o + QK_PAD, :] = (qr * scale).T.astype(BF16)
        kq_ref[:, o:o + LANE] = kv[:, hd * LANE:(hd + 1) * LANE].astype(BF16)
        kq_ref[:, o + LANE:o + QK_PAD] = kr.astype(BF16)
    vq_ref[0] = kv[:, MLA_HEADS * QK_NOPE_DIM:].T.astype(BF16)
    gm_ref[...] = _silu(cols(C_MG, MLA_WIDTH))


def _const_spec(arr):
    return pl.BlockSpec(arr.shape, lambda *_: (0,) * arr.ndim)


def _layer_spec(arr, layer):
    zeros = (0,) * (arr.ndim - 1)
    return pl.BlockSpec((None,) + arr.shape[1:], lambda *_: (layer,) + zeros)


def _proj_call(h, lw, layer, ck, sk, batch, seq):
    t = h.shape[0]
    tiles_per_seq = seq // TM
    row = lambda w: pl.BlockSpec((TM, w), lambda i: (i, 0))
    halo = pl.BlockSpec((HALO, D_MODEL), lambda i: (jnp.maximum(i * (TM // HALO) - 1, 0), 0))
    chunk_id = np.arange(TM) // CHUNK
    csum = jnp.asarray(np.arange(TM // CHUNK)[:, None] == chunk_id[None, :], BF16)
    head = np.arange(RWKV_WIDTH) // RWKV_HEAD_DIM
    seg = jnp.asarray(head[:, None] == head[None, :], BF16)
    per_layer = lambda *names: [(lw[n], _layer_spec(lw[n], layer)) for n in names]
    shared = lambda *arrs: [(a, _const_spec(a)) for a in arrs]
    consts = (per_layer("ng", "win_a", "win_b", "cw", "mu_rkv", "mu_wa", "w0", "a0", "wl", "kk", "ka", "rk")
              + shared(seg, csum) + per_layer("qg", "wq", "kvg", "wkv"))
    hm_widths = [RWKV_HEAD_DIM, LANE] + [RWKV_HEAD_DIM] * 4 + [LANE]
    hm = [pl.BlockSpec((1, RWKV_HEADS, TM, w), lambda i: (i // tiles_per_seq, 0, i % tiles_per_seq, 0))
          for w in hm_widths]
    hm_shapes = [jax.ShapeDtypeStruct((batch, RWKV_HEADS, seq, w), BF16) for w in hm_widths]
    wc_spec = pl.BlockSpec((1, RWKV_HEADS, TM // CHUNK, RWKV_HEAD_DIM),
                           lambda i: (i // tiles_per_seq, 0, i % tiles_per_seq, 0))
    out_shape = [jax.ShapeDtypeStruct((t, CONV_WIDTH), BF16)] + hm_shapes + [
        jax.ShapeDtypeStruct((batch, RWKV_HEADS, seq // CHUNK, RWKV_HEAD_DIM), F32),
        jax.ShapeDtypeStruct((t, RWKV_WIDTH), F32),
        jax.ShapeDtypeStruct((t, RWKV_WIDTH), F32),
        jax.ShapeDtypeStruct((t // TM, MLA_HEADS * QK_PAD, TM), BF16),
        jax.ShapeDtypeStruct((t, MLA_HEADS * QK_PAD), BF16),
        jax.ShapeDtypeStruct((t // TM, MLA_WIDTH, TM), BF16),
        jax.ShapeDtypeStruct((t, MLA_WIDTH), F32),
    ]
    vt_spec = pl.BlockSpec((1, MLA_WIDTH, TM), lambda i: (i, 0, 0))
    qt_spec = pl.BlockSpec((1, MLA_HEADS * QK_PAD, TM), lambda i: (i, 0, 0))
    out_specs = [row(CONV_WIDTH)] + hm + [wc_spec, row(RWKV_WIDTH), row(RWKV_WIDTH),
                                                  qt_spec, row(MLA_HEADS * QK_PAD),
                                                  vt_spec, row(MLA_WIDTH)]
    return pl.pallas_call(
        functools.partial(_proj_kernel, tiles_per_seq=tiles_per_seq),
        grid=(t // TM,),
        in_specs=[row(D_MODEL), halo] + [s for _, s in consts] + [row(LANE), row(LANE)],
        out_specs=out_specs,
        out_shape=out_shape,
        scratch_shapes=[pltpu.VMEM((HALO + TM, D_IN_P), F32)],
        compiler_params=pltpu.CompilerParams(dimension_semantics=("arbitrary",),
                                             vmem_limit_bytes=VMEM_LIMIT),
        name="proj",
    )(h, h, *[a for a, _ in consts], ck, sk)


def _rwkv_chunk_local(r_t, a_t, b_t, k_t, bh, kh, v, wc, tri_incl, tri_strict, eye):
    n = range(len(r_t))
    half = RWKV_HEAD_DIM
    lhs = [jnp.concatenate([a_t[i][:, :half], r_t[i]], axis=0) for i in n]
    bk = [jnp.concatenate([b_t[i], k_t[i]], axis=0) for i in n]
    abk = [_dot_nt(lhs[i], bk[i]) for i in n]
    top = [jnp.where(tri_strict, abk[i][:CHUNK], 0.0).astype(BF16) for i in n]
    bot = [jnp.where(tri_incl, abk[i][CHUNK:], 0.0).astype(BF16) for i in n]

    zeros = jnp.zeros((CHUNK, LANE), BF16)
    x = [a_t[i].astype(F32) + _dot(top[i], jnp.concatenate([zeros, v[i]], axis=0)) for i in n]
    p = [top[i][:, :half] for i in n]
    for it in range(6):
        x = [x[i] + _dot(p[i], x[i]) for i in n]
        if it < 5:
            p = [_dot(p[i], p[i]).astype(BF16) for i in n]
    xv = [jnp.concatenate([x[i].astype(BF16), v[i]], axis=0) for i in n]

    yloc = [_dot(bot[i], xv[i]) for i in n]
    qp = [r_t[i].astype(F32) + yloc[i][:, :half] for i in n]
    nc = [_dot_tn(jnp.concatenate([bh[i], kh[i]], axis=0), xv[i]) for i in n]
    mc = [jnp.where(eye, wc[i], 0.0) + nc[i][:, :half] for i in n]
    return qp, yloc, mc, nc


def _rwkv_kernel(rt_ref, at_ref, bt_ref, kt_ref, bh_ref, kh_ref, v_ref, wc_ref, gate_ref, bg_ref,
                 gnw_ref, gnb_ref, gnm_ref, y_ref, state_ref):
    step = pl.program_id(0)
    batch = rt_ref.shape[0]

    @pl.when(step == 0)
    def _():
        state_ref[...] = jnp.zeros_like(state_ref)

    ri = lax.broadcasted_iota(jnp.int32, (CHUNK, LANE), 0)
    li = lax.broadcasted_iota(jnp.int32, (CHUNK, LANE), 1)
    ci = li % CHUNK
    tri_incl = ri >= ci
    tri_strict = ri > ci
    eye = (ri == li)[:, :RWKV_HEAD_DIM]
    value_lanes = li >= RWKV_HEAD_DIM
    seqs = [(b, hd) for b in range(batch) for hd in range(RWKV_HEADS)]
    items = [(c, b, hd) for c in range(RWKV_CHUNKS_PER_STEP) for b, hd in seqs]
    n = range(len(items))
    load = lambda ref: [ref[b, hd, pl.ds(c * CHUNK, CHUNK), :] for c, b, hd in items]
    wc = [wc_ref[b, hd, pl.ds(step * RWKV_CHUNKS_PER_STEP + c, 1), :] for c, b, hd in items]
    qp, yloc, mc, nc = _rwkv_chunk_local(load(rt_ref), load(at_ref), load(bt_ref), load(kt_ref),
                                         load(bh_ref), load(kh_ref), load(v_ref), wc,
                                         tri_incl, tri_strict, eye)
    ns = range(len(seqs))
    states = [state_ref[s] for s in ns]
    y = []
    for c in range(RWKV_CHUNKS_PER_STEP):
        o = c * len(seqs)
        y += [_dot(qp[o + s], states[s]) + yloc[o + s] for s in ns]
        states = [jnp.where(value_lanes, _dot(mc[o + s], states[s]) + nc[o + s], 0.0) for s in ns]
    for s in ns:
        state_ref[s] = states[s]
    gnm = gnm_ref[...]
    head = [hd for _, _, hd in items]
    yall = jnp.concatenate(y, axis=0)
    yc = yall - _dot(yall, gnm)
    scale = lax.rsqrt(_dot(yc * yc, gnm) + GN_EPS)
    part = lambda a, i: a[i * CHUNK:(i + 1) * CHUNK]
    gn = [part(yc, i) * part(scale, i) * gnw_ref[head[i]:head[i] + 1, :]
          + gnb_ref[head[i]:head[i] + 1, :] for i in n]
    for c in range(RWKV_CHUNKS_PER_STEP):
        rows = pl.ds(c * CHUNK, CHUNK)
        for b in range(batch):
            o = (c * batch + b) * RWKV_HEADS
            pair = [pltpu.roll(gn[o + hd], RWKV_HEAD_DIM, axis=1) + gn[o + hd + 1] for hd in (0, 2)]
            y_ref[b, rows, :] = (jnp.concatenate(pair, axis=1) * gate_ref[b, rows, :]
                                 + bg_ref[b, rows, :]).astype(BF16)


def _rwkv_call(hm, wc, gate, bg, lw, layer, batch, seq):
    tc = RWKV_CHUNKS_PER_STEP * CHUNK
    hm_spec = lambda a: pl.BlockSpec((batch, RWKV_HEADS, tc, a.shape[-1]), lambda c: (0, 0, c, 0))
    nat = pl.BlockSpec((batch, tc, RWKV_WIDTH), lambda c: (0, c, 0))
    value = np.arange(LANE) >= RWKV_HEAD_DIM
    gnm = jnp.asarray((value[:, None] & value[None, :]) / RWKV_HEAD_DIM, BF16)
    return pl.pallas_call(
        _rwkv_kernel,
        grid=(seq // tc,),
        in_specs=[hm_spec(a) for a in hm] + [_const_spec(wc), nat, nat,
                                             _layer_spec(lw["gnw"], layer), _layer_spec(lw["gnb"], layer),
                                             _const_spec(gnm)],
        out_specs=nat,
        out_shape=jax.ShapeDtypeStruct((batch, seq, RWKV_WIDTH), BF16),
        scratch_shapes=[pltpu.VMEM((batch * RWKV_HEADS, RWKV_HEAD_DIM, LANE), F32)],
        compiler_params=pltpu.CompilerParams(dimension_semantics=("arbitrary",),
                                             vmem_limit_bytes=VMEM_LIMIT),
        name="rwkv",
    )(*hm, wc, gate.reshape(batch, seq, RWKV_WIDTH), bg.reshape(batch, seq, RWKV_WIDTH),
      lw["gnw"], lw["gnb"], gnm)


def _attn_kernel(qt_ref, k_ref, vt_ref, g_ref, o_ref, m_scr, acc_scr, s0_scr, s1_scr, c0_scr, c1_scr):
    tq = s0_scr.shape[1]
    n_q = qt_ref.shape[1]
    bufs = ((s0_scr, c0_scr), (s1_scr, c1_scr))
    ones = jnp.ones((BF16_ROWS, tq), BF16)
    tiles = [(i, j) for i in range(n_q) for j in range(i + 1)]

    def scores(i, j, slot):
        s_ref, c_ref = bufs[slot]
        k = k_ref[0, j * tq:(j + 1) * tq, :]
        s = jnp.dot(k, qt_ref[0, i], preferred_element_type=F32)
        if i == j:
            kc = lax.broadcasted_iota(jnp.int32, s.shape, 0) // CHUNK
            qc = lax.broadcasted_iota(jnp.int32, s.shape, 1) // CHUNK
            s = jnp.where(kc <= qc, s, -1e30)
        s_ref[...] = s
        c_ref[...] = jnp.max(s, axis=0, keepdims=True)

    def accumulate(i, j, slot):
        s_ref, c_ref = bufs[slot]
        m_new = c_ref[...] if j == 0 else jnp.maximum(m_scr[...], c_ref[...])
        p = jnp.exp2((s_ref[...] - m_new).astype(BF16))
        vt1 = jnp.concatenate([vt_ref[0, j], ones], axis=0)
        acc = jnp.dot(vt1, p, preferred_element_type=F32)
        if j > 0:
            acc = jnp.exp2(m_scr[...] - m_new) * acc_scr[...] + acc
        if j == i:
            rows = slice(i * tq, (i + 1) * tq)
            o = acc[:V_HEAD_DIM] / acc[V_HEAD_DIM:V_HEAD_DIM + 1]
            o_ref[0, rows, :] = (o.T * g_ref[0, rows, :]).astype(BF16)
        else:
            m_scr[...] = m_new
            acc_scr[...] = acc

    scores(*tiles[0], 0)
    for t, (i, j) in enumerate(tiles):
        if t + 1 < len(tiles):
            scores(*tiles[t + 1], (t + 1) % 2)
        accumulate(i, j, t % 2)


def _attn_call(qt, k, vt, gate, batch, seq):
    qt = qt.reshape(batch, seq // TQ, MLA_HEADS * QK_PAD, TQ)
    k = k.reshape(batch, seq, MLA_HEADS * QK_PAD)
    vt = vt.reshape(batch, seq // TQ, MLA_WIDTH, TQ)
    gate = gate.reshape(batch, seq, MLA_WIDTH)
    return pl.pallas_call(
        _attn_kernel,
        grid=(batch, MLA_HEADS),
        in_specs=[pl.BlockSpec((1, seq // TQ, QK_PAD, TQ), lambda b, h: (b, 0, h, 0)),
                  pl.BlockSpec((1, seq, QK_PAD), lambda b, h: (b, 0, h)),
                  pl.BlockSpec((1, seq // TQ, V_HEAD_DIM, TQ), lambda b, h: (b, 0, h, 0)),
                  pl.BlockSpec((1, seq, V_HEAD_DIM), lambda b, h: (b, 0, h))],
        out_specs=pl.BlockSpec((1, seq, V_HEAD_DIM), lambda b, h: (b, 0, h)),
        out_shape=jax.ShapeDtypeStruct((batch, seq, MLA_WIDTH), BF16),
        scratch_shapes=[pltpu.VMEM((1, TQ), F32),
                        pltpu.VMEM((V_HEAD_DIM + BF16_ROWS, TQ), F32),
                        pltpu.VMEM((TQ, TQ), F32), pltpu.VMEM((TQ, TQ), F32),
                        pltpu.VMEM((1, TQ), F32), pltpu.VMEM((1, TQ), F32)],
        compiler_params=pltpu.CompilerParams(
            dimension_semantics=("arbitrary", "arbitrary"), vmem_limit_bytes=VMEM_LIMIT),
        name="attn",
    )(qt, k, vt, gate)


def _out_kernel(h_ref, yc_ref, yr_ref, ym_ref, wo_ref, p_ref, wple_ref, png_ref, wpg_ref, fg_ref,
                o_ref, *, final):
    ycat = jnp.concatenate([yc_ref[...], yr_ref[...], ym_ref[...]], axis=-1)
    h = h_ref[...] + jnp.dot(ycat, wo_ref[...], preferred_element_type=F32)
    gate = jax.nn.sigmoid(jnp.dot(_rms(h, png_ref[...]).astype(BF16), wpg_ref[...],
                                  preferred_element_type=F32))
    h = h + jnp.dot(p_ref[...].astype(BF16), wple_ref[...], preferred_element_type=F32) * gate
    if final:
        h = _rms(h, fg_ref[...])
    o_ref[...] = h


def _out_call(h, yc, yr, ym, p, layer, lw, final_g, final):
    t = h.shape[0]
    row = lambda w: pl.BlockSpec((TM, w), lambda i: (i, 0))
    return pl.pallas_call(
        functools.partial(_out_kernel, final=final),
        grid=(t // TM,),
        in_specs=[row(D_MODEL), row(CONV_WIDTH), row(RWKV_WIDTH), row(MLA_WIDTH),
                  _layer_spec(lw["wo"], layer),
                  pl.BlockSpec((None, TM, PLE_DIM), lambda i: (layer, i, 0)),
                  _layer_spec(lw["wple"], layer), _layer_spec(lw["png"], layer),
                  _layer_spec(lw["wpg"], layer), _const_spec(final_g)],
        out_specs=row(D_MODEL),
        out_shape=jax.ShapeDtypeStruct((t, D_MODEL), F32),
        compiler_params=pltpu.CompilerParams(dimension_semantics=("arbitrary",),
                                             vmem_limit_bytes=VMEM_LIMIT),
        name="out",
    )(h, yc, yr, ym, lw["wo"], p, lw["wple"], lw["png"], lw["wpg"], final_g)


def _prep(norm_mix_g, w_in, conv_w, rwkv_mu, rwkv_w0, rwkv_w2, rwkv_a0, rwkv_a2, rwkv_kk, rwkv_ka, rwkv_rk,
          rwkv_gn_w, rwkv_gn_b, mla_q_norm_g, mla_w_qb, mla_kv_norm_g, mla_w_kvb, w_out, ple_w, ple_norm_g,
          ple_gate_w):
    depth = w_in.shape[0]
    o_kr = C_MG
    o_mg = o_kr + QK_ROPE_DIM
    zeros = jnp.zeros((depth, D_MODEL, QK_ROPE_DIM), F32)
    win_a = w_in[:, :, :o_kr].astype(BF16)
    win_b = jnp.concatenate([w_in[:, :, o_mg:], w_in[:, :, o_kr:o_mg], zeros], axis=2).astype(BF16)

    zl = jnp.zeros((depth, LORA, RWKV_WIDTH), F32)
    wl = jnp.concatenate([jnp.concatenate([rwkv_w2, zl], axis=2),
                          jnp.concatenate([zl, rwkv_a2], axis=2)], axis=1).astype(BF16)

    wqb = mla_w_qb.reshape(depth, Q_LORA_RANK, MLA_HEADS, QK_NOPE_DIM + QK_ROPE_DIM)
    zq = jnp.zeros((depth, Q_LORA_RANK, MLA_HEADS, QK_ROPE_DIM), F32)
    wq = jnp.concatenate([wqb, zq], axis=-1).reshape(depth, Q_LORA_RANK, MLA_HEADS * QK_PAD).astype(BF16)
    wkvb = mla_w_kvb.reshape(depth, KV_LORA_RANK, MLA_HEADS, QK_NOPE_DIM + V_HEAD_DIM)
    wkv = jnp.concatenate([wkvb[..., :QK_NOPE_DIM].reshape(depth, KV_LORA_RANK, -1),
                           wkvb[..., QK_NOPE_DIM:].reshape(depth, KV_LORA_RANK, -1)], axis=2).astype(BF16)
    row = lambda x: x.reshape(depth, 1, -1)
    hd = lambda x: jnp.pad(x.reshape(depth, RWKV_HEADS, RWKV_HEAD_DIM),
                           ((0, 0), (0, 0), (LANE - RWKV_HEAD_DIM, 0)))
    return dict(
        ng=row(norm_mix_g), win_a=win_a, win_b=win_b, cw=conv_w,
        mu_rkv=row(rwkv_mu[:, :3 * RWKV_WIDTH]), mu_wa=row(rwkv_mu[:, 3 * RWKV_WIDTH:]),
        w0=row(rwkv_w0), a0=row(rwkv_a0), wl=wl, kk=row(rwkv_kk), ka=row(rwkv_ka),
        qg=row(mla_q_norm_g), wq=wq, kvg=row(mla_kv_norm_g), wkv=wkv,
        rk=row(rwkv_rk), gnw=hd(rwkv_gn_w), gnb=hd(rwkv_gn_b),
        wo=w_out.astype(BF16), wple=ple_w.astype(BF16), png=row(ple_norm_g), wpg=ple_gate_w.astype(BF16))


def kernel(x, p, positions, norm_mix_g, w_in, conv_w, rwkv_mu, rwkv_w0, rwkv_w2, rwkv_a0, rwkv_a2,
           rwkv_kk, rwkv_ka, rwkv_rk, rwkv_gn_w, rwkv_gn_b, mla_q_norm_g, mla_w_qb, mla_kv_norm_g,
           mla_w_kvb, w_out, ple_w, ple_norm_g, ple_gate_w, final_norm_g):
    batch, seq, _ = x.shape
    depth = w_in.shape[0]
    t = batch * seq

    half = QK_ROPE_DIM // 2
    inv_freq = 1.0 / (ROPE_THETA ** (jnp.arange(0, QK_ROPE_DIM, 2, dtype=F32) / QK_ROPE_DIM))
    inv_tile = jnp.concatenate([inv_freq, inv_freq, jnp.zeros((LANE - QK_ROPE_DIM,), F32)])
    lane = np.arange(LANE)
    ang = positions.astype(F32).reshape(t, 1) * inv_tile
    ck = jnp.where(lane < QK_ROPE_DIM, jnp.cos(ang), 0.0)
    sk = jnp.sin(ang) * jnp.asarray(np.where(lane < half, -1.0, 1.0), F32)

    h = x.reshape(t, D_MODEL)
    p = p.reshape(depth, t, PLE_DIM)
    final_g = final_norm_g.reshape(1, D_MODEL)
    lw = _prep(norm_mix_g, w_in, conv_w, rwkv_mu, rwkv_w0, rwkv_w2, rwkv_a0, rwkv_a2, rwkv_kk, rwkv_ka,
               rwkv_rk, rwkv_gn_w, rwkv_gn_b, mla_q_norm_g, mla_w_qb, mla_kv_norm_g, mla_w_kvb, w_out,
               ple_w, ple_norm_g, ple_gate_w)
    for i in range(depth):
        (yconv, *hm, wc, gate_r, bg, q, kq, vq, gate_m) = _proj_call(h, lw, i, ck, sk, batch, seq)
        y_rwkv = _rwkv_call(hm, wc, gate_r, bg, lw, i, batch, seq).reshape(t, RWKV_WIDTH)
        y_mla = _attn_call(q, kq, vq, gate_m, batch, seq).reshape(t, MLA_WIDTH)
        h = _out_call(h, yconv, y_rwkv, y_mla, p, i, lw, final_g, final=(i == depth - 1))
    return h.reshape(batch, seq, D_MODEL)
```

```python
import functools
import math

import jax
import jax.numpy as jnp
import numpy as np
from jax import lax
from jax.experimental import pallas as pl
from jax.experimental.pallas import tpu as pltpu

F32 = jnp.float32
BF16 = jnp.bfloat16

D_MODEL = 1024
CHUNK = 64
PLE_DIM = 256
NORM_EPS = 1e-6
CONV_WIDTH = 256
RWKV_HEADS = 4
RWKV_HEAD_DIM = 64
RWKV_WIDTH = RWKV_HEADS * RWKV_HEAD_DIM
LORA = 64
DECAY_SCALE = math.exp(-0.5)
GN_EPS = 64e-5
MLA_HEADS = 4
QK_NOPE_DIM = 128
QK_ROPE_DIM = 64
V_HEAD_DIM = 128
Q_LORA_RANK = 384
KV_LORA_RANK = 256
MLA_WIDTH = MLA_HEADS * V_HEAD_DIM
ROPE_THETA = 10000.0
D_MIX = CONV_WIDTH + RWKV_WIDTH + MLA_WIDTH

C_CONV = 0
C_RKV = 1024
C_WA = 1792
C_RG = 1920
C_QA = 2176
C_KVA = 2560
C_MG = 2816
C_KR = 3328
D_IN_P = 3456
QK_PAD = 256

LANE = 128
BF16_ROWS = 16
HALO = BF16_ROWS
TM = 512
OUT_ROW_BLOCKS = 2
TQ = 512
RWKV_CHUNKS_PER_STEP = 4
VMEM_LIMIT = 56 * 1024 * 1024


def _dot(a, b):
    return jnp.dot(a.astype(BF16), b.astype(BF16), preferred_element_type=F32)


def _dot_nt(a, b):
    return lax.dot_general(a.astype(BF16), b.astype(BF16), (((1,), (1,)), ((), ())),
                           preferred_element_type=F32)


def _dot_tn(a, b):
    return lax.dot_general(a.astype(BF16), b.astype(BF16), (((0,), (0,)), ((), ())),
                           preferred_element_type=F32)


def _split2(x):
    hi = x.astype(BF16)
    return hi, (x - hi.astype(F32)).astype(BF16)


def _dot_mask(x, m01, left=False):
    hi, lo = _split2(x)
    if left:
        return jnp.dot(m01, hi, preferred_element_type=F32) + jnp.dot(m01, lo, preferred_element_type=F32)
    return jnp.dot(hi, m01, preferred_element_type=F32) + jnp.dot(lo, m01, preferred_element_type=F32)


def _chunk_cumsum(x):
    pos = lax.broadcasted_iota(jnp.int32, x.shape, 0) % CHUNK
    d = 1
    while d < CHUNK:
        x = x + jnp.where(pos >= d, pltpu.roll(x, d, axis=0), 0.0)
        d *= 2
    return x


def _swap_rope_halves(x):
    half = QK_ROPE_DIM // 2
    return pltpu.roll(x, half, axis=1) + pltpu.roll(x, LANE - half, axis=1)


def _rms(x, g):
    return x * lax.rsqrt(jnp.mean(x * x, axis=-1, keepdims=True) + NORM_EPS) * g


def _silu(x):
    return x * jax.nn.sigmoid(x)


def _proj_kernel(h_ref, ng_ref, wa_ref, wb_ref, cwt_ref, mu_rkv_ref, mu_wa_ref, w0_ref, a0_ref,
                 wl_ref, kk_ref, ka_ref, rk_ref, seg_ref, csum_ref, qg_ref, wq_ref, kvg_ref, wkv_ref,
                 ck_ref, sk_ref,
                 yconv_ref, rt_ref, at_ref, bt_ref, kt_ref, bh_ref, kh_ref, v_ref, wc_ref, gr_ref, bg_ref,
                 q_ref, kq_ref, vq_ref, gm_ref, z_scr, *, tiles_per_seq):
    hm_refs = (rt_ref, at_ref, bt_ref, kt_ref, bh_ref, kh_ref, v_ref)
    tm = h_ref.shape[0]
    tile = pl.ds(HALO, tm)

    @pl.when(pl.program_id(0) == 0)
    def _():
        z_scr[0:HALO, :] = jnp.zeros((HALO, D_IN_P), F32)

    u = _rms(h_ref[...], ng_ref[...]).astype(BF16)
    z_scr[tile, :C_MG] = jnp.dot(u, wa_ref[...], preferred_element_type=F32)
    z_scr[tile, C_MG:] = jnp.dot(u, wb_ref[...], preferred_element_type=F32)

    first = pl.program_id(0) % tiles_per_seq == 0
    row = lax.broadcasted_iota(jnp.int32, (tm, 1), 0)

    def cols(c, w, back=0):
        x = z_scr[pl.ds(HALO - back, tm), c:c + w]
        return jnp.where(row >= jnp.where(first, back, 0), x, 0.0) if back else x

    cwt = cwt_ref[...]
    conv = sum(cols(C_CONV + 256, 256, back) * cols(C_CONV + 512, 256, back) * cwt[2 - back:3 - back, :]
               for back in range(3))
    yconv_ref[...] = (cols(C_CONV, 256) * conv * _silu(cols(C_CONV + 768, 256))).astype(BF16)

    cur = cols(C_RKV, 768)
    rkv = cur + (cols(C_RKV, 768, 1) - cur) * mu_rkv_ref[...]
    cur = cols(C_WA, 128)
    wa = cur + (cols(C_WA, 128, 1) - cur) * mu_wa_ref[...]
    r = rkv[:, 0:256]
    k = rkv[:, 256:512]
    v = rkv[:, 512:768]
    lane = lax.broadcasted_iota(jnp.int32, wa.shape, 1)
    lora_in = jnp.where(lane < LORA, jnp.tanh(wa), wa)
    lora = jnp.dot(lora_in.astype(BF16), wl_ref[...], preferred_element_type=F32)
    wlog = -DECAY_SCALE * jax.nn.sigmoid(w0_ref[...] + lora[:, :RWKV_WIDTH])
    a = jax.nn.sigmoid(a0_ref[...] + lora[:, RWKV_WIDTH:])
    kk = k * kk_ref[...]
    kk = kk * lax.rsqrt(_dot_mask(kk * kk, seg_ref[...]) + 1e-12)
    kmod = k * (1.0 + (a - 1.0) * ka_ref[...])
    zb = kk * a
    cw = _chunk_cumsum(wlog)
    total = _dot_mask(wlog, csum_ref[...], left=True)
    nchunk = tm // CHUNK
    rest = jnp.broadcast_to(total[:, None, :], (nchunk, CHUNK, RWKV_WIDTH)).reshape(tm, RWKV_WIDTH) - cw
    inv = jnp.exp(-cw)
    to_end = jnp.exp(rest)
    wc = jnp.exp(total)
    ops = (r * jnp.exp(cw), -kk * jnp.exp(cw - wlog), zb * inv, kmod * inv, zb * to_end, kmod * to_end, v)
    zpad = jnp.zeros((tm, LANE - RWKV_HEAD_DIM), BF16)
    for hd in range(RWKV_HEADS):
        sl = slice(hd * RWKV_HEAD_DIM, (hd + 1) * RWKV_HEAD_DIM)
        for ref, val in zip(hm_refs, ops):
            piece = val[:, sl].astype(BF16)
            if ref is at_ref:
                piece = jnp.concatenate([piece, zpad], axis=1)
            elif ref is v_ref:
                piece = jnp.concatenate([zpad, piece], axis=1)
            ref[0, hd] = piece
        wc_ref[0, hd] = wc[:, sl]
    gate_r = _silu(cols(C_RG, 256))
    bonus = _dot_mask(r * kmod * rk_ref[...], seg_ref[...]) * v
    gr_ref[...] = gate_r
    bg_ref[...] = bonus * gate_r

    ck = ck_ref[...]
    sk = sk_ref[...]
    qn = _rms(cols(C_QA, Q_LORA_RANK), qg_ref[...]).astype(BF16)
    qm = jnp.dot(qn, wq_ref[...], preferred_element_type=F32)
    scale = math.log2(math.e) / math.sqrt(QK_NOPE_DIM + QK_ROPE_DIM)
    kvn = _rms(cols(C_KVA, KV_LORA_RANK), kvg_ref[...]).astype(BF16)
    kv = jnp.dot(kvn, wkv_ref[...], preferred_element_type=F32)
    kx = cols(C_KR, LANE)
    kr = kx * ck + _swap_rope_halves(kx) * sk
    for hd in range(MLA_HEADS):
        o = hd * QK_PAD
        qx = qm[:, o + LANE:o + QK_PAD]
        qr = qx * ck + _swap_rope_halves(qx) * sk
        q_ref[0, o:o + LANE, :] = (qm[:, o:o + LANE] * scale).T.astype(BF16)
        q_ref[0, o + LANE:o + QK_PAD, :] = (qr * scale).T.astype(BF16)
        kq_ref[:, o:o + LANE] = kv[:, hd * LANE:(hd + 1) * LANE].astype(BF16)
        kq_ref[:, o + LANE:o + QK_PAD] = kr.astype(BF16)
    vq_ref[0] = kv[:, MLA_HEADS * QK_NOPE_DIM:].T.astype(BF16)
    gm_ref[...] = _silu(cols(C_MG, MLA_WIDTH))

    shifted = slice(C_CONV + 256, C_RG)
    z_scr[0:HALO, shifted] = z_scr[pl.ds(tm, HALO), shifted]


def _const_spec(arr):
    return pl.BlockSpec(arr.shape, lambda *_: (0,) * arr.ndim)


def _layer_spec(arr, layer):
    zeros = (0,) * (arr.ndim - 1)
    return pl.BlockSpec((None,) + arr.shape[1:], lambda *_: (layer,) + zeros)


def _proj_call(h, lw, layer, ck, sk, batch, seq):
    t = h.shape[0]
    tiles_per_seq = seq // TM
    row = lambda w: pl.BlockSpec((TM, w), lambda i: (i, 0))
    chunk_id = np.arange(TM) // CHUNK
    csum = jnp.asarray(np.arange(TM // CHUNK)[:, None] == chunk_id[None, :], BF16)
    head = np.arange(RWKV_WIDTH) // RWKV_HEAD_DIM
    seg = jnp.asarray(head[:, None] == head[None, :], BF16)
    per_layer = lambda *names: [(lw[n], _layer_spec(lw[n], layer)) for n in names]
    shared = lambda *arrs: [(a, _const_spec(a)) for a in arrs]
    consts = (per_layer("ng", "win_a", "win_b", "cw", "mu_rkv", "mu_wa", "w0", "a0", "wl", "kk", "ka", "rk")
              + shared(seg, csum) + per_layer("qg", "wq", "kvg", "wkv"))
    hm_widths = [RWKV_HEAD_DIM, LANE] + [RWKV_HEAD_DIM] * 4 + [LANE]
    hm = [pl.BlockSpec((1, RWKV_HEADS, TM, w), lambda i: (i // tiles_per_seq, 0, i % tiles_per_seq, 0))
          for w in hm_widths]
    hm_shapes = [jax.ShapeDtypeStruct((batch, RWKV_HEADS, seq, w), BF16) for w in hm_widths]
    wc_spec = pl.BlockSpec((1, RWKV_HEADS, TM // CHUNK, RWKV_HEAD_DIM),
                           lambda i: (i // tiles_per_seq, 0, i % tiles_per_seq, 0))
    out_shape = [jax.ShapeDtypeStruct((t, CONV_WIDTH), BF16)] + hm_shapes + [
        jax.ShapeDtypeStruct((batch, RWKV_HEADS, seq // CHUNK, RWKV_HEAD_DIM), F32),
        jax.ShapeDtypeStruct((t, RWKV_WIDTH), F32),
        jax.ShapeDtypeStruct((t, RWKV_WIDTH), F32),
        jax.ShapeDtypeStruct((t // TM, MLA_HEADS * QK_PAD, TM), BF16),
        jax.ShapeDtypeStruct((t, MLA_HEADS * QK_PAD), BF16),
        jax.ShapeDtypeStruct((t // TM, MLA_WIDTH, TM), BF16),
        jax.ShapeDtypeStruct((t, MLA_WIDTH), F32),
    ]
    vt_spec = pl.BlockSpec((1, MLA_WIDTH, TM), lambda i: (i, 0, 0))
    qt_spec = pl.BlockSpec((1, MLA_HEADS * QK_PAD, TM), lambda i: (i, 0, 0))
    out_specs = [row(CONV_WIDTH)] + hm + [wc_spec, row(RWKV_WIDTH), row(RWKV_WIDTH),
                                                  qt_spec, row(MLA_HEADS * QK_PAD),
                                                  vt_spec, row(MLA_WIDTH)]
    return pl.pallas_call(
        functools.partial(_proj_kernel, tiles_per_seq=tiles_per_seq),
        grid=(t // TM,),
        in_specs=[row(D_MODEL)] + [s for _, s in consts] + [row(LANE), row(LANE)],
        out_specs=out_specs,
        out_shape=out_shape,
        scratch_shapes=[pltpu.VMEM((HALO + TM, D_IN_P), F32)],
        compiler_params=pltpu.CompilerParams(dimension_semantics=("arbitrary",),
                                             vmem_limit_bytes=VMEM_LIMIT),
        name="proj",
    )(h, *[a for a, _ in consts], ck, sk)


def _rwkv_chunk_local(r_t, a_t, b_t, k_t, bh, kh, v, wc, tri_incl, tri_strict, eye):
    n = range(len(r_t))
    half = RWKV_HEAD_DIM
    lhs = [jnp.concatenate([a_t[i][:, :half], r_t[i]], axis=0) for i in n]
    bk = [jnp.concatenate([b_t[i], k_t[i]], axis=0) for i in n]
    abk = [_dot_nt(lhs[i], bk[i]) for i in n]
    top = [jnp.where(tri_strict, abk[i][:CHUNK], 0.0).astype(BF16) for i in n]
    bot = [jnp.where(tri_incl, abk[i][CHUNK:], 0.0).astype(BF16) for i in n]

    zeros = jnp.zeros((CHUNK, LANE), BF16)
    x = [a_t[i].astype(F32) + _dot(top[i], jnp.concatenate([zeros, v[i]], axis=0)) for i in n]
    p = [top[i][:, :half] for i in n]
    for it in range(6):
        x = [x[i] + _dot(p[i], x[i]) for i in n]
        if it < 5:
            p = [_dot(p[i], p[i]).astype(BF16) for i in n]
    xv = [jnp.concatenate([x[i].astype(BF16), v[i]], axis=0) for i in n]

    yloc = [_dot(bot[i], xv[i]) for i in n]
    qp = [r_t[i].astype(F32) + yloc[i][:, :half] for i in n]
    nc = [_dot_tn(jnp.concatenate([bh[i], kh[i]], axis=0), xv[i]) for i in n]
    mc = [jnp.where(eye, wc[i], 0.0) + nc[i][:, :half] for i in n]
    return qp, yloc, mc, nc


def _rwkv_kernel(rt_ref, at_ref, bt_ref, kt_ref, bh_ref, kh_ref, v_ref, wc_ref, gate_ref, bg_ref,
                 gnw_ref, gnb_ref, gnm_ref, y_ref, state_ref):
    step = pl.program_id(0)
    batch = rt_ref.shape[0]

    @pl.when(step == 0)
    def _():
        state_ref[...] = jnp.zeros_like(state_ref)

    ri = lax.broadcasted_iota(jnp.int32, (CHUNK, LANE), 0)
    li = lax.broadcasted_iota(jnp.int32, (CHUNK, LANE), 1)
    ci = li % CHUNK
    tri_incl = ri >= ci
    tri_strict = ri > ci
    eye = (ri == li)[:, :RWKV_HEAD_DIM]
    value_lanes = li >= RWKV_HEAD_DIM
    seqs = [(b, hd) for b in range(batch) for hd in range(RWKV_HEADS)]
    items = [(c, b, hd) for c in range(RWKV_CHUNKS_PER_STEP) for b, hd in seqs]
    n = range(len(items))
    load = lambda ref: [ref[b, hd, pl.ds(c * CHUNK, CHUNK), :] for c, b, hd in items]
    wc = [wc_ref[b, hd, pl.ds(step * RWKV_CHUNKS_PER_STEP + c, 1), :] for c, b, hd in items]
    qp, yloc, mc, nc = _rwkv_chunk_local(load(rt_ref), load(at_ref), load(bt_ref), load(kt_ref),
                                         load(bh_ref), load(kh_ref), load(v_ref), wc,
                                         tri_incl, tri_strict, eye)
    ns = range(len(seqs))
    states = [state_ref[s] for s in ns]
    y = []
    for c in range(RWKV_CHUNKS_PER_STEP):
        o = c * len(seqs)
        y += [_dot(qp[o + s], states[s]) + yloc[o + s] for s in ns]
        states = [jnp.where(value_lanes, _dot(mc[o + s], states[s]) + nc[o + s], 0.0) for s in ns]
    for s in ns:
        state_ref[s] = states[s]
    gnm = gnm_ref[...]
    head = [hd for _, _, hd in items]
    yall = jnp.concatenate(y, axis=0)
    yc = yall - _dot(yall, gnm)
    scale = lax.rsqrt(_dot(yc * yc, gnm) + GN_EPS)
    part = lambda a, i: a[i * CHUNK:(i + 1) * CHUNK]
    gn = [part(yc, i) * part(scale, i) * gnw_ref[head[i]:head[i] + 1, :]
          + gnb_ref[head[i]:head[i] + 1, :] for i in n]
    for c in range(RWKV_CHUNKS_PER_STEP):
        rows = pl.ds(c * CHUNK, CHUNK)
        for b in range(batch):
            o = (c * batch + b) * RWKV_HEADS
            pair = [pltpu.roll(gn[o + hd], RWKV_HEAD_DIM, axis=1) + gn[o + hd + 1] for hd in (0, 2)]
            y_ref[b, rows, :] = (jnp.concatenate(pair, axis=1) * gate_ref[b, rows, :]
                                 + bg_ref[b, rows, :]).astype(BF16)


def _rwkv_call(hm, wc, gate, bg, lw, layer, batch, seq):
    tc = RWKV_CHUNKS_PER_STEP * CHUNK
    hm_spec = lambda a: pl.BlockSpec((batch, RWKV_HEADS, tc, a.shape[-1]), lambda c: (0, 0, c, 0))
    nat = pl.BlockSpec((batch, tc, RWKV_WIDTH), lambda c: (0, c, 0))
    value = np.arange(LANE) >= RWKV_HEAD_DIM
    gnm = jnp.asarray((value[:, None] & value[None, :]) / RWKV_HEAD_DIM, BF16)
    return pl.pallas_call(
        _rwkv_kernel,
        grid=(seq // tc,),
        in_specs=[hm_spec(a) for a in hm] + [_const_spec(wc), nat, nat,
                                             _layer_spec(lw["gnw"], layer), _layer_spec(lw["gnb"], layer),
                                             _const_spec(gnm)],
        out_specs=nat,
        out_shape=jax.ShapeDtypeStruct((batch, seq, RWKV_WIDTH), BF16),
        scratch_shapes=[pltpu.VMEM((batch * RWKV_HEADS, RWKV_HEAD_DIM, LANE), F32)],
        compiler_params=pltpu.CompilerParams(dimension_semantics=("arbitrary",),
                                             vmem_limit_bytes=VMEM_LIMIT),
        name="rwkv",
    )(*hm, wc, gate.reshape(batch, seq, RWKV_WIDTH), bg.reshape(batch, seq, RWKV_WIDTH),
      lw["gnw"], lw["gnb"], gnm)


def _attn_kernel(qt_ref, k_ref, vt_ref, g_ref, o_ref, m_scr, acc_scr, s0_scr, s1_scr, c0_scr, c1_scr):
    tq = s0_scr.shape[1]
    n_q = qt_ref.shape[1]
    bufs = ((s0_scr, c0_scr), (s1_scr, c1_scr))
    ones = jnp.ones((BF16_ROWS, tq), BF16)
    tiles = [(i, j) for i in range(n_q) for j in range(i + 1)]

    def scores(i, j, slot):
        s_ref, c_ref = bufs[slot]
        k = k_ref[0, j * tq:(j + 1) * tq, :]
        s = jnp.dot(k, qt_ref[0, i], preferred_element_type=F32)
        if i == j:
            kc = lax.broadcasted_iota(jnp.int32, s.shape, 0) // CHUNK
            qc = lax.broadcasted_iota(jnp.int32, s.shape, 1) // CHUNK
            s = jnp.where(kc <= qc, s, -1e30)
        s_ref[...] = s
        c_ref[...] = jnp.max(s, axis=0, keepdims=True)

    def accumulate(i, j, slot):
        s_ref, c_ref = bufs[slot]
        m_new = c_ref[...] if j == 0 else jnp.maximum(m_scr[...], c_ref[...])
        p = jnp.exp2((s_ref[...] - m_new).astype(BF16))
        vt1 = jnp.concatenate([vt_ref[0, j], ones], axis=0)
        acc = jnp.dot(vt1, p, preferred_element_type=F32)
        if j > 0:
            acc = jnp.exp2(m_scr[...] - m_new) * acc_scr[...] + acc
        if j == i:
            rows = slice(i * tq, (i + 1) * tq)
            o = acc[:V_HEAD_DIM] / acc[V_HEAD_DIM:V_HEAD_DIM + 1]
            o_ref[0, rows, :] = (o.T * g_ref[0, rows, :]).astype(BF16)
        else:
            m_scr[...] = m_new
            acc_scr[...] = acc

    scores(*tiles[0], 0)
    for t, (i, j) in enumerate(tiles):
        if t + 1 < len(tiles):
            scores(*tiles[t + 1], (t + 1) % 2)
        accumulate(i, j, t % 2)


def _attn_call(qt, k, vt, gate, batch, seq):
    qt = qt.reshape(batch, seq // TQ, MLA_HEADS * QK_PAD, TQ)
    k = k.reshape(batch, seq, MLA_HEADS * QK_PAD)
    vt = vt.reshape(batch, seq // TQ, MLA_WIDTH, TQ)
    gate = gate.reshape(batch, seq, MLA_WIDTH)
    return pl.pallas_call(
        _attn_kernel,
        grid=(batch, MLA_HEADS),
        in_specs=[pl.BlockSpec((1, seq // TQ, QK_PAD, TQ), lambda b, h: (b, 0, h, 0)),
                  pl.BlockSpec((1, seq, QK_PAD), lambda b, h: (b, 0, h)),
                  pl.BlockSpec((1, seq // TQ, V_HEAD_DIM, TQ), lambda b, h: (b, 0, h, 0)),
                  pl.BlockSpec((1, seq, V_HEAD_DIM), lambda b, h: (b, 0, h))],
        out_specs=pl.BlockSpec((1, seq, V_HEAD_DIM), lambda b, h: (b, 0, h)),
        out_shape=jax.ShapeDtypeStruct((batch, seq, MLA_WIDTH), BF16),
        scratch_shapes=[pltpu.VMEM((1, TQ), F32),
                        pltpu.VMEM((V_HEAD_DIM + BF16_ROWS, TQ), F32),
                        pltpu.VMEM((TQ, TQ), F32), pltpu.VMEM((TQ, TQ), F32),
                        pltpu.VMEM((1, TQ), F32), pltpu.VMEM((1, TQ), F32)],
        compiler_params=pltpu.CompilerParams(
            dimension_semantics=("arbitrary", "arbitrary"), vmem_limit_bytes=VMEM_LIMIT),
        name="attn",
    )(qt, k, vt, gate)


def _out_kernel(h_ref, yc_ref, yr_ref, ym_ref, wo_ref, p_ref, wple_ref, png_ref, wpg_ref, fg_ref,
                o_ref, *, final):
    tm = h_ref.shape[0]
    blocks = [pl.ds(r, tm // OUT_ROW_BLOCKS) for r in range(0, tm, tm // OUT_ROW_BLOCKS)]
    ycat = [jnp.concatenate([yc_ref[b, :], yr_ref[b, :], ym_ref[b, :]], axis=-1) for b in blocks]
    h = [h_ref[b, :] + jnp.dot(y, wo_ref[...], preferred_element_type=F32) for b, y in zip(blocks, ycat)]
    hn = [_rms(x, png_ref[...]).astype(BF16) for x in h]
    gate = [jax.nn.sigmoid(jnp.dot(x, wpg_ref[...], preferred_element_type=F32)) for x in hn]
    ple = [jnp.dot(p_ref[b, :].astype(BF16), wple_ref[...], preferred_element_type=F32) for b in blocks]
    h = [x + e * g for x, e, g in zip(h, ple, gate)]
    if final:
        h = [_rms(x, fg_ref[...]) for x in h]
    for b, x in zip(blocks, h):
        o_ref[b, :] = x


def _out_call(h, yc, yr, ym, p, layer, lw, final_g, final):
    t = h.shape[0]
    row = lambda w: pl.BlockSpec((TM, w), lambda i: (i, 0))
    return pl.pallas_call(
        functools.partial(_out_kernel, final=final),
        grid=(t // TM,),
        in_specs=[row(D_MODEL), row(CONV_WIDTH), row(RWKV_WIDTH), row(MLA_WIDTH),
                  _layer_spec(lw["wo"], layer),
                  pl.BlockSpec((None, TM, PLE_DIM), lambda i: (layer, i, 0)),
                  _layer_spec(lw["wple"], layer), _layer_spec(lw["png"], layer),
                  _layer_spec(lw["wpg"], layer), _const_spec(final_g)],
        out_specs=row(D_MODEL),
        out_shape=jax.ShapeDtypeStruct((t, D_MODEL), F32),
        compiler_params=pltpu.CompilerParams(dimension_semantics=("arbitrary",),
                                             vmem_limit_bytes=VMEM_LIMIT),
        name="out",
    )(h, yc, yr, ym, lw["wo"], p, lw["wple"], lw["png"], lw["wpg"], final_g)


def _prep(norm_mix_g, w_in, conv_w, rwkv_mu, rwkv_w0, rwkv_w2, rwkv_a0, rwkv_a2, rwkv_kk, rwkv_ka, rwkv_rk,
          rwkv_gn_w, rwkv_gn_b, mla_q_norm_g, mla_w_qb, mla_kv_norm_g, mla_w_kvb, w_out, ple_w, ple_norm_g,
          ple_gate_w):
    depth = w_in.shape[0]
    o_kr = C_MG
    o_mg = o_kr + QK_ROPE_DIM
    zeros = jnp.zeros((depth, D_MODEL, QK_ROPE_DIM), F32)
    win_a = w_in[:, :, :o_kr].astype(BF16)
    win_b = jnp.concatenate([w_in[:, :, o_mg:], w_in[:, :, o_kr:o_mg], zeros], axis=2).astype(BF16)

    zl = jnp.zeros((depth, LORA, RWKV_WIDTH), F32)
    wl = jnp.concatenate([jnp.concatenate([rwkv_w2, zl], axis=2),
                          jnp.concatenate([zl, rwkv_a2], axis=2)], axis=1).astype(BF16)

    wqb = mla_w_qb.reshape(depth, Q_LORA_RANK, MLA_HEADS, QK_NOPE_DIM + QK_ROPE_DIM)
    zq = jnp.zeros((depth, Q_LORA_RANK, MLA_HEADS, QK_ROPE_DIM), F32)
    wq = jnp.concatenate([wqb, zq], axis=-1).reshape(depth, Q_LORA_RANK, MLA_HEADS * QK_PAD).astype(BF16)
    wkvb = mla_w_kvb.reshape(depth, KV_LORA_RANK, MLA_HEADS, QK_NOPE_DIM + V_HEAD_DIM)
    wkv = jnp.concatenate([wkvb[..., :QK_NOPE_DIM].reshape(depth, KV_LORA_RANK, -1),
                           wkvb[..., QK_NOPE_DIM:].reshape(depth, KV_LORA_RANK, -1)], axis=2).astype(BF16)
    row = lambda x: x.reshape(depth, 1, -1)
    hd = lambda x: jnp.pad(x.reshape(depth, RWKV_HEADS, RWKV_HEAD_DIM),
                           ((0, 0), (0, 0), (LANE - RWKV_HEAD_DIM, 0)))
    return dict(
        ng=row(norm_mix_g), win_a=win_a, win_b=win_b, cw=conv_w,
        mu_rkv=row(rwkv_mu[:, :3 * RWKV_WIDTH]), mu_wa=row(rwkv_mu[:, 3 * RWKV_WIDTH:]),
        w0=row(rwkv_w0), a0=row(rwkv_a0), wl=wl, kk=row(rwkv_kk), ka=row(rwkv_ka),
        qg=row(mla_q_norm_g), wq=wq, kvg=row(mla_kv_norm_g), wkv=wkv,
        rk=row(rwkv_rk), gnw=hd(rwkv_gn_w), gnb=hd(rwkv_gn_b),
        wo=w_out.astype(BF16), wple=ple_w.astype(BF16), png=row(ple_norm_g), wpg=ple_gate_w.astype(BF16))


def kernel(x, p, positions, norm_mix_g, w_in, conv_w, rwkv_mu, rwkv_w0, rwkv_w2, rwkv_a0, rwkv_a2,
           rwkv_kk, rwkv_ka, rwkv_rk, rwkv_gn_w, rwkv_gn_b, mla_q_norm_g, mla_w_qb, mla_kv_norm_g,
           mla_w_kvb, w_out, ple_w, ple_norm_g, ple_gate_w, final_norm_g):
    batch, seq, _ = x.shape
    depth = w_in.shape[0]
    t = batch * seq

    half = QK_ROPE_DIM // 2
    inv_freq = 1.0 / (ROPE_THETA ** (jnp.arange(0, QK_ROPE_DIM, 2, dtype=F32) / QK_ROPE_DIM))
    inv_tile = jnp.concatenate([inv_freq, inv_freq, jnp.zeros((LANE - QK_ROPE_DIM,), F32)])
    lane = np.arange(LANE)
    ang = positions.astype(F32).reshape(t, 1) * inv_tile
    ck = jnp.where(lane < QK_ROPE_DIM, jnp.cos(ang), 0.0)
    sk = jnp.sin(ang) * jnp.asarray(np.where(lane < half, -1.0, 1.0), F32)

    h = x.reshape(t, D_MODEL)
    p = p.reshape(depth, t, PLE_DIM)
    final_g = final_norm_g.reshape(1, D_MODEL)
    lw = _prep(norm_mix_g, w_in, conv_w, rwkv_mu, rwkv_w0, rwkv_w2, rwkv_a0, rwkv_a2, rwkv_kk, rwkv_ka,
               rwkv_rk, rwkv_gn_w, rwkv_gn_b, mla_q_norm_g, mla_w_qb, mla_kv_norm_g, mla_w_kvb, w_out,
               ple_w, ple_norm_g, ple_gate_w)
    for i in range(depth):
        (yconv, *hm, wc, gate_r, bg, q, kq, vq, gate_m) = _proj_call(h, lw, i, ck, sk, batch, seq)
        y_rwkv = _rwkv_call(hm, wc, gate_r, bg, lw, i, batch, seq).reshape(t, RWKV_WIDTH)
        y_mla = _attn_call(q, kq, vq, gate_m, batch, seq).reshape(t, MLA_WIDTH)
        h = _out_call(h, yconv, y_rwkv, y_mla, p, i, lw, final_g, final=(i == depth - 1))
    return h.reshape(batch, seq, D_MODEL)
```

```python
import functools
import math

import jax
import jax.numpy as jnp
import numpy as np
from jax import lax
from jax.experimental import pallas as pl
from jax.experimental.pallas import tpu as pltpu

F32 = jnp.float32
BF16 = jnp.bfloat16

D_MODEL = 1024
CHUNK = 64
PLE_DIM = 256
NORM_EPS = 1e-6
CONV_WIDTH = 256
RWKV_HEADS = 4
RWKV_HEAD_DIM = 64
RWKV_WIDTH = RWKV_HEADS * RWKV_HEAD_DIM
LORA = 64
DECAY_SCALE = math.exp(-0.5)
GN_EPS = 64e-5
MLA_HEADS = 4
QK_NOPE_DIM = 128
QK_ROPE_DIM = 64
V_HEAD_DIM = 128
Q_LORA_RANK = 384
KV_LORA_RANK = 256
MLA_WIDTH = MLA_HEADS * V_HEAD_DIM
ROPE_THETA = 10000.0
D_MIX = CONV_WIDTH + RWKV_WIDTH + MLA_WIDTH

C_CONV = 0
C_RKV = 1024
C_WA = 1792
C_RG = 1920
C_QA = 2176
C_KVA = 2560
C_MG = 2816
C_KR = 3328
D_IN_P = 3456
QK_PAD = 256

LANE = 128
BF16_ROWS = 16
HALO = BF16_ROWS
TM = 512
OUT_ROW_BLOCKS = 2
TQ = 512
RWKV_CHUNKS_PER_STEP = 4
VMEM_LIMIT = 56 * 1024 * 1024


def _dot(a, b):
    return jnp.dot(a.astype(BF16), b.astype(BF16), preferred_element_type=F32)


def _dot_nt(a, b):
    return lax.dot_general(a.astype(BF16), b.astype(BF16), (((1,), (1,)), ((), ())),
                           preferred_element_type=F32)


def _dot_tn(a, b):
    return lax.dot_general(a.astype(BF16), b.astype(BF16), (((0,), (0,)), ((), ())),
                           preferred_element_type=F32)


def _split2(x):
    hi = x.astype(BF16)
    return hi, (x - hi.astype(F32)).astype(BF16)


def _dot_mask(x, m01, left=False):
    hi, lo = _split2(x)
    if left:
        return jnp.dot(m01, hi, preferred_element_type=F32) + jnp.dot(m01, lo, preferred_element_type=F32)
    return jnp.dot(hi, m01, preferred_element_type=F32) + jnp.dot(lo, m01, preferred_element_type=F32)


def _chunk_cumsum(x):
    pos = lax.broadcasted_iota(jnp.int32, x.shape, 0) % CHUNK
    d = 1
    while d < CHUNK:
        x = x + jnp.where(pos >= d, pltpu.roll(x, d, axis=0), 0.0)
        d *= 2
    return x


def _swap_rope_halves(x):
    half = QK_ROPE_DIM // 2
    return pltpu.roll(x, half, axis=1) + pltpu.roll(x, LANE - half, axis=1)


def _rms(x, g):
    return x * lax.rsqrt(jnp.mean(x * x, axis=-1, keepdims=True) + NORM_EPS) * g


def _silu(x):
    return x * jax.nn.sigmoid(x)


def _proj_kernel(h_ref, ng_ref, wa_ref, wb_ref, cwt_ref, mu_rkv_ref, mu_wa_ref, w0_ref, a0_ref,
                 wl_ref, kk_ref, ka_ref, rk_ref, seg_ref, csum_ref, qg_ref, wq_ref, kvg_ref, wkv_ref,
                 ck_ref, sk_ref,
                 yconv_ref, rt_ref, at_ref, bt_ref, kt_ref, bh_ref, kh_ref, v_ref, wc_ref, gr_ref, bg_ref,
                 q_ref, kq_ref, vq_ref, gm_ref, z_scr, wa_scr, *, tiles_per_seq):
    hm_refs = (rt_ref, at_ref, bt_ref, kt_ref, bh_ref, kh_ref, v_ref)
    tm = h_ref.shape[0]
    tile = pl.ds(HALO, tm)

    @pl.when(pl.program_id(0) == 0)
    def _():
        z_scr[0:HALO, :] = jnp.zeros((HALO, D_IN_P), F32)
        wa_scr[...] = wa_ref[...].astype(BF16)

    u = _rms(h_ref[...], ng_ref[...]).astype(BF16)
    z_scr[tile, :C_MG] = jnp.dot(u, wa_scr[...], preferred_element_type=F32)
    z_scr[tile, C_MG:] = jnp.dot(u, wb_ref[...], preferred_element_type=F32)

    first = pl.program_id(0) % tiles_per_seq == 0
    row = lax.broadcasted_iota(jnp.int32, (tm, 1), 0)

    def cols(c, w, back=0):
        x = z_scr[pl.ds(HALO - back, tm), c:c + w]
        return jnp.where(row >= jnp.where(first, back, 0), x, 0.0) if back else x

    cwt = cwt_ref[...]
    conv = sum(cols(C_CONV + 256, 256, back) * cols(C_CONV + 512, 256, back) * cwt[2 - back:3 - back, :]
               for back in range(3))
    yconv_ref[...] = (cols(C_CONV, 256) * conv * _silu(cols(C_CONV + 768, 256))).astype(BF16)

    cur = cols(C_RKV, 768)
    rkv = cur + (cols(C_RKV, 768, 1) - cur) * mu_rkv_ref[...]
    cur = cols(C_WA, 128)
    wa = cur + (cols(C_WA, 128, 1) - cur) * mu_wa_ref[...]
    r = rkv[:, 0:256]
    k = rkv[:, 256:512]
    v = rkv[:, 512:768]
    lane = lax.broadcasted_iota(jnp.int32, wa.shape, 1)
    lora_in = jnp.where(lane < LORA, jnp.tanh(wa), wa)
    lora = jnp.dot(lora_in.astype(BF16), wl_ref[...], preferred_element_type=F32)
    wlog = -DECAY_SCALE * jax.nn.sigmoid(w0_ref[...] + lora[:, :RWKV_WIDTH])
    a = jax.nn.sigmoid(a0_ref[...] + lora[:, RWKV_WIDTH:])
    kk = k * kk_ref[...]
    kk = kk * lax.rsqrt(_dot_mask(kk * kk, seg_ref[...]) + 1e-12)
    kmod = k * (1.0 + (a - 1.0) * ka_ref[...])
    zb = kk * a
    cw = _chunk_cumsum(wlog)
    total = _dot_mask(wlog, csum_ref[...], left=True)
    nchunk = tm // CHUNK
    rest = jnp.broadcast_to(total[:, None, :], (nchunk, CHUNK, RWKV_WIDTH)).reshape(tm, RWKV_WIDTH) - cw
    inv = jnp.exp(-cw)
    to_end = jnp.exp(rest)
    wc = jnp.exp(total)
    ops = (r * jnp.exp(cw), -kk * jnp.exp(cw - wlog), zb * inv, kmod * inv, zb * to_end, kmod * to_end, v)
    zpad = jnp.zeros((tm, LANE - RWKV_HEAD_DIM), BF16)
    for hd in range(RWKV_HEADS):
        sl = slice(hd * RWKV_HEAD_DIM, (hd + 1) * RWKV_HEAD_DIM)
        for ref, val in zip(hm_refs, ops):
            piece = val[:, sl].astype(BF16)
            if ref is at_ref:
                piece = jnp.concatenate([piece, zpad], axis=1)
            elif ref is v_ref:
                piece = jnp.concatenate([zpad, piece], axis=1)
            ref[0, hd] = piece
        wc_ref[0, hd] = wc[:, sl]
    gate_r = _silu(cols(C_RG, 256))
    bonus = _dot_mask(r * kmod * rk_ref[...], seg_ref[...]) * v
    gr_ref[...] = gate_r
    bg_ref[...] = bonus * gate_r

    ck = ck_ref[...]
    sk = sk_ref[...]
    qn = _rms(cols(C_QA, Q_LORA_RANK), qg_ref[...]).astype(BF16)
    qm = jnp.dot(qn, wq_ref[...], preferred_element_type=F32)
    scale = math.log2(math.e) / math.sqrt(QK_NOPE_DIM + QK_ROPE_DIM)
    kvn = _rms(cols(C_KVA, KV_LORA_RANK), kvg_ref[...]).astype(BF16)
    kv = jnp.dot(kvn, wkv_ref[...], preferred_element_type=F32)
    kx = cols(C_KR, LANE)
    kr = kx * ck + _swap_rope_halves(kx) * sk
    for hd in range(MLA_HEADS):
        o = hd * QK_PAD
        qx = qm[:, o + LANE:o + QK_PAD]
        qr = qx * ck + _swap_rope_halves(qx) * sk
        q_ref[0, o:o + LANE, :] = (qm[:, o:o + LANE] * scale).T.astype(BF16)
        q_ref[0, o + LANE:o + QK_PAD, :] = (qr * scale).T.astype(BF16)
        kq_ref[:, o:o + LANE] = kv[:, hd * LANE:(hd + 1) * LANE].astype(BF16)
        kq_ref[:, o + LANE:o + QK_PAD] = kr.astype(BF16)
    vq_ref[0] = kv[:, MLA_HEADS * QK_NOPE_DIM:].T.astype(BF16)
    gm_ref[...] = _silu(cols(C_MG, MLA_WIDTH))

    shifted = slice(C_CONV + 256, C_RG)
    z_scr[0:HALO, shifted] = z_scr[pl.ds(tm, HALO), shifted]


def _const_spec(arr):
    return pl.BlockSpec(arr.shape, lambda *_: (0,) * arr.ndim)


def _layer_spec(arr, layer):
    zeros = (0,) * (arr.ndim - 1)
    return pl.BlockSpec((None,) + arr.shape[1:], lambda *_: (layer,) + zeros)


def _proj_call(h, lw, layer, ck, sk, batch, seq):
    t = h.shape[0]
    tiles_per_seq = seq // TM
    row = lambda w: pl.BlockSpec((TM, w), lambda i: (i, 0))
    chunk_id = np.arange(TM) // CHUNK
    csum = jnp.asarray(np.arange(TM // CHUNK)[:, None] == chunk_id[None, :], BF16)
    head = np.arange(RWKV_WIDTH) // RWKV_HEAD_DIM
    seg = jnp.asarray(head[:, None] == head[None, :], BF16)
    per_layer = lambda *names: [(lw[n], _layer_spec(lw[n], layer)) for n in names]
    shared = lambda *arrs: [(a, _const_spec(a)) for a in arrs]
    win_a = (lw["w_in"], pl.BlockSpec((None, D_MODEL, C_MG), lambda *_: (layer, 0, 0),
                                      pipeline_mode=pl.Buffered(1)))
    consts = (per_layer("ng") + [win_a]
              + per_layer("win_b", "cw", "mu_rkv", "mu_wa", "w0", "a0", "wl", "kk", "ka", "rk")
              + shared(seg, csum) + per_layer("qg", "wq", "kvg", "wkv"))
    hm_widths = [RWKV_HEAD_DIM, LANE] + [RWKV_HEAD_DIM] * 4 + [LANE]
    hm = [pl.BlockSpec((1, RWKV_HEADS, TM, w), lambda i: (i // tiles_per_seq, 0, i % tiles_per_seq, 0))
          for w in hm_widths]
    hm_shapes = [jax.ShapeDtypeStruct((batch, RWKV_HEADS, seq, w), BF16) for w in hm_widths]
    wc_spec = pl.BlockSpec((1, RWKV_HEADS, TM // CHUNK, RWKV_HEAD_DIM),
                           lambda i: (i // tiles_per_seq, 0, i % tiles_per_seq, 0))
    out_shape = [jax.ShapeDtypeStruct((t, CONV_WIDTH), BF16)] + hm_shapes + [
        jax.ShapeDtypeStruct((batch, RWKV_HEADS, seq // CHUNK, RWKV_HEAD_DIM), F32),
        jax.ShapeDtypeStruct((t, RWKV_WIDTH), F32),
        jax.ShapeDtypeStruct((t, RWKV_WIDTH), F32),
        jax.ShapeDtypeStruct((t // TM, MLA_HEADS * QK_PAD, TM), BF16),
        jax.ShapeDtypeStruct((t, MLA_HEADS * QK_PAD), BF16),
        jax.ShapeDtypeStruct((t // TM, MLA_WIDTH, TM), BF16),
        jax.ShapeDtypeStruct((t, MLA_WIDTH), F32),
    ]
    vt_spec = pl.BlockSpec((1, MLA_WIDTH, TM), lambda i: (i, 0, 0))
    qt_spec = pl.BlockSpec((1, MLA_HEADS * QK_PAD, TM), lambda i: (i, 0, 0))
    out_specs = [row(CONV_WIDTH)] + hm + [wc_spec, row(RWKV_WIDTH), row(RWKV_WIDTH),
                                                  qt_spec, row(MLA_HEADS * QK_PAD),
                                                  vt_spec, row(MLA_WIDTH)]
    return pl.pallas_call(
        functools.partial(_proj_kernel, tiles_per_seq=tiles_per_seq),
        grid=(t // TM,),
        in_specs=[row(D_MODEL)] + [s for _, s in consts] + [row(LANE), row(LANE)],
        out_specs=out_specs,
        out_shape=out_shape,
        scratch_shapes=[pltpu.VMEM((HALO + TM, D_IN_P), F32), pltpu.VMEM((D_MODEL, C_MG), BF16)],
        compiler_params=pltpu.CompilerParams(dimension_semantics=("arbitrary",),
                                             vmem_limit_bytes=VMEM_LIMIT),
        name="proj",
    )(h, *[a for a, _ in consts], ck, sk)


def _rwkv_chunk_local(r_t, a_t, b_t, k_t, bh, kh, v, wc, tri_incl, tri_strict, eye):
    n = range(len(r_t))
    half = RWKV_HEAD_DIM
    lhs = [jnp.concatenate([a_t[i][:, :half], r_t[i]], axis=0) for i in n]
    bk = [jnp.concatenate([b_t[i], k_t[i]], axis=0) for i in n]
    abk = [_dot_nt(lhs[i], bk[i]) for i in n]
    top = [jnp.where(tri_strict, abk[i][:CHUNK], 0.0).astype(BF16) for i in n]
    bot = [jnp.where(tri_incl, abk[i][CHUNK:], 0.0).astype(BF16) for i in n]

    zeros = jnp.zeros((CHUNK, LANE), BF16)
    x = [a_t[i].astype(F32) + _dot(top[i], jnp.concatenate([zeros, v[i]], axis=0)) for i in n]
    p = [top[i][:, :half] for i in n]
    for it in range(6):
        x = [x[i] + _dot(p[i], x[i]) for i in n]
        if it < 5:
            p = [_dot(p[i], p[i]).astype(BF16) for i in n]
    xv = [jnp.concatenate([x[i].astype(BF16), v[i]], axis=0) for i in n]

    yloc = [_dot(bot[i], xv[i]) for i in n]
    qp = [r_t[i].astype(F32) + yloc[i][:, :half] for i in n]
    nc = [_dot_tn(jnp.concatenate([bh[i], kh[i]], axis=0), xv[i]) for i in n]
    mc = [jnp.where(eye, wc[i], 0.0) + nc[i][:, :half] for i in n]
    return qp, yloc, mc, nc


def _rwkv_kernel(rt_ref, at_ref, bt_ref, kt_ref, bh_ref, kh_ref, v_ref, wc_ref, gate_ref, bg_ref,
                 gnw_ref, gnb_ref, gnm_ref, y_ref, state_ref):
    step = pl.program_id(0)
    batch = rt_ref.shape[0]

    @pl.when(step == 0)
    def _():
        state_ref[...] = jnp.zeros_like(state_ref)

    ri = lax.broadcasted_iota(jnp.int32, (CHUNK, LANE), 0)
    li = lax.broadcasted_iota(jnp.int32, (CHUNK, LANE), 1)
    ci = li % CHUNK
    tri_incl = ri >= ci
    tri_strict = ri > ci
    eye = (ri == li)[:, :RWKV_HEAD_DIM]
    value_lanes = li >= RWKV_HEAD_DIM
    seqs = [(b, hd) for b in range(batch) for hd in range(RWKV_HEADS)]
    items = [(c, b, hd) for c in range(RWKV_CHUNKS_PER_STEP) for b, hd in seqs]
    n = range(len(items))
    load = lambda ref: [ref[b, hd, pl.ds(c * CHUNK, CHUNK), :] for c, b, hd in items]
    wc = [wc_ref[b, hd, pl.ds(step * RWKV_CHUNKS_PER_STEP + c, 1), :] for c, b, hd in items]
    qp, yloc, mc, nc = _rwkv_chunk_local(load(rt_ref), load(at_ref), load(bt_ref), load(kt_ref),
                                         load(bh_ref), load(kh_ref), load(v_ref), wc,
                                         tri_incl, tri_strict, eye)
    ns = range(len(seqs))
    states = [state_ref[s] for s in ns]
    y = []
    for c in range(RWKV_CHUNKS_PER_STEP):
        o = c * len(seqs)
        y += [_dot(qp[o + s], states[s]) + yloc[o + s] for s in ns]
        states = [jnp.where(value_lanes, _dot(mc[o + s], states[s]) + nc[o + s], 0.0) for s in ns]
    for s in ns:
        state_ref[s] = states[s]
    gnm = gnm_ref[...]
    head = [hd for _, _, hd in items]
    yall = jnp.concatenate(y, axis=0)
    yc = yall - _dot(yall, gnm)
    scale = lax.rsqrt(_dot(yc * yc, gnm) + GN_EPS)
    part = lambda a, i: a[i * CHUNK:(i + 1) * CHUNK]
    gn = [part(yc, i) * part(scale, i) * gnw_ref[head[i]:head[i] + 1, :]
          + gnb_ref[head[i]:head[i] + 1, :] for i in n]
    for c in range(RWKV_CHUNKS_PER_STEP):
        rows = pl.ds(c * CHUNK, CHUNK)
        for b in range(batch):
            o = (c * batch + b) * RWKV_HEADS
            pair = [pltpu.roll(gn[o + hd], RWKV_HEAD_DIM, axis=1) + gn[o + hd + 1] for hd in (0, 2)]
            y_ref[b, rows, :] = (jnp.concatenate(pair, axis=1) * gate_ref[b, rows, :]
                                 + bg_ref[b, rows, :]).astype(BF16)


def _rwkv_call(hm, wc, gate, bg, lw, layer, batch, seq):
    tc = RWKV_CHUNKS_PER_STEP * CHUNK
    hm_spec = lambda a: pl.BlockSpec((batch, RWKV_HEADS, tc, a.shape[-1]), lambda c: (0, 0, c, 0))
    nat = pl.BlockSpec((batch, tc, RWKV_WIDTH), lambda c: (0, c, 0))
    value = np.arange(LANE) >= RWKV_HEAD_DIM
    gnm = jnp.asarray((value[:, None] & value[None, :]) / RWKV_HEAD_DIM, BF16)
    return pl.pallas_call(
        _rwkv_kernel,
        grid=(seq // tc,),
        in_specs=[hm_spec(a) for a in hm] + [_const_spec(wc), nat, nat,
                                             _layer_spec(lw["gnw"], layer), _layer_spec(lw["gnb"], layer),
                                             _const_spec(gnm)],
        out_specs=nat,
        out_shape=jax.ShapeDtypeStruct((batch, seq, RWKV_WIDTH), BF16),
        scratch_shapes=[pltpu.VMEM((batch * RWKV_HEADS, RWKV_HEAD_DIM, LANE), F32)],
        compiler_params=pltpu.CompilerParams(dimension_semantics=("arbitrary",),
                                             vmem_limit_bytes=VMEM_LIMIT),
        name="rwkv",
    )(*hm, wc, gate.reshape(batch, seq, RWKV_WIDTH), bg.reshape(batch, seq, RWKV_WIDTH),
      lw["gnw"], lw["gnb"], gnm)


def _attn_kernel(qt_ref, k_ref, vt_ref, g_ref, o_ref, m_scr, acc_scr, s0_scr, s1_scr, c0_scr, c1_scr):
    tq = s0_scr.shape[1]
    n_q = qt_ref.shape[1]
    bufs = ((s0_scr, c0_scr), (s1_scr, c1_scr))
    ones = jnp.ones((BF16_ROWS, tq), BF16)
    tiles = [(i, j) for i in range(n_q) for j in range(i + 1)]

    def scores(i, j, slot):
        s_ref, c_ref = bufs[slot]
        k = k_ref[0, j * tq:(j + 1) * tq, :]
        s = jnp.dot(k, qt_ref[0, i], preferred_element_type=F32)
        if i == j:
            kc = lax.broadcasted_iota(jnp.int32, s.shape, 0) // CHUNK
            qc = lax.broadcasted_iota(jnp.int32, s.shape, 1) // CHUNK
            s = jnp.where(kc <= qc, s, -1e30)
        s_ref[...] = s
        c_ref[...] = jnp.max(s, axis=0, keepdims=True)

    def accumulate(i, j, slot):
        s_ref, c_ref = bufs[slot]
        m_new = c_ref[...] if j == 0 else jnp.maximum(m_scr[...], c_ref[...])
        p = jnp.exp2((s_ref[...] - m_new).astype(BF16))
        vt1 = jnp.concatenate([vt_ref[0, j], ones], axis=0)
        acc = jnp.dot(vt1, p, preferred_element_type=F32)
        if j > 0:
            acc = jnp.exp2(m_scr[...] - m_new) * acc_scr[...] + acc
        if j == i:
            rows = slice(i * tq, (i + 1) * tq)
            o = acc[:V_HEAD_DIM] / acc[V_HEAD_DIM:V_HEAD_DIM + 1]
            o_ref[0, rows, :] = (o.T * g_ref[0, rows, :]).astype(BF16)
        else:
            m_scr[...] = m_new
            acc_scr[...] = acc

    scores(*tiles[0], 0)
    for t, (i, j) in enumerate(tiles):
        if t + 1 < len(tiles):
            scores(*tiles[t + 1], (t + 1) % 2)
        accumulate(i, j, t % 2)


def _attn_call(qt, k, vt, gate, batch, seq):
    qt = qt.reshape(batch, seq // TQ, MLA_HEADS * QK_PAD, TQ)
    k = k.reshape(batch, seq, MLA_HEADS * QK_PAD)
    vt = vt.reshape(batch, seq // TQ, MLA_WIDTH, TQ)
    gate = gate.reshape(batch, seq, MLA_WIDTH)
    return pl.pallas_call(
        _attn_kernel,
        grid=(batch, MLA_HEADS),
        in_specs=[pl.BlockSpec((1, seq // TQ, QK_PAD, TQ), lambda b, h: (b, 0, h, 0)),
                  pl.BlockSpec((1, seq, QK_PAD), lambda b, h: (b, 0, h)),
                  pl.BlockSpec((1, seq // TQ, V_HEAD_DIM, TQ), lambda b, h: (b, 0, h, 0)),
                  pl.BlockSpec((1, seq, V_HEAD_DIM), lambda b, h: (b, 0, h))],
        out_specs=pl.BlockSpec((1, seq, V_HEAD_DIM), lambda b, h: (b, 0, h)),
        out_shape=jax.ShapeDtypeStruct((batch, seq, MLA_WIDTH), BF16),
        scratch_shapes=[pltpu.VMEM((1, TQ), F32),
                        pltpu.VMEM((V_HEAD_DIM + BF16_ROWS, TQ), F32),
                        pltpu.VMEM((TQ, TQ), F32), pltpu.VMEM((TQ, TQ), F32),
                        pltpu.VMEM((1, TQ), F32), pltpu.VMEM((1, TQ), F32)],
        compiler_params=pltpu.CompilerParams(
            dimension_semantics=("arbitrary", "arbitrary"), vmem_limit_bytes=VMEM_LIMIT),
        name="attn",
    )(qt, k, vt, gate)


def _out_kernel(h_ref, yc_ref, yr_ref, ym_ref, wo_ref, p_ref, wple_ref, png_ref, wpg_ref, fg_ref,
                o_ref, *, final):
    tm = h_ref.shape[0]
    blocks = [pl.ds(r, tm // OUT_ROW_BLOCKS) for r in range(0, tm, tm // OUT_ROW_BLOCKS)]
    ycat = [jnp.concatenate([yc_ref[b, :], yr_ref[b, :], ym_ref[b, :]], axis=-1) for b in blocks]
    h = [h_ref[b, :] + jnp.dot(y, wo_ref[...], preferred_element_type=F32) for b, y in zip(blocks, ycat)]
    hn = [_rms(x, png_ref[...]).astype(BF16) for x in h]
    gate = [jax.nn.sigmoid(jnp.dot(x, wpg_ref[...], preferred_element_type=F32)) for x in hn]
    ple = [jnp.dot(p_ref[b, :].astype(BF16), wple_ref[...], preferred_element_type=F32) for b in blocks]
    h = [x + e * g for x, e, g in zip(h, ple, gate)]
    if final:
        h = [_rms(x, fg_ref[...]) for x in h]
    for b, x in zip(blocks, h):
        o_ref[b, :] = x


def _out_call(h, yc, yr, ym, p, layer, lw, final_g, final):
    t = h.shape[0]
    row = lambda w: pl.BlockSpec((TM, w), lambda i: (i, 0))
    return pl.pallas_call(
        functools.partial(_out_kernel, final=final),
        grid=(t // TM,),
        in_specs=[row(D_MODEL), row(CONV_WIDTH), row(RWKV_WIDTH), row(MLA_WIDTH),
                  _layer_spec(lw["wo"], layer),
                  pl.BlockSpec((None, TM, PLE_DIM), lambda i: (layer, i, 0)),
                  _layer_spec(lw["wple"], layer), _layer_spec(lw["png"], layer),
                  _layer_spec(lw["wpg"], layer), _const_spec(final_g)],
        out_specs=row(D_MODEL),
        out_shape=jax.ShapeDtypeStruct((t, D_MODEL), F32),
        compiler_params=pltpu.CompilerParams(dimension_semantics=("arbitrary",),
                                             vmem_limit_bytes=VMEM_LIMIT),
        name="out",
    )(h, yc, yr, ym, lw["wo"], p, lw["wple"], lw["png"], lw["wpg"], final_g)


def _prep(norm_mix_g, w_in, conv_w, rwkv_mu, rwkv_w0, rwkv_w2, rwkv_a0, rwkv_a2, rwkv_kk, rwkv_ka, rwkv_rk,
          rwkv_gn_w, rwkv_gn_b, mla_q_norm_g, mla_w_qb, mla_kv_norm_g, mla_w_kvb, w_out, ple_w, ple_norm_g,
          ple_gate_w):
    depth = w_in.shape[0]
    o_kr = C_MG
    o_mg = o_kr + QK_ROPE_DIM
    zeros = jnp.zeros((depth, D_MODEL, QK_ROPE_DIM), F32)
    win_b = jnp.concatenate([w_in[:, :, o_mg:], w_in[:, :, o_kr:o_mg], zeros], axis=2).astype(BF16)

    zl = jnp.zeros((depth, LORA, RWKV_WIDTH), F32)
    wl = jnp.concatenate([jnp.concatenate([rwkv_w2, zl], axis=2),
                          jnp.concatenate([zl, rwkv_a2], axis=2)], axis=1).astype(BF16)

    wqb = mla_w_qb.reshape(depth, Q_LORA_RANK, MLA_HEADS, QK_NOPE_DIM + QK_ROPE_DIM)
    zq = jnp.zeros((depth, Q_LORA_RANK, MLA_HEADS, QK_ROPE_DIM), F32)
    wq = jnp.concatenate([wqb, zq], axis=-1).reshape(depth, Q_LORA_RANK, MLA_HEADS * QK_PAD).astype(BF16)
    wkvb = mla_w_kvb.reshape(depth, KV_LORA_RANK, MLA_HEADS, QK_NOPE_DIM + V_HEAD_DIM)
    wkv = jnp.concatenate([wkvb[..., :QK_NOPE_DIM].reshape(depth, KV_LORA_RANK, -1),
                           wkvb[..., QK_NOPE_DIM:].reshape(depth, KV_LORA_RANK, -1)], axis=2).astype(BF16)
    row = lambda x: x.reshape(depth, 1, -1)
    hd = lambda x: jnp.pad(x.reshape(depth, RWKV_HEADS, RWKV_HEAD_DIM),
                           ((0, 0), (0, 0), (LANE - RWKV_HEAD_DIM, 0)))
    return dict(
        ng=row(norm_mix_g), w_in=w_in, win_b=win_b, cw=conv_w,
        mu_rkv=row(rwkv_mu[:, :3 * RWKV_WIDTH]), mu_wa=row(rwkv_mu[:, 3 * RWKV_WIDTH:]),
        w0=row(rwkv_w0), a0=row(rwkv_a0), wl=wl, kk=row(rwkv_kk), ka=row(rwkv_ka),
        qg=row(mla_q_norm_g), wq=wq, kvg=row(mla_kv_norm_g), wkv=wkv,
        rk=row(rwkv_rk), gnw=hd(rwkv_gn_w), gnb=hd(rwkv_gn_b),
        wo=w_out.astype(BF16), wple=ple_w.astype(BF16), png=row(ple_norm_g), wpg=ple_gate_w.astype(BF16))


def kernel(x, p, positions, norm_mix_g, w_in, conv_w, rwkv_mu, rwkv_w0, rwkv_w2, rwkv_a0, rwkv_a2,
           rwkv_kk, rwkv_ka, rwkv_rk, rwkv_gn_w, rwkv_gn_b, mla_q_norm_g, mla_w_qb, mla_kv_norm_g,
           mla_w_kvb, w_out, ple_w, ple_norm_g, ple_gate_w, final_norm_g):
    batch, seq, _ = x.shape
    depth = w_in.shape[0]
    t = batch * seq

    half = QK_ROPE_DIM // 2
    inv_freq = 1.0 / (ROPE_THETA ** (jnp.arange(0, QK_ROPE_DIM, 2, dtype=F32) / QK_ROPE_DIM))
    inv_tile = jnp.concatenate([inv_freq, inv_freq, jnp.zeros((LANE - QK_ROPE_DIM,), F32)])
    lane = np.arange(LANE)
    ang = positions.astype(F32).reshape(t, 1) * inv_tile
    ck = jnp.where(lane < QK_ROPE_DIM, jnp.cos(ang), 0.0)
    sk = jnp.sin(ang) * jnp.asarray(np.where(lane < half, -1.0, 1.0), F32)

    h = x.reshape(t, D_MODEL)
    p = p.reshape(depth, t, PLE_DIM)
    final_g = final_norm_g.reshape(1, D_MODEL)
    lw = _prep(norm_mix_g, w_in, conv_w, rwkv_mu, rwkv_w0, rwkv_w2, rwkv_a0, rwkv_a2, rwkv_kk, rwkv_ka,
               rwkv_rk, rwkv_gn_w, rwkv_gn_b, mla_q_norm_g, mla_w_qb, mla_kv_norm_g, mla_w_kvb, w_out,
               ple_w, ple_norm_g, ple_gate_w)
    for i in range(depth):
        (yconv, *hm, wc, gate_r, bg, q, kq, vq, gate_m) = _proj_call(h, lw, i, ck, sk, batch, seq)
        y_rwkv = _rwkv_call(hm, wc, gate_r, bg, lw, i, batch, seq).reshape(t, RWKV_WIDTH)
        y_mla = _attn_call(q, kq, vq, gate_m, batch, seq).reshape(t, MLA_WIDTH)
        h = _out_call(h, yconv, y_rwkv, y_mla, p, i, lw, final_g, final=(i == depth - 1))
    return h.reshape(batch, seq, D_MODEL)
```

```python
import functools
import math

import jax
import jax.numpy as jnp
import numpy as np
from jax import lax
from jax.experimental import pallas as pl
from jax.experimental.pallas import tpu as pltpu

F32 = jnp.float32
BF16 = jnp.bfloat16

D_MODEL = 1024
CHUNK = 64
PLE_DIM = 256
NORM_EPS = 1e-6
CONV_WIDTH = 256
RWKV_HEADS = 4
RWKV_HEAD_DIM = 64
RWKV_WIDTH = RWKV_HEADS * RWKV_HEAD_DIM
LORA = 64
DECAY_SCALE = math.exp(-0.5)
GN_EPS = 64e-5
MLA_HEADS = 4
QK_NOPE_DIM = 128
QK_ROPE_DIM = 64
V_HEAD_DIM = 128
Q_LORA_RANK = 384
KV_LORA_RANK = 256
MLA_WIDTH = MLA_HEADS * V_HEAD_DIM
ROPE_THETA = 10000.0
D_MIX = CONV_WIDTH + RWKV_WIDTH + MLA_WIDTH

C_CONV = 0
C_RKV = 1024
C_WA = 1792
C_RG = 1920
C_QA = 2176
C_KVA = 2560
C_MG = 2816
C_KR = 3328
D_IN_P = 3456
QK_PAD = 256

LANE = 128
BF16_ROWS = 16
HALO = BF16_ROWS
TM = 512
OUT_ROW_BLOCKS = 2
TQ = 512
RWKV_CHUNKS_PER_STEP = 4
VMEM_LIMIT = 56 * 1024 * 1024


def _dot(a, b):
    return jnp.dot(a.astype(BF16), b.astype(BF16), preferred_element_type=F32)


def _dot_nt(a, b):
    return lax.dot_general(a.astype(BF16), b.astype(BF16), (((1,), (1,)), ((), ())),
                           preferred_element_type=F32)


def _dot_tn(a, b):
    return lax.dot_general(a.astype(BF16), b.astype(BF16), (((0,), (0,)), ((), ())),
                           preferred_element_type=F32)


def _split2(x):
    hi = x.astype(BF16)
    return hi, (x - hi.astype(F32)).astype(BF16)


def _dot_mask(x, m01, left=False):
    hi, lo = _split2(x)
    if left:
        return jnp.dot(m01, hi, preferred_element_type=F32) + jnp.dot(m01, lo, preferred_element_type=F32)
    return jnp.dot(hi, m01, preferred_element_type=F32) + jnp.dot(lo, m01, preferred_element_type=F32)


def _chunk_cumsum(x):
    pos = lax.broadcasted_iota(jnp.int32, x.shape, 0) % CHUNK
    d = 1
    while d < CHUNK:
        x = x + jnp.where(pos >= d, pltpu.roll(x, d, axis=0), 0.0)
        d *= 2
    return x


def _swap_rope_halves(x):
    half = QK_ROPE_DIM // 2
    return pltpu.roll(x, half, axis=1) + pltpu.roll(x, LANE - half, axis=1)


def _rms(x, g):
    return x * lax.rsqrt(jnp.mean(x * x, axis=-1, keepdims=True) + NORM_EPS) * g


def _silu(x):
    return x * jax.nn.sigmoid(x)


def _proj_kernel(h_ref, ng_ref, wa_ref, wb_ref, cwt_ref, mu_rkv_ref, mu_wa_ref, w0_ref, a0_ref,
                 wl_ref, kk_ref, ka_ref, rk_ref, seg_ref, csum_ref, qg_ref, wq_ref, kvg_ref, wkv_ref,
                 ck_ref, sk_ref,
                 yconv_ref, rt_ref, at_ref, bt_ref, kt_ref, bh_ref, kh_ref, v_ref, wc_ref, gr_ref, bg_ref,
                 q_ref, kq_ref, vq_ref, gm_ref, z_scr, *, tiles_per_seq):
    hm_refs = (rt_ref, at_ref, bt_ref, kt_ref, bh_ref, kh_ref, v_ref)
    tm = h_ref.shape[0]
    tile = pl.ds(HALO, tm)

    @pl.when(pl.program_id(0) == 0)
    def _():
        z_scr[0:HALO, :] = jnp.zeros((HALO, D_IN_P), F32)

    u = _rms(h_ref[...], ng_ref[...]).astype(BF16)
    z_scr[tile, :C_MG] = jnp.dot(u, wa_ref[:, :C_MG], preferred_element_type=F32)
    z_scr[tile, C_MG:] = jnp.dot(u, wb_ref[...], preferred_element_type=F32)

    first = pl.program_id(0) % tiles_per_seq == 0
    row = lax.broadcasted_iota(jnp.int32, (tm, 1), 0)

    def cols(c, w, back=0):
        x = z_scr[pl.ds(HALO - back, tm), c:c + w]
        return jnp.where(row >= jnp.where(first, back, 0), x, 0.0) if back else x

    cwt = cwt_ref[...]
    conv = sum(cols(C_CONV + 256, 256, back) * cols(C_CONV + 512, 256, back) * cwt[2 - back:3 - back, :]
               for back in range(3))
    yconv_ref[...] = (cols(C_CONV, 256) * conv * _silu(cols(C_CONV + 768, 256))).astype(BF16)

    cur = cols(C_RKV, 768)
    rkv = cur + (cols(C_RKV, 768, 1) - cur) * mu_rkv_ref[...]
    cur = cols(C_WA, 128)
    wa = cur + (cols(C_WA, 128, 1) - cur) * mu_wa_ref[...]
    r = rkv[:, 0:256]
    k = rkv[:, 256:512]
    v = rkv[:, 512:768]
    lane = lax.broadcasted_iota(jnp.int32, wa.shape, 1)
    lora_in = jnp.where(lane < LORA, jnp.tanh(wa), wa)
    lora = jnp.dot(lora_in.astype(BF16), wl_ref[...], preferred_element_type=F32)
    wlog = -DECAY_SCALE * jax.nn.sigmoid(w0_ref[...] + lora[:, :RWKV_WIDTH])
    a = jax.nn.sigmoid(a0_ref[...] + lora[:, RWKV_WIDTH:])
    kk = k * kk_ref[...]
    kk = kk * lax.rsqrt(_dot_mask(kk * kk, seg_ref[...]) + 1e-12)
    kmod = k * (1.0 + (a - 1.0) * ka_ref[...])
    zb = kk * a
    cw = _chunk_cumsum(wlog)
    total = _dot_mask(wlog, csum_ref[...], left=True)
    nchunk = tm // CHUNK
    rest = jnp.broadcast_to(total[:, None, :], (nchunk, CHUNK, RWKV_WIDTH)).reshape(tm, RWKV_WIDTH) - cw
    inv = jnp.exp(-cw)
    to_end = jnp.exp(rest)
    wc = jnp.exp(total)
    ops = (r * jnp.exp(cw), -kk * jnp.exp(cw - wlog), zb * inv, kmod * inv, zb * to_end, kmod * to_end, v)
    zpad = jnp.zeros((tm, LANE - RWKV_HEAD_DIM), BF16)
    for hd in range(RWKV_HEADS):
        sl = slice(hd * RWKV_HEAD_DIM, (hd + 1) * RWKV_HEAD_DIM)
        for ref, val in zip(hm_refs, ops):
            piece = val[:, sl].astype(BF16)
            if ref is at_ref:
                piece = jnp.concatenate([piece, zpad], axis=1)
            elif ref is v_ref:
                piece = jnp.concatenate([zpad, piece], axis=1)
            ref[0, hd] = piece
        wc_ref[0, hd] = wc[:, sl]
    gate_r = _silu(cols(C_RG, 256))
    bonus = _dot_mask(r * kmod * rk_ref[...], seg_ref[...]) * v
    gr_ref[...] = gate_r
    bg_ref[...] = bonus * gate_r

    ck = ck_ref[...]
    sk = sk_ref[...]
    qn = _rms(cols(C_QA, Q_LORA_RANK), qg_ref[...]).astype(BF16)
    qm = jnp.dot(qn, wq_ref[...], preferred_element_type=F32)
    scale = math.log2(math.e) / math.sqrt(QK_NOPE_DIM + QK_ROPE_DIM)
    kvn = _rms(cols(C_KVA, KV_LORA_RANK), kvg_ref[...]).astype(BF16)
    kv = jnp.dot(kvn, wkv_ref[...], preferred_element_type=F32)
    kx = cols(C_KR, LANE)
    kr = kx * ck + _swap_rope_halves(kx) * sk
    for hd in range(MLA_HEADS):
        o = hd * QK_PAD
        qx = qm[:, o + LANE:o + QK_PAD]
        qr = qx * ck + _swap_rope_halves(qx) * sk
        q_ref[0, o:o + LANE, :] = (qm[:, o:o + LANE] * scale).T.astype(BF16)
        q_ref[0, o + LANE:o + QK_PAD, :] = (qr * scale).T.astype(BF16)
        kq_ref[:, o:o + LANE] = kv[:, hd * LANE:(hd + 1) * LANE].astype(BF16)
        kq_ref[:, o + LANE:o + QK_PAD] = kr.astype(BF16)
    vq_ref[0] = kv[:, MLA_HEADS * QK_NOPE_DIM:].T.astype(BF16)
    gm_ref[...] = _silu(cols(C_MG, MLA_WIDTH))

    shifted = slice(C_CONV + 256, C_RG)
    z_scr[0:HALO, shifted] = z_scr[pl.ds(tm, HALO), shifted]


def _const_spec(arr):
    return pl.BlockSpec(arr.shape, lambda *_: (0,) * arr.ndim)


def _layer_spec(arr, layer):
    zeros = (0,) * (arr.ndim - 1)
    return pl.BlockSpec((None,) + arr.shape[1:], lambda *_: (layer,) + zeros)


def _proj_call(h, lw, layer, ck, sk, batch, seq):
    t = h.shape[0]
    tiles_per_seq = seq // TM
    row = lambda w: pl.BlockSpec((TM, w), lambda i: (i, 0))
    chunk_id = np.arange(TM) // CHUNK
    csum = jnp.asarray(np.arange(TM // CHUNK)[:, None] == chunk_id[None, :], BF16)
    head = np.arange(RWKV_WIDTH) // RWKV_HEAD_DIM
    seg = jnp.asarray(head[:, None] == head[None, :], BF16)
    per_layer = lambda *names: [(lw[n], _layer_spec(lw[n], layer)) for n in names]
    shared = lambda *arrs: [(a, _const_spec(a)) for a in arrs]
    consts = (per_layer("ng", "win_a", "win_b", "cw", "mu_rkv", "mu_wa", "w0", "a0", "wl", "kk", "ka", "rk")
              + shared(seg, csum) + per_layer("qg", "wq", "kvg", "wkv"))
    hm_widths = [RWKV_HEAD_DIM, LANE] + [RWKV_HEAD_DIM] * 4 + [LANE]
    hm = [pl.BlockSpec((1, RWKV_HEADS, TM, w), lambda i: (i // tiles_per_seq, 0, i % tiles_per_seq, 0))
          for w in hm_widths]
    hm_shapes = [jax.ShapeDtypeStruct((batch, RWKV_HEADS, seq, w), BF16) for w in hm_widths]
    wc_spec = pl.BlockSpec((1, RWKV_HEADS, TM // CHUNK, RWKV_HEAD_DIM),
                           lambda i: (i // tiles_per_seq, 0, i % tiles_per_seq, 0))
    out_shape = [jax.ShapeDtypeStruct((t, CONV_WIDTH), BF16)] + hm_shapes + [
        jax.ShapeDtypeStruct((batch, RWKV_HEADS, seq // CHUNK, RWKV_HEAD_DIM), F32),
        jax.ShapeDtypeStruct((t, RWKV_WIDTH), F32),
        jax.ShapeDtypeStruct((t, RWKV_WIDTH), F32),
        jax.ShapeDtypeStruct((t // TM, MLA_HEADS * QK_PAD, TM), BF16),
        jax.ShapeDtypeStruct((t, MLA_HEADS * QK_PAD), BF16),
        jax.ShapeDtypeStruct((t // TM, MLA_WIDTH, TM), BF16),
        jax.ShapeDtypeStruct((t, MLA_WIDTH), F32),
    ]
    vt_spec = pl.BlockSpec((1, MLA_WIDTH, TM), lambda i: (i, 0, 0))
    qt_spec = pl.BlockSpec((1, MLA_HEADS * QK_PAD, TM), lambda i: (i, 0, 0))
    out_specs = [row(CONV_WIDTH)] + hm + [wc_spec, row(RWKV_WIDTH), row(RWKV_WIDTH),
                                                  qt_spec, row(MLA_HEADS * QK_PAD),
                                                  vt_spec, row(MLA_WIDTH)]
    return pl.pallas_call(
        functools.partial(_proj_kernel, tiles_per_seq=tiles_per_seq),
        grid=(t // TM,),
        in_specs=[row(D_MODEL)] + [s for _, s in consts] + [row(LANE), row(LANE)],
        out_specs=out_specs,
        out_shape=out_shape,
        scratch_shapes=[pltpu.VMEM((HALO + TM, D_IN_P), F32)],
        compiler_params=pltpu.CompilerParams(dimension_semantics=("arbitrary",),
                                             vmem_limit_bytes=VMEM_LIMIT),
        name="proj",
    )(h, *[a for a, _ in consts], ck, sk)


def _rwkv_chunk_local(r_t, a_t, b_t, k_t, bh, kh, v, wc, tri_incl, tri_strict, eye):
    n = range(len(r_t))
    half = RWKV_HEAD_DIM
    lhs = [jnp.concatenate([a_t[i][:, :half], r_t[i]], axis=0) for i in n]
    bk = [jnp.concatenate([b_t[i], k_t[i]], axis=0) for i in n]
    abk = [_dot_nt(lhs[i], bk[i]) for i in n]
    top = [jnp.where(tri_strict, abk[i][:CHUNK], 0.0).astype(BF16) for i in n]
    bot = [jnp.where(tri_incl, abk[i][CHUNK:], 0.0).astype(BF16) for i in n]

    zeros = jnp.zeros((CHUNK, LANE), BF16)
    x = [a_t[i].astype(F32) + _dot(top[i], jnp.concatenate([zeros, v[i]], axis=0)) for i in n]
    p = [top[i][:, :half] for i in n]
    for it in range(6):
        x = [x[i] + _dot(p[i], x[i]) for i in n]
        if it < 5:
            p = [_dot(p[i], p[i]).astype(BF16) for i in n]
    xv = [jnp.concatenate([x[i].astype(BF16), v[i]], axis=0) for i in n]

    yloc = [_dot(bot[i], xv[i]) for i in n]
    qp = [r_t[i].astype(F32) + yloc[i][:, :half] for i in n]
    nc = [_dot_tn(jnp.concatenate([bh[i], kh[i]], axis=0), xv[i]) for i in n]
    mc = [jnp.where(eye, wc[i], 0.0) + nc[i][:, :half] for i in n]
    return qp, yloc, mc, nc


def _rwkv_kernel(rt_ref, at_ref, bt_ref, kt_ref, bh_ref, kh_ref, v_ref, wc_ref, gate_ref, bg_ref,
                 gnw_ref, gnb_ref, gnm_ref, y_ref, state_ref):
    step = pl.program_id(0)
    batch = rt_ref.shape[0]

    @pl.when(step == 0)
    def _():
        state_ref[...] = jnp.zeros_like(state_ref)

    ri = lax.broadcasted_iota(jnp.int32, (CHUNK, LANE), 0)
    li = lax.broadcasted_iota(jnp.int32, (CHUNK, LANE), 1)
    ci = li % CHUNK
    tri_incl = ri >= ci
    tri_strict = ri > ci
    eye = (ri == li)[:, :RWKV_HEAD_DIM]
    value_lanes = li >= RWKV_HEAD_DIM
    seqs = [(b, hd) for b in range(batch) for hd in range(RWKV_HEADS)]
    items = [(c, b, hd) for c in range(RWKV_CHUNKS_PER_STEP) for b, hd in seqs]
    n = range(len(items))
    load = lambda ref: [ref[b, hd, pl.ds(c * CHUNK, CHUNK), :] for c, b, hd in items]
    wc = [wc_ref[b, hd, pl.ds(step * RWKV_CHUNKS_PER_STEP + c, 1), :] for c, b, hd in items]
    qp, yloc, mc, nc = _rwkv_chunk_local(load(rt_ref), load(at_ref), load(bt_ref), load(kt_ref),
                                         load(bh_ref), load(kh_ref), load(v_ref), wc,
                                         tri_incl, tri_strict, eye)
    ns = range(len(seqs))
    states = [state_ref[s] for s in ns]
    y = []
    for c in range(RWKV_CHUNKS_PER_STEP):
        o = c * len(seqs)
        y += [_dot(qp[o + s], states[s]) + yloc[o + s] for s in ns]
        states = [jnp.where(value_lanes, _dot(mc[o + s], states[s]) + nc[o + s], 0.0) for s in ns]
    for s in ns:
        state_ref[s] = states[s]
    gnm = gnm_ref[...]
    head = [hd for _, _, hd in items]
    yall = jnp.concatenate(y, axis=0)
    yc = yall - _dot(yall, gnm)
    scale = lax.rsqrt(_dot(yc * yc, gnm) + GN_EPS)
    part = lambda a, i: a[i * CHUNK:(i + 1) * CHUNK]
    gn = [part(yc, i) * part(scale, i) * gnw_ref[head[i]:head[i] + 1, :]
          + gnb_ref[head[i]:head[i] + 1, :] for i in n]
    for c in range(RWKV_CHUNKS_PER_STEP):
        rows = pl.ds(c * CHUNK, CHUNK)
        for b in range(batch):
            o = (c * batch + b) * RWKV_HEADS
            pair = [pltpu.roll(gn[o + hd], RWKV_HEAD_DIM, axis=1) + gn[o + hd + 1] for hd in (0, 2)]
            y_ref[b, rows, :] = (jnp.concatenate(pair, axis=1) * gate_ref[b, rows, :]
                                 + bg_ref[b, rows, :]).astype(BF16)


def _rwkv_call(hm, wc, gate, bg, lw, layer, batch, seq):
    tc = RWKV_CHUNKS_PER_STEP * CHUNK
    hm_spec = lambda a: pl.BlockSpec((batch, RWKV_HEADS, tc, a.shape[-1]), lambda c: (0, 0, c, 0))
    nat = pl.BlockSpec((batch, tc, RWKV_WIDTH), lambda c: (0, c, 0))
    value = np.arange(LANE) >= RWKV_HEAD_DIM
    gnm = jnp.asarray((value[:, None] & value[None, :]) / RWKV_HEAD_DIM, BF16)
    return pl.pallas_call(
        _rwkv_kernel,
        grid=(seq // tc,),
        in_specs=[hm_spec(a) for a in hm] + [_const_spec(wc), nat, nat,
                                             _layer_spec(lw["gnw"], layer), _layer_spec(lw["gnb"], layer),
                                             _const_spec(gnm)],
        out_specs=nat,
        out_shape=jax.ShapeDtypeStruct((batch, seq, RWKV_WIDTH), BF16),
        scratch_shapes=[pltpu.VMEM((batch * RWKV_HEADS, RWKV_HEAD_DIM, LANE), F32)],
        compiler_params=pltpu.CompilerParams(dimension_semantics=("arbitrary",),
                                             vmem_limit_bytes=VMEM_LIMIT),
        name="rwkv",
    )(*hm, wc, gate.reshape(batch, seq, RWKV_WIDTH), bg.reshape(batch, seq, RWKV_WIDTH),
      lw["gnw"], lw["gnb"], gnm)


def _attn_kernel(qt_ref, k_ref, vt_ref, g_ref, o_ref, m_scr, acc_scr, s0_scr, s1_scr, c0_scr, c1_scr):
    tq = s0_scr.shape[1]
    n_q = qt_ref.shape[1]
    bufs = ((s0_scr, c0_scr), (s1_scr, c1_scr))
    ones = jnp.ones((BF16_ROWS, tq), BF16)
    tiles = [(i, j) for i in range(n_q) for j in range(i + 1)]

    def scores(i, j, slot):
        s_ref, c_ref = bufs[slot]
        k = k_ref[0, j * tq:(j + 1) * tq, :]
        s = jnp.dot(k, qt_ref[0, i], preferred_element_type=F32)
        if i == j:
            kc = lax.broadcasted_iota(jnp.int32, s.shape, 0) // CHUNK
            qc = lax.broadcasted_iota(jnp.int32, s.shape, 1) // CHUNK
            s = jnp.where(kc <= qc, s, -1e30)
        s_ref[...] = s
        c_ref[...] = jnp.max(s, axis=0, keepdims=True)

    def accumulate(i, j, slot):
        s_ref, c_ref = bufs[slot]
        m_new = c_ref[...] if j == 0 else jnp.maximum(m_scr[...], c_ref[...])
        p = jnp.exp2((s_ref[...] - m_new).astype(BF16))
        vt1 = jnp.concatenate([vt_ref[0, j], ones], axis=0)
        acc = jnp.dot(vt1, p, preferred_element_type=F32)
        if j > 0:
            acc = jnp.exp2(m_scr[...] - m_new) * acc_scr[...] + acc
        if j == i:
            rows = slice(i * tq, (i + 1) * tq)
            o = acc[:V_HEAD_DIM] / acc[V_HEAD_DIM:V_HEAD_DIM + 1]
            o_ref[0, rows, :] = (o.T * g_ref[0, rows, :]).astype(BF16)
        else:
            m_scr[...] = m_new
            acc_scr[...] = acc

    scores(*tiles[0], 0)
    for t, (i, j) in enumerate(tiles):
        if t + 1 < len(tiles):
            scores(*tiles[t + 1], (t + 1) % 2)
        accumulate(i, j, t % 2)


def _attn_call(qt, k, vt, gate, batch, seq):
    qt = qt.reshape(batch, seq // TQ, MLA_HEADS * QK_PAD, TQ)
    k = k.reshape(batch, seq, MLA_HEADS * QK_PAD)
    vt = vt.reshape(batch, seq // TQ, MLA_WIDTH, TQ)
    gate = gate.reshape(batch, seq, MLA_WIDTH)
    return pl.pallas_call(
        _attn_kernel,
        grid=(batch, MLA_HEADS),
        in_specs=[pl.BlockSpec((1, seq // TQ, QK_PAD, TQ), lambda b, h: (b, 0, h, 0)),
                  pl.BlockSpec((1, seq, QK_PAD), lambda b, h: (b, 0, h)),
                  pl.BlockSpec((1, seq // TQ, V_HEAD_DIM, TQ), lambda b, h: (b, 0, h, 0)),
                  pl.BlockSpec((1, seq, V_HEAD_DIM), lambda b, h: (b, 0, h))],
        out_specs=pl.BlockSpec((1, seq, V_HEAD_DIM), lambda b, h: (b, 0, h)),
        out_shape=jax.ShapeDtypeStruct((batch, seq, MLA_WIDTH), BF16),
        scratch_shapes=[pltpu.VMEM((1, TQ), F32),
                        pltpu.VMEM((V_HEAD_DIM + BF16_ROWS, TQ), F32),
                        pltpu.VMEM((TQ, TQ), F32), pltpu.VMEM((TQ, TQ), F32),
                        pltpu.VMEM((1, TQ), F32), pltpu.VMEM((1, TQ), F32)],
        compiler_params=pltpu.CompilerParams(
            dimension_semantics=("arbitrary", "arbitrary"), vmem_limit_bytes=VMEM_LIMIT),
        name="attn",
    )(qt, k, vt, gate)


def _out_kernel(h_ref, yc_ref, yr_ref, ym_ref, wo_ref, p_ref, wple_ref, png_ref, wpg_ref, fg_ref,
                o_ref, *, final):
    tm = h_ref.shape[0]
    blocks = [pl.ds(r, tm // OUT_ROW_BLOCKS) for r in range(0, tm, tm // OUT_ROW_BLOCKS)]
    ycat = [jnp.concatenate([yc_ref[b, :], yr_ref[b, :], ym_ref[b, :]], axis=-1) for b in blocks]
    h = [h_ref[b, :] + jnp.dot(y, wo_ref[...], preferred_element_type=F32) for b, y in zip(blocks, ycat)]
    hn = [_rms(x, png_ref[...]).astype(BF16) for x in h]
    gate = [jax.nn.sigmoid(jnp.dot(x, wpg_ref[...], preferred_element_type=F32)) for x in hn]
    ple = [jnp.dot(p_ref[b, :].astype(BF16), wple_ref[...], preferred_element_type=F32) for b in blocks]
    h = [x + e * g for x, e, g in zip(h, ple, gate)]
    if final:
        h = [_rms(x, fg_ref[...]) for x in h]
    for b, x in zip(blocks, h):
        o_ref[b, :] = x


def _out_call(h, yc, yr, ym, p, layer, lw, final_g, final):
    t = h.shape[0]
    row = lambda w: pl.BlockSpec((TM, w), lambda i: (i, 0))
    return pl.pallas_call(
        functools.partial(_out_kernel, final=final),
        grid=(t // TM,),
        in_specs=[row(D_MODEL), row(CONV_WIDTH), row(RWKV_WIDTH), row(MLA_WIDTH),
                  _layer_spec(lw["wo"], layer),
                  pl.BlockSpec((None, TM, PLE_DIM), lambda i: (layer, i, 0)),
                  _layer_spec(lw["wple"], layer), _layer_spec(lw["png"], layer),
                  _layer_spec(lw["wpg"], layer), _const_spec(final_g)],
        out_specs=row(D_MODEL),
        out_shape=jax.ShapeDtypeStruct((t, D_MODEL), F32),
        compiler_params=pltpu.CompilerParams(dimension_semantics=("arbitrary",),
                                             vmem_limit_bytes=VMEM_LIMIT),
        name="out",
    )(h, yc, yr, ym, lw["wo"], p, lw["wple"], lw["png"], lw["wpg"], final_g)


def _prep(norm_mix_g, w_in, conv_w, rwkv_mu, rwkv_w0, rwkv_w2, rwkv_a0, rwkv_a2, rwkv_kk, rwkv_ka, rwkv_rk,
          rwkv_gn_w, rwkv_gn_b, mla_q_norm_g, mla_w_qb, mla_kv_norm_g, mla_w_kvb, w_out, ple_w, ple_norm_g,
          ple_gate_w):
    depth = w_in.shape[0]
    o_kr = C_MG
    o_mg = o_kr + QK_ROPE_DIM
    zeros = jnp.zeros((depth, D_MODEL, QK_ROPE_DIM), F32)
    win_a = w_in.astype(BF16)
    win_b = jnp.concatenate([w_in[:, :, o_mg:], w_in[:, :, o_kr:o_mg], zeros], axis=2).astype(BF16)

    zl = jnp.zeros((depth, LORA, RWKV_WIDTH), F32)
    wl = jnp.concatenate([jnp.concatenate([rwkv_w2, zl], axis=2),
                          jnp.concatenate([zl, rwkv_a2], axis=2)], axis=1).astype(BF16)

    wqb = mla_w_qb.reshape(depth, Q_LORA_RANK, MLA_HEADS, QK_NOPE_DIM + QK_ROPE_DIM)
    zq = jnp.zeros((depth, Q_LORA_RANK, MLA_HEADS, QK_ROPE_DIM), F32)
    wq = jnp.concatenate([wqb, zq], axis=-1).reshape(depth, Q_LORA_RANK, MLA_HEADS * QK_PAD).astype(BF16)
    wkvb = mla_w_kvb.reshape(depth, KV_LORA_RANK, MLA_HEADS, QK_NOPE_DIM + V_HEAD_DIM)
    wkv = jnp.concatenate([wkvb[..., :QK_NOPE_DIM].reshape(depth, KV_LORA_RANK, -1),
                           wkvb[..., QK_NOPE_DIM:].reshape(depth, KV_LORA_RANK, -1)], axis=2).astype(BF16)
    row = lambda x: x.reshape(depth, 1, -1)
    hd = lambda x: jnp.pad(x.reshape(depth, RWKV_HEADS, RWKV_HEAD_DIM),
                           ((0, 0), (0, 0), (LANE - RWKV_HEAD_DIM, 0)))
    return dict(
        ng=row(norm_mix_g), win_a=win_a, win_b=win_b, cw=conv_w,
        mu_rkv=row(rwkv_mu[:, :3 * RWKV_WIDTH]), mu_wa=row(rwkv_mu[:, 3 * RWKV_WIDTH:]),
        w0=row(rwkv_w0), a0=row(rwkv_a0), wl=wl, kk=row(rwkv_kk), ka=row(rwkv_ka),
        qg=row(mla_q_norm_g), wq=wq, kvg=row(mla_kv_norm_g), wkv=wkv,
        rk=row(rwkv_rk), gnw=hd(rwkv_gn_w), gnb=hd(rwkv_gn_b),
        wo=w_out.astype(BF16), wple=ple_w.astype(BF16), png=row(ple_norm_g), wpg=ple_gate_w.astype(BF16))


def kernel(x, p, positions, norm_mix_g, w_in, conv_w, rwkv_mu, rwkv_w0, rwkv_w2, rwkv_a0, rwkv_a2,
           rwkv_kk, rwkv_ka, rwkv_rk, rwkv_gn_w, rwkv_gn_b, mla_q_norm_g, mla_w_qb, mla_kv_norm_g,
           mla_w_kvb, w_out, ple_w, ple_norm_g, ple_gate_w, final_norm_g):
    batch, seq, _ = x.shape
    depth = w_in.shape[0]
    t = batch * seq

    half = QK_ROPE_DIM // 2
    inv_freq = 1.0 / (ROPE_THETA ** (jnp.arange(0, QK_ROPE_DIM, 2, dtype=F32) / QK_ROPE_DIM))
    inv_tile = jnp.concatenate([inv_freq, inv_freq, jnp.zeros((LANE - QK_ROPE_DIM,), F32)])
    lane = np.arange(LANE)
    ang = positions.astype(F32).reshape(t, 1) * inv_tile
    ck = jnp.where(lane < QK_ROPE_DIM, jnp.cos(ang), 0.0)
    sk = jnp.sin(ang) * jnp.asarray(np.where(lane < half, -1.0, 1.0), F32)

    h = x.reshape(t, D_MODEL)
    p = p.reshape(depth, t, PLE_DIM)
    final_g = final_norm_g.reshape(1, D_MODEL)
    lw = _prep(norm_mix_g, w_in, conv_w, rwkv_mu, rwkv_w0, rwkv_w2, rwkv_a0, rwkv_a2, rwkv_kk, rwkv_ka,
               rwkv_rk, rwkv_gn_w, rwkv_gn_b, mla_q_norm_g, mla_w_qb, mla_kv_norm_g, mla_w_kvb, w_out,
               ple_w, ple_norm_g, ple_gate_w)
    for i in range(depth):
        (yconv, *hm, wc, gate_r, bg, q, kq, vq, gate_m) = _proj_call(h, lw, i, ck, sk, batch, seq)
        y_rwkv = _rwkv_call(hm, wc, gate_r, bg, lw, i, batch, seq).reshape(t, RWKV_WIDTH)
        y_mla = _attn_call(q, kq, vq, gate_m, batch, seq).reshape(t, MLA_WIDTH)
        h = _out_call(h, yconv, y_rwkv, y_mla, p, i, lw, final_g, final=(i == depth - 1))
    return h.reshape(batch, seq, D_MODEL)
```

```python
import functools
import math

import jax
import jax.numpy as jnp
import numpy as np
from jax import lax
from jax.experimental import pallas as pl
from jax.experimental.pallas import tpu as pltpu

F32 = jnp.float32
BF16 = jnp.bfloat16

D_MODEL = 1024
CHUNK = 64
PLE_DIM = 256
NORM_EPS = 1e-6
CONV_WIDTH = 256
RWKV_HEADS = 4
RWKV_HEAD_DIM = 64
RWKV_WIDTH = RWKV_HEADS * RWKV_HEAD_DIM
LORA = 64
DECAY_SCALE = math.exp(-0.5)
GN_EPS = 64e-5
MLA_HEADS = 4
QK_NOPE_DIM = 128
QK_ROPE_DIM = 64
V_HEAD_DIM = 128
Q_LORA_RANK = 384
KV_LORA_RANK = 256
MLA_WIDTH = MLA_HEADS * V_HEAD_DIM
ROPE_THETA = 10000.0
D_MIX = CONV_WIDTH + RWKV_WIDTH + MLA_WIDTH

C_CONV = 0
C_RKV = 1024
C_WA = 1792
C_RG = 1920
C_QA = 2176
C_KVA = 2560
C_MG = 2816
C_KR = 3328
D_IN_P = 3456
QK_PAD = 256

LANE = 128
BF16_ROWS = 16
HALO = BF16_ROWS
TM = 512
OUT_ROW_BLOCKS = 2
TQ = 512
EXP_ROWS = 128
EXP_LAG = 1
RWKV_CHUNKS_PER_STEP = 4
VMEM_LIMIT = 56 * 1024 * 1024


def _dot(a, b):
    return jnp.dot(a.astype(BF16), b.astype(BF16), preferred_element_type=F32)


def _dot_nt(a, b):
    return lax.dot_general(a.astype(BF16), b.astype(BF16), (((1,), (1,)), ((), ())),
                           preferred_element_type=F32)


def _dot_tn(a, b):
    return lax.dot_general(a.astype(BF16), b.astype(BF16), (((0,), (0,)), ((), ())),
                           preferred_element_type=F32)


def _split2(x):
    hi = x.astype(BF16)
    return hi, (x - hi.astype(F32)).astype(BF16)


def _dot_mask(x, m01, left=False):
    hi, lo = _split2(x)
    if left:
        return jnp.dot(m01, hi, preferred_element_type=F32) + jnp.dot(m01, lo, preferred_element_type=F32)
    return jnp.dot(hi, m01, preferred_element_type=F32) + jnp.dot(lo, m01, preferred_element_type=F32)


def _chunk_cumsum(x):
    pos = lax.broadcasted_iota(jnp.int32, x.shape, 0) % CHUNK
    d = 1
    while d < CHUNK:
        x = x + jnp.where(pos >= d, pltpu.roll(x, d, axis=0), 0.0)
        d *= 2
    return x


def _swap_rope_halves(x):
    half = QK_ROPE_DIM // 2
    return pltpu.roll(x, half, axis=1) + pltpu.roll(x, LANE - half, axis=1)


def _rms(x, g):
    return x * lax.rsqrt(jnp.mean(x * x, axis=-1, keepdims=True) + NORM_EPS) * g


def _silu(x):
    return x * jax.nn.sigmoid(x)


def _proj_kernel(h_ref, ng_ref, wa_ref, wb_ref, cwt_ref, mu_rkv_ref, mu_wa_ref, w0_ref, a0_ref,
                 wl_ref, kk_ref, ka_ref, rk_ref, seg_ref, csum_ref, qg_ref, wq_ref, kvg_ref, wkv_ref,
                 ck_ref, sk_ref,
                 yconv_ref, rt_ref, at_ref, bt_ref, kt_ref, bh_ref, kh_ref, v_ref, wc_ref, gr_ref, bg_ref,
                 q_ref, kq_ref, vq_ref, gm_ref, z_scr, *, tiles_per_seq):
    hm_refs = (rt_ref, at_ref, bt_ref, kt_ref, bh_ref, kh_ref, v_ref)
    tm = h_ref.shape[0]
    tile = pl.ds(HALO, tm)

    @pl.when(pl.program_id(0) == 0)
    def _():
        z_scr[0:HALO, :] = jnp.zeros((HALO, D_IN_P), F32)

    u = _rms(h_ref[...], ng_ref[...]).astype(BF16)
    z_scr[tile, :C_MG] = jnp.dot(u, wa_ref[:, :C_MG], preferred_element_type=F32)
    z_scr[tile, C_MG:] = jnp.dot(u, wb_ref[...], preferred_element_type=F32)

    first = pl.program_id(0) % tiles_per_seq == 0
    row = lax.broadcasted_iota(jnp.int32, (tm, 1), 0)

    def cols(c, w, back=0):
        x = z_scr[pl.ds(HALO - back, tm), c:c + w]
        return jnp.where(row >= jnp.where(first, back, 0), x, 0.0) if back else x

    cwt = cwt_ref[...]
    conv = sum(cols(C_CONV + 256, 256, back) * cols(C_CONV + 512, 256, back) * cwt[2 - back:3 - back, :]
               for back in range(3))
    yconv_ref[...] = (cols(C_CONV, 256) * conv * _silu(cols(C_CONV + 768, 256))).astype(BF16)

    cur = cols(C_RKV, 768)
    rkv = cur + (cols(C_RKV, 768, 1) - cur) * mu_rkv_ref[...]
    cur = cols(C_WA, 128)
    wa = cur + (cols(C_WA, 128, 1) - cur) * mu_wa_ref[...]
    r = rkv[:, 0:256]
    k = rkv[:, 256:512]
    v = rkv[:, 512:768]
    lane = lax.broadcasted_iota(jnp.int32, wa.shape, 1)
    lora_in = jnp.where(lane < LORA, jnp.tanh(wa), wa)
    lora = jnp.dot(lora_in.astype(BF16), wl_ref[...], preferred_element_type=F32)
    wlog = -DECAY_SCALE * jax.nn.sigmoid(w0_ref[...] + lora[:, :RWKV_WIDTH])
    a = jax.nn.sigmoid(a0_ref[...] + lora[:, RWKV_WIDTH:])
    kk = k * kk_ref[...]
    kk = kk * lax.rsqrt(_dot_mask(kk * kk, seg_ref[...]) + 1e-12)
    kmod = k * (1.0 + (a - 1.0) * ka_ref[...])
    zb = kk * a
    cw = _chunk_cumsum(wlog)
    total = _dot_mask(wlog, csum_ref[...], left=True)
    nchunk = tm // CHUNK
    rest = jnp.broadcast_to(total[:, None, :], (nchunk, CHUNK, RWKV_WIDTH)).reshape(tm, RWKV_WIDTH) - cw
    inv = jnp.exp(-cw)
    to_end = jnp.exp(rest)
    wc = jnp.exp(total)
    ops = (r * jnp.exp(cw), -kk * jnp.exp(cw - wlog), zb * inv, kmod * inv, zb * to_end, kmod * to_end, v)
    zpad = jnp.zeros((tm, LANE - RWKV_HEAD_DIM), BF16)
    for hd in range(RWKV_HEADS):
        sl = slice(hd * RWKV_HEAD_DIM, (hd + 1) * RWKV_HEAD_DIM)
        for ref, val in zip(hm_refs, ops):
            piece = val[:, sl].astype(BF16)
            if ref is at_ref:
                piece = jnp.concatenate([piece, zpad], axis=1)
            elif ref is v_ref:
                piece = jnp.concatenate([zpad, piece], axis=1)
            ref[0, hd] = piece
        wc_ref[0, hd] = wc[:, sl]
    gate_r = _silu(cols(C_RG, 256))
    bonus = _dot_mask(r * kmod * rk_ref[...], seg_ref[...]) * v
    gr_ref[...] = gate_r
    bg_ref[...] = bonus * gate_r

    ck = ck_ref[...]
    sk = sk_ref[...]
    qn = _rms(cols(C_QA, Q_LORA_RANK), qg_ref[...]).astype(BF16)
    qm = jnp.dot(qn, wq_ref[...], preferred_element_type=F32)
    scale = math.log2(math.e) / math.sqrt(QK_NOPE_DIM + QK_ROPE_DIM)
    kvn = _rms(cols(C_KVA, KV_LORA_RANK), kvg_ref[...]).astype(BF16)
    kv = jnp.dot(kvn, wkv_ref[...], preferred_element_type=F32)
    kx = cols(C_KR, LANE)
    kr = kx * ck + _swap_rope_halves(kx) * sk
    for hd in range(MLA_HEADS):
        o = hd * QK_PAD
        qx = qm[:, o + LANE:o + QK_PAD]
        qr = qx * ck + _swap_rope_halves(qx) * sk
        q_ref[0, o:o + LANE, :] = (qm[:, o:o + LANE] * scale).T.astype(BF16)
        q_ref[0, o + LANE:o + QK_PAD, :] = (qr * scale).T.astype(BF16)
        kq_ref[:, o:o + LANE] = kv[:, hd * LANE:(hd + 1) * LANE].astype(BF16)
        kq_ref[:, o + LANE:o + QK_PAD] = kr.astype(BF16)
    vq_ref[0] = kv[:, MLA_HEADS * QK_NOPE_DIM:].T.astype(BF16)
    gm_ref[...] = _silu(cols(C_MG, MLA_WIDTH))

    shifted = slice(C_CONV + 256, C_RG)
    z_scr[0:HALO, shifted] = z_scr[pl.ds(tm, HALO), shifted]


def _const_spec(arr):
    return pl.BlockSpec(arr.shape, lambda *_: (0,) * arr.ndim)


def _layer_spec(arr, layer):
    zeros = (0,) * (arr.ndim - 1)
    return pl.BlockSpec((None,) + arr.shape[1:], lambda *_: (layer,) + zeros)


def _proj_call(h, lw, layer, ck, sk, batch, seq):
    t = h.shape[0]
    tiles_per_seq = seq // TM
    row = lambda w: pl.BlockSpec((TM, w), lambda i: (i, 0))
    chunk_id = np.arange(TM) // CHUNK
    csum = jnp.asarray(np.arange(TM // CHUNK)[:, None] == chunk_id[None, :], BF16)
    head = np.arange(RWKV_WIDTH) // RWKV_HEAD_DIM
    seg = jnp.asarray(head[:, None] == head[None, :], BF16)
    per_layer = lambda *names: [(lw[n], _layer_spec(lw[n], layer)) for n in names]
    shared = lambda *arrs: [(a, _const_spec(a)) for a in arrs]
    consts = (per_layer("ng", "win_a", "win_b", "cw", "mu_rkv", "mu_wa", "w0", "a0", "wl", "kk", "ka", "rk")
              + shared(seg, csum) + per_layer("qg", "wq", "kvg", "wkv"))
    hm_widths = [RWKV_HEAD_DIM, LANE] + [RWKV_HEAD_DIM] * 4 + [LANE]
    hm = [pl.BlockSpec((1, RWKV_HEADS, TM, w), lambda i: (i // tiles_per_seq, 0, i % tiles_per_seq, 0))
          for w in hm_widths]
    hm_shapes = [jax.ShapeDtypeStruct((batch, RWKV_HEADS, seq, w), BF16) for w in hm_widths]
    wc_spec = pl.BlockSpec((1, RWKV_HEADS, TM // CHUNK, RWKV_HEAD_DIM),
                           lambda i: (i // tiles_per_seq, 0, i % tiles_per_seq, 0))
    out_shape = [jax.ShapeDtypeStruct((t, CONV_WIDTH), BF16)] + hm_shapes + [
        jax.ShapeDtypeStruct((batch, RWKV_HEADS, seq // CHUNK, RWKV_HEAD_DIM), F32),
        jax.ShapeDtypeStruct((t, RWKV_WIDTH), F32),
        jax.ShapeDtypeStruct((t, RWKV_WIDTH), F32),
        jax.ShapeDtypeStruct((t // TM, MLA_HEADS * QK_PAD, TM), BF16),
        jax.ShapeDtypeStruct((t, MLA_HEADS * QK_PAD), BF16),
        jax.ShapeDtypeStruct((t // TM, MLA_WIDTH, TM), BF16),
        jax.ShapeDtypeStruct((t, MLA_WIDTH), F32),
    ]
    vt_spec = pl.BlockSpec((1, MLA_WIDTH, TM), lambda i: (i, 0, 0))
    qt_spec = pl.BlockSpec((1, MLA_HEADS * QK_PAD, TM), lambda i: (i, 0, 0))
    out_specs = [row(CONV_WIDTH)] + hm + [wc_spec, row(RWKV_WIDTH), row(RWKV_WIDTH),
                                                  qt_spec, row(MLA_HEADS * QK_PAD),
                                                  vt_spec, row(MLA_WIDTH)]
    return pl.pallas_call(
        functools.partial(_proj_kernel, tiles_per_seq=tiles_per_seq),
        grid=(t // TM,),
        in_specs=[row(D_MODEL)] + [s for _, s in consts] + [row(LANE), row(LANE)],
        out_specs=out_specs,
        out_shape=out_shape,
        scratch_shapes=[pltpu.VMEM((HALO + TM, D_IN_P), F32)],
        compiler_params=pltpu.CompilerParams(dimension_semantics=("arbitrary",),
                                             vmem_limit_bytes=VMEM_LIMIT),
        name="proj",
    )(h, *[a for a, _ in consts], ck, sk)


def _rwkv_chunk_local(r_t, a_t, b_t, k_t, bh, kh, v, wc, tri_incl, tri_strict, eye):
    n = range(len(r_t))
    half = RWKV_HEAD_DIM
    lhs = [jnp.concatenate([a_t[i][:, :half], r_t[i]], axis=0) for i in n]
    bk = [jnp.concatenate([b_t[i], k_t[i]], axis=0) for i in n]
    abk = [_dot_nt(lhs[i], bk[i]) for i in n]
    top = [jnp.where(tri_strict, abk[i][:CHUNK], 0.0).astype(BF16) for i in n]
    bot = [jnp.where(tri_incl, abk[i][CHUNK:], 0.0).astype(BF16) for i in n]

    zeros = jnp.zeros((CHUNK, LANE), BF16)
    x = [a_t[i].astype(F32) + _dot(top[i], jnp.concatenate([zeros, v[i]], axis=0)) for i in n]
    p = [top[i][:, :half] for i in n]
    for it in range(6):
        x = [x[i] + _dot(p[i], x[i]) for i in n]
        if it < 5:
            p = [_dot(p[i], p[i]).astype(BF16) for i in n]
    xv = [jnp.concatenate([x[i].astype(BF16), v[i]], axis=0) for i in n]

    yloc = [_dot(bot[i], xv[i]) for i in n]
    qp = [r_t[i].astype(F32) + yloc[i][:, :half] for i in n]
    nc = [_dot_tn(jnp.concatenate([bh[i], kh[i]], axis=0), xv[i]) for i in n]
    mc = [jnp.where(eye, wc[i], 0.0) + nc[i][:, :half] for i in n]
    return qp, yloc, mc, nc


def _rwkv_kernel(rt_ref, at_ref, bt_ref, kt_ref, bh_ref, kh_ref, v_ref, wc_ref, gate_ref, bg_ref,
                 gnw_ref, gnb_ref, gnm_ref, y_ref, state_ref):
    step = pl.program_id(0)
    batch = rt_ref.shape[0]

    @pl.when(step == 0)
    def _():
        state_ref[...] = jnp.zeros_like(state_ref)

    ri = lax.broadcasted_iota(jnp.int32, (CHUNK, LANE), 0)
    li = lax.broadcasted_iota(jnp.int32, (CHUNK, LANE), 1)
    ci = li % CHUNK
    tri_incl = ri >= ci
    tri_strict = ri > ci
    eye = (ri == li)[:, :RWKV_HEAD_DIM]
    value_lanes = li >= RWKV_HEAD_DIM
    seqs = [(b, hd) for b in range(batch) for hd in range(RWKV_HEADS)]
    items = [(c, b, hd) for c in range(RWKV_CHUNKS_PER_STEP) for b, hd in seqs]
    n = range(len(items))
    load = lambda ref: [ref[b, hd, pl.ds(c * CHUNK, CHUNK), :] for c, b, hd in items]
    wc = [wc_ref[b, hd, pl.ds(step * RWKV_CHUNKS_PER_STEP + c, 1), :] for c, b, hd in items]
    qp, yloc, mc, nc = _rwkv_chunk_local(load(rt_ref), load(at_ref), load(bt_ref), load(kt_ref),
                                         load(bh_ref), load(kh_ref), load(v_ref), wc,
                                         tri_incl, tri_strict, eye)
    ns = range(len(seqs))
    states = [state_ref[s] for s in ns]
    y = []
    for c in range(RWKV_CHUNKS_PER_STEP):
        o = c * len(seqs)
        y += [_dot(qp[o + s], states[s]) + yloc[o + s] for s in ns]
        states = [jnp.where(value_lanes, _dot(mc[o + s], states[s]) + nc[o + s], 0.0) for s in ns]
    for s in ns:
        state_ref[s] = states[s]
    gnm = gnm_ref[...]
    head = [hd for _, _, hd in items]
    yall = jnp.concatenate(y, axis=0)
    yc = yall - _dot(yall, gnm)
    scale = lax.rsqrt(_dot(yc * yc, gnm) + GN_EPS)
    part = lambda a, i: a[i * CHUNK:(i + 1) * CHUNK]
    gn = [part(yc, i) * part(scale, i) * gnw_ref[head[i]:head[i] + 1, :]
          + gnb_ref[head[i]:head[i] + 1, :] for i in n]
    for c in range(RWKV_CHUNKS_PER_STEP):
        rows = pl.ds(c * CHUNK, CHUNK)
        for b in range(batch):
            o = (c * batch + b) * RWKV_HEADS
            pair = [pltpu.roll(gn[o + hd], RWKV_HEAD_DIM, axis=1) + gn[o + hd + 1] for hd in (0, 2)]
            y_ref[b, rows, :] = (jnp.concatenate(pair, axis=1) * gate_ref[b, rows, :]
                                 + bg_ref[b, rows, :]).astype(BF16)


def _rwkv_call(hm, wc, gate, bg, lw, layer, batch, seq):
    tc = RWKV_CHUNKS_PER_STEP * CHUNK
    hm_spec = lambda a: pl.BlockSpec((batch, RWKV_HEADS, tc, a.shape[-1]), lambda c: (0, 0, c, 0))
    nat = pl.BlockSpec((batch, tc, RWKV_WIDTH), lambda c: (0, c, 0))
    value = np.arange(LANE) >= RWKV_HEAD_DIM
    gnm = jnp.asarray((value[:, None] & value[None, :]) / RWKV_HEAD_DIM, BF16)
    return pl.pallas_call(
        _rwkv_kernel,
        grid=(seq // tc,),
        in_specs=[hm_spec(a) for a in hm] + [_const_spec(wc), nat, nat,
                                             _layer_spec(lw["gnw"], layer), _layer_spec(lw["gnb"], layer),
                                             _const_spec(gnm)],
        out_specs=nat,
        out_shape=jax.ShapeDtypeStruct((batch, seq, RWKV_WIDTH), BF16),
        scratch_shapes=[pltpu.VMEM((batch * RWKV_HEADS, RWKV_HEAD_DIM, LANE), F32)],
        compiler_params=pltpu.CompilerParams(dimension_semantics=("arbitrary",),
                                             vmem_limit_bytes=VMEM_LIMIT),
        name="rwkv",
    )(*hm, wc, gate.reshape(batch, seq, RWKV_WIDTH), bg.reshape(batch, seq, RWKV_WIDTH),
      lw["gnw"], lw["gnb"], gnm)


def _attn_kernel(qt_ref, k_ref, vt_ref, g_ref, o_ref, m_scr, acc_scr,
                 s0_scr, s1_scr, c0_scr, c1_scr, p0_scr, p1_scr, a0_scr, a1_scr):
    tq = s0_scr.shape[1]
    n_q = qt_ref.shape[1]
    sbufs = ((s0_scr, c0_scr), (s1_scr, c1_scr))
    pbufs = ((p0_scr, a0_scr), (p1_scr, a1_scr))
    ones = jnp.ones((BF16_ROWS, tq), BF16)
    tiles = [(i, j) for i in range(n_q) for j in range(i + 1)]

    def scores(i, j, slot):
        s_ref, c_ref = sbufs[slot]
        k = k_ref[0, j * tq:(j + 1) * tq, :]
        s = jnp.dot(k, qt_ref[0, i], preferred_element_type=F32)
        if i == j:
            kc = lax.broadcasted_iota(jnp.int32, s.shape, 0) // CHUNK
            qc = lax.broadcasted_iota(jnp.int32, s.shape, 1) // CHUNK
            s = jnp.where(kc <= qc, s, -1e30)
        s_ref[...] = s
        c_ref[...] = jnp.max(s, axis=0, keepdims=True)

    def exponentiate(i, j, slot):
        s_ref, c_ref = sbufs[slot]
        p_ref, a_ref = pbufs[slot]
        if j == 0:
            m_new = c_ref[...]
        else:
            m_old = m_scr[...]
            m_new = jnp.maximum(m_old, c_ref[...])
            a_ref[...] = jnp.exp2(m_old - m_new)
        m8 = jnp.broadcast_to(m_new, (8, tq))
        done = []
        for b in range(tq // EXP_ROWS):
            mb = m8
            if b >= EXP_LAG:
                seen = jnp.concatenate([done[b - EXP_LAG]] * (tq // LANE), axis=1)
                mb = jnp.where(seen <= 1.0, m8, 0.0)
            rows = pl.ds(b * EXP_ROWS, EXP_ROWS)
            x = s_ref[rows, :].reshape(EXP_ROWS // 8, 8, tq) - mb[None]
            pb = jnp.exp2(x.reshape(EXP_ROWS, tq).astype(BF16))
            p_ref[rows, :] = pb
            done.append(pb[0:BF16_ROWS, 0:LANE].astype(F32)[0:8])
        if j < i:
            m_scr[...] = m_new

    def accumulate(i, j, slot):
        p_ref, a_ref = pbufs[slot]
        vt1 = jnp.concatenate([vt_ref[0, j], ones], axis=0)
        acc = jnp.dot(vt1, p_ref[...], preferred_element_type=F32)
        if j > 0:
            acc = a_ref[...] * acc_scr[...] + acc
        if j == i:
            rows = slice(i * tq, (i + 1) * tq)
            o = acc[:V_HEAD_DIM] / acc[V_HEAD_DIM:V_HEAD_DIM + 1]
            o_ref[0, rows, :] = (o.T * g_ref[0, rows, :]).astype(BF16)
        else:
            acc_scr[...] = acc

    scores(*tiles[0], 0)
    scores(*tiles[1], 1)
    exponentiate(*tiles[0], 0)
    for t, (i, j) in enumerate(tiles):
        if t + 2 < len(tiles):
            scores(*tiles[t + 2], t % 2)
        if t + 1 < len(tiles):
            exponentiate(*tiles[t + 1], (t + 1) % 2)
        accumulate(i, j, t % 2)


def _attn_call(qt, k, vt, gate, batch, seq):
    qt = qt.reshape(batch, seq // TQ, MLA_HEADS * QK_PAD, TQ)
    k = k.reshape(batch, seq, MLA_HEADS * QK_PAD)
    vt = vt.reshape(batch, seq // TQ, MLA_WIDTH, TQ)
    gate = gate.reshape(batch, seq, MLA_WIDTH)
    return pl.pallas_call(
        _attn_kernel,
        grid=(batch, MLA_HEADS),
        in_specs=[pl.BlockSpec((1, seq // TQ, QK_PAD, TQ), lambda b, h: (b, 0, h, 0)),
                  pl.BlockSpec((1, seq, QK_PAD), lambda b, h: (b, 0, h)),
                  pl.BlockSpec((1, seq // TQ, V_HEAD_DIM, TQ), lambda b, h: (b, 0, h, 0)),
                  pl.BlockSpec((1, seq, V_HEAD_DIM), lambda b, h: (b, 0, h))],
        out_specs=pl.BlockSpec((1, seq, V_HEAD_DIM), lambda b, h: (b, 0, h)),
        out_shape=jax.ShapeDtypeStruct((batch, seq, MLA_WIDTH), BF16),
        scratch_shapes=[pltpu.VMEM((1, TQ), F32),
                        pltpu.VMEM((V_HEAD_DIM + BF16_ROWS, TQ), F32),
                        pltpu.VMEM((TQ, TQ), F32), pltpu.VMEM((TQ, TQ), F32),
                        pltpu.VMEM((1, TQ), F32), pltpu.VMEM((1, TQ), F32),
                        pltpu.VMEM((TQ, TQ), BF16), pltpu.VMEM((TQ, TQ), BF16),
                        pltpu.VMEM((1, TQ), F32), pltpu.VMEM((1, TQ), F32)],
        compiler_params=pltpu.CompilerParams(
            dimension_semantics=("arbitrary", "arbitrary"), vmem_limit_bytes=VMEM_LIMIT),
        name="attn",
    )(qt, k, vt, gate)


def _out_kernel(h_ref, yc_ref, yr_ref, ym_ref, wo_ref, p_ref, wple_ref, png_ref, wpg_ref, fg_ref,
                o_ref, *, final):
    tm = h_ref.shape[0]
    blocks = [pl.ds(r, tm // OUT_ROW_BLOCKS) for r in range(0, tm, tm // OUT_ROW_BLOCKS)]
    ycat = [jnp.concatenate([yc_ref[b, :], yr_ref[b, :], ym_ref[b, :]], axis=-1) for b in blocks]
    h = [h_ref[b, :] + jnp.dot(y, wo_ref[...], preferred_element_type=F32) for b, y in zip(blocks, ycat)]
    hn = [_rms(x, png_ref[...]).astype(BF16) for x in h]
    gate = [jax.nn.sigmoid(jnp.dot(x, wpg_ref[...], preferred_element_type=F32)) for x in hn]
    ple = [jnp.dot(p_ref[b, :].astype(BF16), wple_ref[...], preferred_element_type=F32) for b in blocks]
    h = [x + e * g for x, e, g in zip(h, ple, gate)]
    if final:
        h = [_rms(x, fg_ref[...]) for x in h]
    for b, x in zip(blocks, h):
        o_ref[b, :] = x


def _out_call(h, yc, yr, ym, p, layer, lw, final_g, final):
    t = h.shape[0]
    row = lambda w: pl.BlockSpec((TM, w), lambda i: (i, 0))
    return pl.pallas_call(
        functools.partial(_out_kernel, final=final),
        grid=(t // TM,),
        in_specs=[row(D_MODEL), row(CONV_WIDTH), row(RWKV_WIDTH), row(MLA_WIDTH),
                  _layer_spec(lw["wo"], layer),
                  pl.BlockSpec((None, TM, PLE_DIM), lambda i: (layer, i, 0)),
                  _layer_spec(lw["wple"], layer), _layer_spec(lw["png"], layer),
                  _layer_spec(lw["wpg"], layer), _const_spec(final_g)],
        out_specs=row(D_MODEL),
        out_shape=jax.ShapeDtypeStruct((t, D_MODEL), F32),
        compiler_params=pltpu.CompilerParams(dimension_semantics=("arbitrary",),
                                             vmem_limit_bytes=VMEM_LIMIT),
        name="out",
    )(h, yc, yr, ym, lw["wo"], p, lw["wple"], lw["png"], lw["wpg"], final_g)


def _prep(norm_mix_g, w_in, conv_w, rwkv_mu, rwkv_w0, rwkv_w2, rwkv_a0, rwkv_a2, rwkv_kk, rwkv_ka, rwkv_rk,
          rwkv_gn_w, rwkv_gn_b, mla_q_norm_g, mla_w_qb, mla_kv_norm_g, mla_w_kvb, w_out, ple_w, ple_norm_g,
          ple_gate_w):
    depth = w_in.shape[0]
    o_kr = C_MG
    o_mg = o_kr + QK_ROPE_DIM
    zeros = jnp.zeros((depth, D_MODEL, QK_ROPE_DIM), F32)
    win_a = w_in.astype(BF16)
    win_b = jnp.concatenate([w_in[:, :, o_mg:], w_in[:, :, o_kr:o_mg], zeros], axis=2).astype(BF16)

    zl = jnp.zeros((depth, LORA, RWKV_WIDTH), F32)
    wl = jnp.concatenate([jnp.concatenate([rwkv_w2, zl], axis=2),
                          jnp.concatenate([zl, rwkv_a2], axis=2)], axis=1).astype(BF16)

    wqb = mla_w_qb.reshape(depth, Q_LORA_RANK, MLA_HEADS, QK_NOPE_DIM + QK_ROPE_DIM)
    zq = jnp.zeros((depth, Q_LORA_RANK, MLA_HEADS, QK_ROPE_DIM), F32)
    wq = jnp.concatenate([wqb, zq], axis=-1).reshape(depth, Q_LORA_RANK, MLA_HEADS * QK_PAD).astype(BF16)
    wkvb = mla_w_kvb.reshape(depth, KV_LORA_RANK, MLA_HEADS, QK_NOPE_DIM + V_HEAD_DIM)
    wkv = jnp.concatenate([wkvb[..., :QK_NOPE_DIM].reshape(depth, KV_LORA_RANK, -1),
                           wkvb[..., QK_NOPE_DIM:].reshape(depth, KV_LORA_RANK, -1)], axis=2).astype(BF16)
    row = lambda x: x.reshape(depth, 1, -1)
    hd = lambda x: jnp.pad(x.reshape(depth, RWKV_HEADS, RWKV_HEAD_DIM),
                           ((0, 0), (0, 0), (LANE - RWKV_HEAD_DIM, 0)))
    return dict(
        ng=row(norm_mix_g), win_a=win_a, win_b=win_b, cw=conv_w,
        mu_rkv=row(rwkv_mu[:, :3 * RWKV_WIDTH]), mu_wa=row(rwkv_mu[:, 3 * RWKV_WIDTH:]),
        w0=row(rwkv_w0), a0=row(rwkv_a0), wl=wl, kk=row(rwkv_kk), ka=row(rwkv_ka),
        qg=row(mla_q_norm_g), wq=wq, kvg=row(mla_kv_norm_g), wkv=wkv,
        rk=row(rwkv_rk), gnw=hd(rwkv_gn_w), gnb=hd(rwkv_gn_b),
        wo=w_out.astype(BF16), wple=ple_w.astype(BF16), png=row(ple_norm_g), wpg=ple_gate_w.astype(BF16))


def kernel(x, p, positions, norm_mix_g, w_in, conv_w, rwkv_mu, rwkv_w0, rwkv_w2, rwkv_a0, rwkv_a2,
           rwkv_kk, rwkv_ka, rwkv_rk, rwkv_gn_w, rwkv_gn_b, mla_q_norm_g, mla_w_qb, mla_kv_norm_g,
           mla_w_kvb, w_out, ple_w, ple_norm_g, ple_gate_w, final_norm_g):
    batch, seq, _ = x.shape
    depth = w_in.shape[0]
    t = batch * seq

    half = QK_ROPE_DIM // 2
    inv_freq = 1.0 / (ROPE_THETA ** (jnp.arange(0, QK_ROPE_DIM, 2, dtype=F32) / QK_ROPE_DIM))
    inv_tile = jnp.concatenate([inv_freq, inv_freq, jnp.zeros((LANE - QK_ROPE_DIM,), F32)])
    lane = np.arange(LANE)
    ang = positions.astype(F32).reshape(t, 1) * inv_tile
    ck = jnp.where(lane < QK_ROPE_DIM, jnp.cos(ang), 0.0)
    sk = jnp.sin(ang) * jnp.asarray(np.where(lane < half, -1.0, 1.0), F32)

    h = x.reshape(t, D_MODEL)
    p = p.reshape(depth, t, PLE_DIM)
    final_g = final_norm_g.reshape(1, D_MODEL)
    lw = _prep(norm_mix_g, w_in, conv_w, rwkv_mu, rwkv_w0, rwkv_w2, rwkv_a0, rwkv_a2, rwkv_kk, rwkv_ka,
               rwkv_rk, rwkv_gn_w, rwkv_gn_b, mla_q_norm_g, mla_w_qb, mla_kv_norm_g, mla_w_kvb, w_out,
               ple_w, ple_norm_g, ple_gate_w)
    for i in range(depth):
        (yconv, *hm, wc, gate_r, bg, q, kq, vq, gate_m) = _proj_call(h, lw, i, ck, sk, batch, seq)
        y_rwkv = _rwkv_call(hm, wc, gate_r, bg, lw, i, batch, seq).reshape(t, RWKV_WIDTH)
        y_mla = _attn_call(q, kq, vq, gate_m, batch, seq).reshape(t, MLA_WIDTH)
        h = _out_call(h, yconv, y_rwkv, y_mla, p, i, lw, final_g, final=(i == depth - 1))
    return h.reshape(batch, seq, D_MODEL)
```

```python
import functools
import math

import jax
import jax.numpy as jnp
import numpy as np
from jax import lax
from jax.experimental import pallas as pl
from jax.experimental.pallas import tpu as pltpu

F32 = jnp.float32
BF16 = jnp.bfloat16

D_MODEL = 1024
CHUNK = 64
PLE_DIM = 256
NORM_EPS = 1e-6
CONV_WIDTH = 256
RWKV_HEADS = 4
RWKV_HEAD_DIM = 64
RWKV_WIDTH = RWKV_HEADS * RWKV_HEAD_DIM
LORA = 64
DECAY_SCALE = math.exp(-0.5)
GN_EPS = 64e-5
MLA_HEADS = 4
QK_NOPE_DIM = 128
QK_ROPE_DIM = 64
V_HEAD_DIM = 128
Q_LORA_RANK = 384
KV_LORA_RANK = 256
MLA_WIDTH = MLA_HEADS * V_HEAD_DIM
ROPE_THETA = 10000.0
D_MIX = CONV_WIDTH + RWKV_WIDTH + MLA_WIDTH

C_CONV = 0
C_RKV = 1024
C_WA = 1792
C_RG = 1920
C_QA = 2176
C_KVA = 2560
C_MG = 2816
C_KR = 3328
D_IN_P = 3456
QK_PAD = 256

LANE = 128
BF16_ROWS = 16
HALO = BF16_ROWS
TM = 512
OUT_ROW_BLOCKS = 2
TQ = 512
EXP_ROWS = 128
EXP_LAG = 1
RWKV_CHUNKS_PER_STEP = 4
VMEM_LIMIT = 56 * 1024 * 1024


def _dot(a, b):
    return jnp.dot(a.astype(BF16), b.astype(BF16), preferred_element_type=F32)


def _dot_nt(a, b):
    return lax.dot_general(a.astype(BF16), b.astype(BF16), (((1,), (1,)), ((), ())),
                           preferred_element_type=F32)


def _dot_tn(a, b):
    return lax.dot_general(a.astype(BF16), b.astype(BF16), (((0,), (0,)), ((), ())),
                           preferred_element_type=F32)


def _split2(x):
    hi = x.astype(BF16)
    return hi, (x - hi.astype(F32)).astype(BF16)


def _dot_mask(x, m01, left=False):
    hi, lo = _split2(x)
    if left:
        return jnp.dot(m01, hi, preferred_element_type=F32) + jnp.dot(m01, lo, preferred_element_type=F32)
    return jnp.dot(hi, m01, preferred_element_type=F32) + jnp.dot(lo, m01, preferred_element_type=F32)


def _chunk_cumsum(x):
    pos = lax.broadcasted_iota(jnp.int32, x.shape, 0) % CHUNK
    d = 1
    while d < CHUNK:
        x = x + jnp.where(pos >= d, pltpu.roll(x, d, axis=0), 0.0)
        d *= 2
    return x


def _swap_rope_halves(x):
    half = QK_ROPE_DIM // 2
    return pltpu.roll(x, half, axis=1) + pltpu.roll(x, LANE - half, axis=1)


def _rms(x, g):
    return x * lax.rsqrt(jnp.mean(x * x, axis=-1, keepdims=True) + NORM_EPS) * g


def _silu(x):
    return x * jax.nn.sigmoid(x)


def _proj_kernel(h_ref, ng_ref, wa_ref, wb_ref, cwt_ref, mu_rkv_ref, mu_wa_ref, w0_ref, a0_ref,
                 wl_ref, kk_ref, ka_ref, rk_ref, seg_ref, csum_ref, qg_ref, wq_ref, kvg_ref, wkv_ref,
                 ck_ref, sk_ref,
                 yconv_ref, rt_ref, at_ref, bt_ref, kt_ref, bh_ref, kh_ref, v_ref, wc_ref, gr_ref, bg_ref,
                 q_ref, kq_ref, vq_ref, gm_ref, z_scr, *, tiles_per_seq):
    hm_refs = (rt_ref, at_ref, bt_ref, kt_ref, bh_ref, kh_ref, v_ref)
    tm = h_ref.shape[0]
    tile = pl.ds(HALO, tm)

    @pl.when(pl.program_id(0) == 0)
    def _():
        z_scr[0:HALO, :] = jnp.zeros((HALO, D_IN_P), F32)

    u = _rms(h_ref[...], ng_ref[...]).astype(BF16)
    z_scr[tile, :C_MG] = jnp.dot(u, wa_ref[:, :C_MG], preferred_element_type=F32)
    z_scr[tile, C_MG:] = jnp.dot(u, wb_ref[...], preferred_element_type=F32)

    first = pl.program_id(0) % tiles_per_seq == 0
    row = lax.broadcasted_iota(jnp.int32, (tm, 1), 0)

    def cols(c, w, back=0):
        x = z_scr[pl.ds(HALO - back, tm), c:c + w]
        return jnp.where(row >= jnp.where(first, back, 0), x, 0.0) if back else x

    cwt = cwt_ref[...]
    conv = sum(cols(C_CONV + 256, 256, back) * cols(C_CONV + 512, 256, back) * cwt[2 - back:3 - back, :]
               for back in range(3))
    yconv_ref[...] = (cols(C_CONV, 256) * conv * _silu(cols(C_CONV + 768, 256))).astype(BF16)

    cur = cols(C_RKV, 768)
    rkv = cur + (cols(C_RKV, 768, 1) - cur) * mu_rkv_ref[...]
    cur = cols(C_WA, 128)
    wa = cur + (cols(C_WA, 128, 1) - cur) * mu_wa_ref[...]
    r = rkv[:, 0:256]
    k = rkv[:, 256:512]
    v = rkv[:, 512:768]
    lane = lax.broadcasted_iota(jnp.int32, wa.shape, 1)
    lora_in = jnp.where(lane < LORA, jnp.tanh(wa), wa)
    lora = jnp.dot(lora_in.astype(BF16), wl_ref[...], preferred_element_type=F32)
    wlog = -DECAY_SCALE * jax.nn.sigmoid(w0_ref[...] + lora[:, :RWKV_WIDTH])
    a = jax.nn.sigmoid(a0_ref[...] + lora[:, RWKV_WIDTH:])
    kk = k * kk_ref[...]
    kk = kk * lax.rsqrt(_dot_mask(kk * kk, seg_ref[...]) + 1e-12)
    kmod = k * (1.0 + (a - 1.0) * ka_ref[...])
    zb = kk * a
    cw = _chunk_cumsum(wlog)
    total = _dot_mask(wlog, csum_ref[...], left=True)
    nchunk = tm // CHUNK
    rest = jnp.broadcast_to(total[:, None, :], (nchunk, CHUNK, RWKV_WIDTH)).reshape(tm, RWKV_WIDTH) - cw
    inv = jnp.exp(-cw)
    to_end = jnp.exp(rest)
    wc = jnp.exp(total)
    ops = (r * jnp.exp(cw), -kk * jnp.exp(cw - wlog), zb * inv, kmod * inv, zb * to_end, kmod * to_end, v)
    zpad = jnp.zeros((tm, LANE - RWKV_HEAD_DIM), BF16)
    for hd in range(RWKV_HEADS):
        sl = slice(hd * RWKV_HEAD_DIM, (hd + 1) * RWKV_HEAD_DIM)
        for ref, val in zip(hm_refs, ops):
            piece = val[:, sl].astype(BF16)
            if ref is at_ref:
                piece = jnp.concatenate([piece, zpad], axis=1)
            elif ref is v_ref:
                piece = jnp.concatenate([zpad, piece], axis=1)
            ref[0, hd] = piece
        wc_ref[0, hd] = wc[:, sl]
    gate_r = _silu(cols(C_RG, 256))
    bonus = _dot_mask(r * kmod * rk_ref[...], seg_ref[...]) * v
    gr_ref[...] = gate_r
    bg_ref[...] = bonus * gate_r

    ck = ck_ref[...]
    sk = sk_ref[...]
    qn = _rms(cols(C_QA, Q_LORA_RANK), qg_ref[...]).astype(BF16)
    qm = jnp.dot(qn, wq_ref[...], preferred_element_type=F32)
    scale = math.log2(math.e) / math.sqrt(QK_NOPE_DIM + QK_ROPE_DIM)
    kvn = _rms(cols(C_KVA, KV_LORA_RANK), kvg_ref[...]).astype(BF16)
    kv = jnp.dot(kvn, wkv_ref[...], preferred_element_type=F32)
    kx = cols(C_KR, LANE)
    kr = kx * ck + _swap_rope_halves(kx) * sk
    for hd in range(MLA_HEADS):
        o = hd * QK_PAD
        qx = qm[:, o + LANE:o + QK_PAD]
        qr = qx * ck + _swap_rope_halves(qx) * sk
        q_ref[0, o:o + LANE, :] = (qm[:, o:o + LANE] * scale).T.astype(BF16)
        q_ref[0, o + LANE:o + QK_PAD, :] = (qr * scale).T.astype(BF16)
        kq_ref[:, o:o + LANE] = kv[:, hd * LANE:(hd + 1) * LANE].astype(BF16)
        kq_ref[:, o + LANE:o + QK_PAD] = kr.astype(BF16)
    vq_ref[0] = kv[:, MLA_HEADS * QK_NOPE_DIM:].T.astype(BF16)
    gm_ref[...] = _silu(cols(C_MG, MLA_WIDTH))

    shifted = slice(C_CONV + 256, C_RG)
    z_scr[0:HALO, shifted] = z_scr[pl.ds(tm, HALO), shifted]


def _const_spec(arr):
    return pl.BlockSpec(arr.shape, lambda *_: (0,) * arr.ndim)


def _layer_spec(arr, layer):
    zeros = (0,) * (arr.ndim - 1)
    return pl.BlockSpec((None,) + arr.shape[1:], lambda *_: (layer,) + zeros)


def _proj_call(h, lw, layer, ck, sk, batch, seq):
    t = h.shape[0]
    tiles_per_seq = seq // TM
    row = lambda w: pl.BlockSpec((TM, w), lambda i: (i, 0))
    chunk_id = np.arange(TM) // CHUNK
    csum = jnp.asarray(np.arange(TM // CHUNK)[:, None] == chunk_id[None, :], BF16)
    head = np.arange(RWKV_WIDTH) // RWKV_HEAD_DIM
    seg = jnp.asarray(head[:, None] == head[None, :], BF16)
    per_layer = lambda *names: [(lw[n], _layer_spec(lw[n], layer)) for n in names]
    shared = lambda *arrs: [(a, _const_spec(a)) for a in arrs]
    consts = (per_layer("ng", "win_a", "win_b", "cw", "mu_rkv", "mu_wa", "w0", "a0", "wl", "kk", "ka", "rk")
              + shared(seg, csum) + per_layer("qg", "wq", "kvg", "wkv"))
    hm_widths = [RWKV_HEAD_DIM, LANE] + [RWKV_HEAD_DIM] * 4 + [LANE]
    hm = [pl.BlockSpec((1, RWKV_HEADS, TM, w), lambda i: (i // tiles_per_seq, 0, i % tiles_per_seq, 0))
          for w in hm_widths]
    hm_shapes = [jax.ShapeDtypeStruct((batch, RWKV_HEADS, seq, w), BF16) for w in hm_widths]
    wc_spec = pl.BlockSpec((1, RWKV_HEADS, TM // CHUNK, RWKV_HEAD_DIM),
                           lambda i: (i // tiles_per_seq, 0, i % tiles_per_seq, 0))
    out_shape = [jax.ShapeDtypeStruct((t, CONV_WIDTH), BF16)] + hm_shapes + [
        jax.ShapeDtypeStruct((batch, RWKV_HEADS, seq // CHUNK, RWKV_HEAD_DIM), F32),
        jax.ShapeDtypeStruct((t, RWKV_WIDTH), F32),
        jax.ShapeDtypeStruct((t, RWKV_WIDTH), F32),
        jax.ShapeDtypeStruct((t // TM, MLA_HEADS * QK_PAD, TM), BF16),
        jax.ShapeDtypeStruct((t, MLA_HEADS * QK_PAD), BF16),
        jax.ShapeDtypeStruct((t // TM, MLA_WIDTH, TM), BF16),
        jax.ShapeDtypeStruct((t, MLA_WIDTH), F32),
    ]
    vt_spec = pl.BlockSpec((1, MLA_WIDTH, TM), lambda i: (i, 0, 0))
    qt_spec = pl.BlockSpec((1, MLA_HEADS * QK_PAD, TM), lambda i: (i, 0, 0))
    out_specs = [row(CONV_WIDTH)] + hm + [wc_spec, row(RWKV_WIDTH), row(RWKV_WIDTH),
                                                  qt_spec, row(MLA_HEADS * QK_PAD),
                                                  vt_spec, row(MLA_WIDTH)]
    return pl.pallas_call(
        functools.partial(_proj_kernel, tiles_per_seq=tiles_per_seq),
        grid=(t // TM,),
        in_specs=[row(D_MODEL)] + [s for _, s in consts] + [row(LANE), row(LANE)],
        out_specs=out_specs,
        out_shape=out_shape,
        scratch_shapes=[pltpu.VMEM((HALO + TM, D_IN_P), F32)],
        compiler_params=pltpu.CompilerParams(dimension_semantics=("arbitrary",),
                                             vmem_limit_bytes=VMEM_LIMIT),
        name="proj",
    )(h, *[a for a, _ in consts], ck, sk)


def _rwkv_chunk_local(r_t, a_t, b_t, k_t, bh, kh, v, wc, tri_incl, tri_strict, eye):
    n = range(len(r_t))
    half = RWKV_HEAD_DIM
    lhs = [jnp.concatenate([a_t[i][:, :half], r_t[i]], axis=0) for i in n]
    bk = [jnp.concatenate([b_t[i], k_t[i]], axis=0) for i in n]
    abk = [_dot_nt(lhs[i], bk[i]) for i in n]
    top = [jnp.where(tri_strict, abk[i][:CHUNK], 0.0).astype(BF16) for i in n]
    bot = [jnp.where(tri_incl, abk[i][CHUNK:], 0.0).astype(BF16) for i in n]

    zeros = jnp.zeros((CHUNK, LANE), BF16)
    x = [a_t[i].astype(F32) + _dot(top[i], jnp.concatenate([zeros, v[i]], axis=0)) for i in n]
    p = [top[i][:, :half] for i in n]
    for it in range(6):
        x = [x[i] + _dot(p[i], x[i]) for i in n]
        if it < 5:
            p = [_dot(p[i], p[i]).astype(BF16) for i in n]
    xv = [jnp.concatenate([x[i].astype(BF16), v[i]], axis=0) for i in n]

    yloc = [_dot(bot[i], xv[i]) for i in n]
    qp = [r_t[i].astype(F32) + yloc[i][:, :half] for i in n]
    nc = [_dot_tn(jnp.concatenate([bh[i], kh[i]], axis=0), xv[i]) for i in n]
    mc = [jnp.where(eye, wc[i], 0.0) + nc[i][:, :half] for i in n]
    return qp, yloc, mc, nc


def _rwkv_kernel(rt_ref, at_ref, bt_ref, kt_ref, bh_ref, kh_ref, v_ref, wc_ref, gate_ref, bg_ref,
                 gnw_ref, gnb_ref, gnm_ref, y_ref, state_ref):
    step = pl.program_id(0)
    batch = rt_ref.shape[0]

    @pl.when(step == 0)
    def _():
        state_ref[...] = jnp.zeros_like(state_ref)

    ri = lax.broadcasted_iota(jnp.int32, (CHUNK, LANE), 0)
    li = lax.broadcasted_iota(jnp.int32, (CHUNK, LANE), 1)
    ci = li % CHUNK
    tri_incl = ri >= ci
    tri_strict = ri > ci
    eye = (ri == li)[:, :RWKV_HEAD_DIM]
    value_lanes = li >= RWKV_HEAD_DIM
    seqs = [(b, hd) for b in range(batch) for hd in range(RWKV_HEADS)]
    items = [(c, b, hd) for c in range(RWKV_CHUNKS_PER_STEP) for b, hd in seqs]
    n = range(len(items))
    load = lambda ref: [ref[b, hd, pl.ds(c * CHUNK, CHUNK), :] for c, b, hd in items]
    wc = [wc_ref[b, hd, pl.ds(step * RWKV_CHUNKS_PER_STEP + c, 1), :] for c, b, hd in items]
    qp, yloc, mc, nc = _rwkv_chunk_local(load(rt_ref), load(at_ref), load(bt_ref), load(kt_ref),
                                         load(bh_ref), load(kh_ref), load(v_ref), wc,
                                         tri_incl, tri_strict, eye)
    ns = range(len(seqs))
    states = [state_ref[s] for s in ns]
    y = []
    for c in range(RWKV_CHUNKS_PER_STEP):
        o = c * len(seqs)
        y += [_dot(qp[o + s], states[s]) + yloc[o + s] for s in ns]
        states = [jnp.where(value_lanes, _dot(mc[o + s], states[s]) + nc[o + s], 0.0) for s in ns]
    for s in ns:
        state_ref[s] = states[s]
    gnm = gnm_ref[...]
    head = [hd for _, _, hd in items]
    yall = jnp.concatenate(y, axis=0)
    yc = yall - _dot(yall, gnm)
    scale = lax.rsqrt(_dot(yc * yc, gnm) + GN_EPS)
    part = lambda a, i: a[i * CHUNK:(i + 1) * CHUNK]
    gn = [part(yc, i) * part(scale, i) * gnw_ref[head[i]:head[i] + 1, :]
          + gnb_ref[head[i]:head[i] + 1, :] for i in n]
    for c in range(RWKV_CHUNKS_PER_STEP):
        rows = pl.ds(c * CHUNK, CHUNK)
        for b in range(batch):
            o = (c * batch + b) * RWKV_HEADS
            pair = [pltpu.roll(gn[o + hd], RWKV_HEAD_DIM, axis=1) + gn[o + hd + 1] for hd in (0, 2)]
            y_ref[b, rows, :] = (jnp.concatenate(pair, axis=1) * gate_ref[b, rows, :]
                                 + bg_ref[b, rows, :]).astype(BF16)


def _rwkv_call(hm, wc, gate, bg, lw, layer, batch, seq):
    tc = RWKV_CHUNKS_PER_STEP * CHUNK
    hm_spec = lambda a: pl.BlockSpec((batch, RWKV_HEADS, tc, a.shape[-1]), lambda c: (0, 0, c, 0))
    nat = pl.BlockSpec((batch, tc, RWKV_WIDTH), lambda c: (0, c, 0))
    value = np.arange(LANE) >= RWKV_HEAD_DIM
    gnm = jnp.asarray((value[:, None] & value[None, :]) / RWKV_HEAD_DIM, BF16)
    return pl.pallas_call(
        _rwkv_kernel,
        grid=(seq // tc,),
        in_specs=[hm_spec(a) for a in hm] + [_const_spec(wc), nat, nat,
                                             _layer_spec(lw["gnw"], layer), _layer_spec(lw["gnb"], layer),
                                             _const_spec(gnm)],
        out_specs=nat,
        out_shape=jax.ShapeDtypeStruct((batch, seq, RWKV_WIDTH), BF16),
        scratch_shapes=[pltpu.VMEM((batch * RWKV_HEADS, RWKV_HEAD_DIM, LANE), F32)],
        compiler_params=pltpu.CompilerParams(dimension_semantics=("arbitrary",),
                                             vmem_limit_bytes=VMEM_LIMIT),
        name="rwkv",
    )(*hm, wc, gate.reshape(batch, seq, RWKV_WIDTH), bg.reshape(batch, seq, RWKV_WIDTH),
      lw["gnw"], lw["gnb"], gnm)


def _attn_kernel(qt_ref, k_ref, vt_ref, g_ref, o_ref, m_scr, acc_scr,
                 s0_scr, s1_scr, c0_scr, c1_scr, p0_scr, p1_scr, a0_scr, a1_scr):
    tq = s0_scr.shape[1]
    n_q = qt_ref.shape[1]
    sbufs = ((s0_scr, c0_scr), (s1_scr, c1_scr))
    pbufs = ((p0_scr, a0_scr), (p1_scr, a1_scr))
    ones = jnp.ones((BF16_ROWS, tq), BF16)
    tiles = [(i, j) for i in range(n_q) for j in range(i + 1)]

    half = tq // 2

    def masked(s, key0, query0):
        kc = (key0 + lax.broadcasted_iota(jnp.int32, s.shape, 0)) // CHUNK
        qc = (query0 + lax.broadcasted_iota(jnp.int32, s.shape, 1)) // CHUNK
        return jnp.where(kc <= qc, s, -1e30)

    def scores(i, j, slot):
        s_ref, c_ref = sbufs[slot]
        if i == j:
            top = masked(jnp.dot(k_ref[0, j * tq:j * tq + half, :], qt_ref[0, i],
                                 preferred_element_type=F32), 0, 0)
            bot = masked(jnp.dot(k_ref[0, j * tq + half:(j + 1) * tq, :], qt_ref[0, i, :, half:],
                                 preferred_element_type=F32), half, half)
            s_ref[:half, :] = top
            s_ref[half:, half:] = bot
            ctop = jnp.max(top, axis=0, keepdims=True)
            cbot = jnp.max(bot, axis=0, keepdims=True)
            c_ref[...] = jnp.concatenate([ctop[:, :half], jnp.maximum(ctop[:, half:], cbot)], axis=1)
            return
        k = k_ref[0, j * tq:(j + 1) * tq, :]
        s = jnp.dot(k, qt_ref[0, i], preferred_element_type=F32)
        s_ref[...] = s
        c_ref[...] = jnp.max(s, axis=0, keepdims=True)

    def exponentiate(i, j, slot):
        s_ref, c_ref = sbufs[slot]
        p_ref, a_ref = pbufs[slot]
        if j == 0:
            m_new = c_ref[...]
        else:
            m_old = m_scr[...]
            m_new = jnp.maximum(m_old, c_ref[...])
            a_ref[...] = jnp.exp2(m_old - m_new)
        m8 = jnp.broadcast_to(m_new, (8, tq))
        done = []
        for b in range(tq // EXP_ROWS):
            lanes = slice(half if i == j and b * EXP_ROWS >= half else 0, tq)
            width = tq - lanes.start
            mb = m8[:, lanes]
            if b >= EXP_LAG:
                seen = jnp.concatenate([done[b - EXP_LAG]] * (width // LANE), axis=1)
                mb = jnp.where(seen <= 1.0, mb, 0.0)
            rows = pl.ds(b * EXP_ROWS, EXP_ROWS)
            x = s_ref[rows, lanes].reshape(EXP_ROWS // 8, 8, width) - mb[None]
            pb = jnp.exp2(x.reshape(EXP_ROWS, width).astype(BF16))
            p_ref[rows, lanes] = pb
            done.append(pb[0:BF16_ROWS, 0:LANE].astype(F32)[0:8])
        if j < i:
            m_scr[...] = m_new

    def accumulate(i, j, slot):
        p_ref, a_ref = pbufs[slot]
        vt1 = jnp.concatenate([vt_ref[0, j], ones], axis=0)
        if j == i:
            atop = jnp.dot(vt1[:, :half], p_ref[:half, :], preferred_element_type=F32)
            abot = jnp.dot(vt1[:, half:], p_ref[half:, half:], preferred_element_type=F32)
            acc = jnp.concatenate([atop[:, :half], atop[:, half:] + abot], axis=1)
        else:
            acc = jnp.dot(vt1, p_ref[...], preferred_element_type=F32)
        if j > 0:
            acc = a_ref[...] * acc_scr[...] + acc
        if j == i:
            rows = slice(i * tq, (i + 1) * tq)
            o = acc[:V_HEAD_DIM] / acc[V_HEAD_DIM:V_HEAD_DIM + 1]
            o_ref[0, rows, :] = (o.T * g_ref[0, rows, :]).astype(BF16)
        else:
            acc_scr[...] = acc

    scores(*tiles[0], 0)
    scores(*tiles[1], 1)
    exponentiate(*tiles[0], 0)
    for t, (i, j) in enumerate(tiles):
        if t + 2 < len(tiles):
            scores(*tiles[t + 2], t % 2)
        if t + 1 < len(tiles):
            exponentiate(*tiles[t + 1], (t + 1) % 2)
        accumulate(i, j, t % 2)


def _attn_call(qt, k, vt, gate, batch, seq):
    qt = qt.reshape(batch, seq // TQ, MLA_HEADS * QK_PAD, TQ)
    k = k.reshape(batch, seq, MLA_HEADS * QK_PAD)
    vt = vt.reshape(batch, seq // TQ, MLA_WIDTH, TQ)
    gate = gate.reshape(batch, seq, MLA_WIDTH)
    return pl.pallas_call(
        _attn_kernel,
        grid=(batch, MLA_HEADS),
        in_specs=[pl.BlockSpec((1, seq // TQ, QK_PAD, TQ), lambda b, h: (b, 0, h, 0)),
                  pl.BlockSpec((1, seq, QK_PAD), lambda b, h: (b, 0, h)),
                  pl.BlockSpec((1, seq // TQ, V_HEAD_DIM, TQ), lambda b, h: (b, 0, h, 0)),
                  pl.BlockSpec((1, seq, V_HEAD_DIM), lambda b, h: (b, 0, h))],
        out_specs=pl.BlockSpec((1, seq, V_HEAD_DIM), lambda b, h: (b, 0, h)),
        out_shape=jax.ShapeDtypeStruct((batch, seq, MLA_WIDTH), BF16),
        scratch_shapes=[pltpu.VMEM((1, TQ), F32),
                        pltpu.VMEM((V_HEAD_DIM + BF16_ROWS, TQ), F32),
                        pltpu.VMEM((TQ, TQ), F32), pltpu.VMEM((TQ, TQ), F32),
                        pltpu.VMEM((1, TQ), F32), pltpu.VMEM((1, TQ), F32),
                        pltpu.VMEM((TQ, TQ), BF16), pltpu.VMEM((TQ, TQ), BF16),
                        pltpu.VMEM((1, TQ), F32), pltpu.VMEM((1, TQ), F32)],
        compiler_params=pltpu.CompilerParams(
            dimension_semantics=("arbitrary", "arbitrary"), vmem_limit_bytes=VMEM_LIMIT),
        name="attn",
    )(qt, k, vt, gate)


def _out_kernel(h_ref, yc_ref, yr_ref, ym_ref, wo_ref, p_ref, wple_ref, png_ref, wpg_ref, fg_ref,
                o_ref, *, final):
    tm = h_ref.shape[0]
    blocks = [pl.ds(r, tm // OUT_ROW_BLOCKS) for r in range(0, tm, tm // OUT_ROW_BLOCKS)]
    ycat = [jnp.concatenate([yc_ref[b, :], yr_ref[b, :], ym_ref[b, :]], axis=-1) for b in blocks]
    h = [h_ref[b, :] + jnp.dot(y, wo_ref[...], preferred_element_type=F32) for b, y in zip(blocks, ycat)]
    hn = [_rms(x, png_ref[...]).astype(BF16) for x in h]
    gate = [jax.nn.sigmoid(jnp.dot(x, wpg_ref[...], preferred_element_type=F32)) for x in hn]
    ple = [jnp.dot(p_ref[b, :].astype(BF16), wple_ref[...], preferred_element_type=F32) for b in blocks]
    h = [x + e * g for x, e, g in zip(h, ple, gate)]
    if final:
        h = [_rms(x, fg_ref[...]) for x in h]
    for b, x in zip(blocks, h):
        o_ref[b, :] = x


def _out_call(h, yc, yr, ym, p, layer, lw, final_g, final):
    t = h.shape[0]
    row = lambda w: pl.BlockSpec((TM, w), lambda i: (i, 0))
    return pl.pallas_call(
        functools.partial(_out_kernel, final=final),
        grid=(t // TM,),
        in_specs=[row(D_MODEL), row(CONV_WIDTH), row(RWKV_WIDTH), row(MLA_WIDTH),
                  _layer_spec(lw["wo"], layer),
                  pl.BlockSpec((None, TM, PLE_DIM), lambda i: (layer, i, 0)),
                  _layer_spec(lw["wple"], layer), _layer_spec(lw["png"], layer),
                  _layer_spec(lw["wpg"], layer), _const_spec(final_g)],
        out_specs=row(D_MODEL),
        out_shape=jax.ShapeDtypeStruct((t, D_MODEL), F32),
        compiler_params=pltpu.CompilerParams(dimension_semantics=("arbitrary",),
                                             vmem_limit_bytes=VMEM_LIMIT),
        name="out",
    )(h, yc, yr, ym, lw["wo"], p, lw["wple"], lw["png"], lw["wpg"], final_g)


def _prep(norm_mix_g, w_in, conv_w, rwkv_mu, rwkv_w0, rwkv_w2, rwkv_a0, rwkv_a2, rwkv_kk, rwkv_ka, rwkv_rk,
          rwkv_gn_w, rwkv_gn_b, mla_q_norm_g, mla_w_qb, mla_kv_norm_g, mla_w_kvb, w_out, ple_w, ple_norm_g,
          ple_gate_w):
    depth = w_in.shape[0]
    o_kr = C_MG
    o_mg = o_kr + QK_ROPE_DIM
    zeros = jnp.zeros((depth, D_MODEL, QK_ROPE_DIM), F32)
    win_a = w_in.astype(BF16)
    win_b = jnp.concatenate([w_in[:, :, o_mg:], w_in[:, :, o_kr:o_mg], zeros], axis=2).astype(BF16)

    zl = jnp.zeros((depth, LORA, RWKV_WIDTH), F32)
    wl = jnp.concatenate([jnp.concatenate([rwkv_w2, zl], axis=2),
                          jnp.concatenate([zl, rwkv_a2], axis=2)], axis=1).astype(BF16)

    wqb = mla_w_qb.reshape(depth, Q_LORA_RANK, MLA_HEADS, QK_NOPE_DIM + QK_ROPE_DIM)
    zq = jnp.zeros((depth, Q_LORA_RANK, MLA_HEADS, QK_ROPE_DIM), F32)
    wq = jnp.concatenate([wqb, zq], axis=-1).reshape(depth, Q_LORA_RANK, MLA_HEADS * QK_PAD).astype(BF16)
    wkvb = mla_w_kvb.reshape(depth, KV_LORA_RANK, MLA_HEADS, QK_NOPE_DIM + V_HEAD_DIM)
    wkv = jnp.concatenate([wkvb[..., :QK_NOPE_DIM].reshape(depth, KV_LORA_RANK, -1),
                           wkvb[..., QK_NOPE_DIM:].reshape(depth, KV_LORA_RANK, -1)], axis=2).astype(BF16)
    row = lambda x: x.reshape(depth, 1, -1)
    hd = lambda x: jnp.pad(x.reshape(depth, RWKV_HEADS, RWKV_HEAD_DIM),
                           ((0, 0), (0, 0), (LANE - RWKV_HEAD_DIM, 0)))
    return dict(
        ng=row(norm_mix_g), win_a=win_a, win_b=win_b, cw=conv_w,
        mu_rkv=row(rwkv_mu[:, :3 * RWKV_WIDTH]), mu_wa=row(rwkv_mu[:, 3 * RWKV_WIDTH:]),
        w0=row(rwkv_w0), a0=row(rwkv_a0), wl=wl, kk=row(rwkv_kk), ka=row(rwkv_ka),
        qg=row(mla_q_norm_g), wq=wq, kvg=row(mla_kv_norm_g), wkv=wkv,
        rk=row(rwkv_rk), gnw=hd(rwkv_gn_w), gnb=hd(rwkv_gn_b),
        wo=w_out.astype(BF16), wple=ple_w.astype(BF16), png=row(ple_norm_g), wpg=ple_gate_w.astype(BF16))


def kernel(x, p, positions, norm_mix_g, w_in, conv_w, rwkv_mu, rwkv_w0, rwkv_w2, rwkv_a0, rwkv_a2,
           rwkv_kk, rwkv_ka, rwkv_rk, rwkv_gn_w, rwkv_gn_b, mla_q_norm_g, mla_w_qb, mla_kv_norm_g,
           mla_w_kvb, w_out, ple_w, ple_norm_g, ple_gate_w, final_norm_g):
    batch, seq, _ = x.shape
    depth = w_in.shape[0]
    t = batch * seq

    half = QK_ROPE_DIM // 2
    inv_freq = 1.0 / (ROPE_THETA ** (jnp.arange(0, QK_ROPE_DIM, 2, dtype=F32) / QK_ROPE_DIM))
    inv_tile = jnp.concatenate([inv_freq, inv_freq, jnp.zeros((LANE - QK_ROPE_DIM,), F32)])
    lane = np.arange(LANE)
    ang = positions.astype(F32).reshape(t, 1) * inv_tile
    ck = jnp.where(lane < QK_ROPE_DIM, jnp.cos(ang), 0.0)
    sk = jnp.sin(ang) * jnp.asarray(np.where(lane < half, -1.0, 1.0), F32)

    h = x.reshape(t, D_MODEL)
    p = p.reshape(depth, t, PLE_DIM)
    final_g = final_norm_g.reshape(1, D_MODEL)
    lw = _prep(norm_mix_g, w_in, conv_w, rwkv_mu, rwkv_w0, rwkv_w2, rwkv_a0, rwkv_a2, rwkv_kk, rwkv_ka,
               rwkv_rk, rwkv_gn_w, rwkv_gn_b, mla_q_norm_g, mla_w_qb, mla_kv_norm_g, mla_w_kvb, w_out,
               ple_w, ple_norm_g, ple_gate_w)
    for i in range(depth):
        (yconv, *hm, wc, gate_r, bg, q, kq, vq, gate_m) = _proj_call(h, lw, i, ck, sk, batch, seq)
        y_rwkv = _rwkv_call(hm, wc, gate_r, bg, lw, i, batch, seq).reshape(t, RWKV_WIDTH)
        y_mla = _attn_call(q, kq, vq, gate_m, batch, seq).reshape(t, MLA_WIDTH)
        h = _out_call(h, yconv, y_rwkv, y_mla, p, i, lw, final_g, final=(i == depth - 1))
    return h.reshape(batch, seq, D_MODEL)
```

```python
import functools
import math

import jax
import jax.numpy as jnp
import numpy as np
from jax import lax
from jax.experimental import pallas as pl
from jax.experimental.pallas import tpu as pltpu

F32 = jnp.float32
BF16 = jnp.bfloat16

D_MODEL = 1024
CHUNK = 64
PLE_DIM = 256
NORM_EPS = 1e-6
CONV_WIDTH = 256
RWKV_HEADS = 4
RWKV_HEAD_DIM = 64
RWKV_WIDTH = RWKV_HEADS * RWKV_HEAD_DIM
LORA = 64
DECAY_SCALE = math.exp(-0.5)
GN_EPS = 64e-5
MLA_HEADS = 4
QK_NOPE_DIM = 128
QK_ROPE_DIM = 64
V_HEAD_DIM = 128
Q_LORA_RANK = 384
KV_LORA_RANK = 256
MLA_WIDTH = MLA_HEADS * V_HEAD_DIM
ROPE_THETA = 10000.0
D_MIX = CONV_WIDTH + RWKV_WIDTH + MLA_WIDTH

C_CONV = 0
C_RKV = 1024
C_WA = 1792
C_RG = 1920
C_QA = 2176
C_KVA = 2560
C_MG = 2816
C_KR = 3328
D_IN_P = 3456
QK_PAD = 256

LANE = 128
BF16_ROWS = 16
HALO = BF16_ROWS
TM = 512
OUT_ROW_BLOCKS = 2
TQ = 512
EXP_ROWS = 128
EXP_LAG = 1
RWKV_CHUNKS_PER_STEP = 4
VMEM_LIMIT = 56 * 1024 * 1024


def _dot(a, b):
    return jnp.dot(a.astype(BF16), b.astype(BF16), preferred_element_type=F32)


def _dot_nt(a, b):
    return lax.dot_general(a.astype(BF16), b.astype(BF16), (((1,), (1,)), ((), ())),
                           preferred_element_type=F32)


def _dot_tn(a, b):
    return lax.dot_general(a.astype(BF16), b.astype(BF16), (((0,), (0,)), ((), ())),
                           preferred_element_type=F32)


def _split2(x):
    hi = x.astype(BF16)
    return hi, (x - hi.astype(F32)).astype(BF16)


def _dot_mask(x, m01, left=False):
    hi, lo = _split2(x)
    if left:
        return jnp.dot(m01, hi, preferred_element_type=F32) + jnp.dot(m01, lo, preferred_element_type=F32)
    return jnp.dot(hi, m01, preferred_element_type=F32) + jnp.dot(lo, m01, preferred_element_type=F32)


def _chunk_cumsum(x):
    pos = lax.broadcasted_iota(jnp.int32, x.shape, 0) % CHUNK
    d = 1
    while d < CHUNK:
        x = x + jnp.where(pos >= d, pltpu.roll(x, d, axis=0), 0.0)
        d *= 2
    return x


def _swap_rope_halves(x):
    half = QK_ROPE_DIM // 2
    return pltpu.roll(x, half, axis=1) + pltpu.roll(x, LANE - half, axis=1)


def _rms(x, g):
    return x * lax.rsqrt(jnp.mean(x * x, axis=-1, keepdims=True) + NORM_EPS) * g


def _silu(x):
    return x * jax.nn.sigmoid(x)


def _proj_kernel(h_ref, ng_ref, wa_ref, wb_ref, cwt_ref, mu_rkv_ref, mu_wa_ref, w0_ref, a0_ref,
                 wl_ref, kk_ref, ka_ref, rk_ref, seg_ref, csum_ref, qg_ref, wq_ref, kvg_ref, wkv_ref,
                 ck_ref, sk_ref,
                 yconv_ref, rt_ref, at_ref, bt_ref, kt_ref, bh_ref, kh_ref, v_ref, wc_ref, gr_ref, bg_ref,
                 q_ref, kq_ref, vq_ref, gm_ref, z_scr, *, tiles_per_seq):
    hm_refs = (rt_ref, at_ref, bt_ref, kt_ref, bh_ref, kh_ref, v_ref)
    tm = h_ref.shape[0]
    tile = pl.ds(HALO, tm)

    @pl.when(pl.program_id(0) == 0)
    def _():
        z_scr[0:HALO, :] = jnp.zeros((HALO, D_IN_P), F32)

    u = _rms(h_ref[...], ng_ref[...]).astype(BF16)
    z_scr[tile, C_RKV:C_MG] = jnp.dot(u, wa_ref[:, C_RKV:C_MG], preferred_element_type=F32)
    z_scr[tile, C_MG:] = jnp.dot(u, wb_ref[...], preferred_element_type=F32)

    first = pl.program_id(0) % tiles_per_seq == 0
    row = lax.broadcasted_iota(jnp.int32, (tm, 1), 0)

    def cols(c, w, back=0):
        x = z_scr[pl.ds(HALO - back, tm), c:c + w]
        return jnp.where(row >= jnp.where(first, back, 0), x, 0.0) if back else x

    cur = cols(C_RKV, 768)
    rkv = cur + (cols(C_RKV, 768, 1) - cur) * mu_rkv_ref[...]
    cur = cols(C_WA, 128)
    wa = cur + (cols(C_WA, 128, 1) - cur) * mu_wa_ref[...]
    r = rkv[:, 0:256]
    k = rkv[:, 256:512]
    v = rkv[:, 512:768]
    lane = lax.broadcasted_iota(jnp.int32, wa.shape, 1)
    lora_in = jnp.where(lane < LORA, jnp.tanh(wa), wa)
    lora = jnp.dot(lora_in.astype(BF16), wl_ref[...], preferred_element_type=F32)
    wlog = -DECAY_SCALE * jax.nn.sigmoid(w0_ref[...] + lora[:, :RWKV_WIDTH])
    a = jax.nn.sigmoid(a0_ref[...] + lora[:, RWKV_WIDTH:])
    kk = k * kk_ref[...]
    kk = kk * lax.rsqrt(_dot_mask(kk * kk, seg_ref[...]) + 1e-12)
    qn = _rms(cols(C_QA, Q_LORA_RANK), qg_ref[...]).astype(BF16)
    qm = jnp.dot(qn, wq_ref[...], preferred_element_type=F32)
    kvn = _rms(cols(C_KVA, KV_LORA_RANK), kvg_ref[...]).astype(BF16)
    kv = jnp.dot(kvn, wkv_ref[...], preferred_element_type=F32)
    kmod = k * (1.0 + (a - 1.0) * ka_ref[...])
    zb = kk * a
    cw = _chunk_cumsum(wlog)
    total = _dot_mask(wlog, csum_ref[...], left=True)
    nchunk = tm // CHUNK
    rest = jnp.broadcast_to(total[:, None, :], (nchunk, CHUNK, RWKV_WIDTH)).reshape(tm, RWKV_WIDTH) - cw
    inv = jnp.exp(-cw)
    to_end = jnp.exp(rest)
    wc = jnp.exp(total)
    ops = (r * jnp.exp(cw), -kk * jnp.exp(cw - wlog), zb * inv, kmod * inv, zb * to_end, kmod * to_end, v)
    zpad = jnp.zeros((tm, LANE - RWKV_HEAD_DIM), BF16)
    for hd in range(RWKV_HEADS):
        sl = slice(hd * RWKV_HEAD_DIM, (hd + 1) * RWKV_HEAD_DIM)
        for ref, val in zip(hm_refs, ops):
            piece = val[:, sl].astype(BF16)
            if ref is at_ref:
                piece = jnp.concatenate([piece, zpad], axis=1)
            elif ref is v_ref:
                piece = jnp.concatenate([zpad, piece], axis=1)
            ref[0, hd] = piece
        wc_ref[0, hd] = wc[:, sl]
    gate_r = _silu(cols(C_RG, 256))
    bonus = _dot_mask(r * kmod * rk_ref[...], seg_ref[...]) * v
    gr_ref[...] = gate_r
    bg_ref[...] = bonus * gate_r

    ck = ck_ref[...]
    sk = sk_ref[...]
    scale = math.log2(math.e) / math.sqrt(QK_NOPE_DIM + QK_ROPE_DIM)
    kx = cols(C_KR, LANE)
    kr = kx * ck + _swap_rope_halves(kx) * sk
    for hd in range(MLA_HEADS):
        o = hd * QK_PAD
        qx = qm[:, o + LANE:o + QK_PAD]
        qr = qx * ck + _swap_rope_halves(qx) * sk
        q_ref[0, o:o + LANE, :] = (qm[:, o:o + LANE] * scale).T.astype(BF16)
        q_ref[0, o + LANE:o + QK_PAD, :] = (qr * scale).T.astype(BF16)
        kq_ref[:, o:o + LANE] = kv[:, hd * LANE:(hd + 1) * LANE].astype(BF16)
        kq_ref[:, o + LANE:o + QK_PAD] = kr.astype(BF16)
    vq_ref[0] = kv[:, MLA_HEADS * QK_NOPE_DIM:].T.astype(BF16)
    gm_ref[...] = _silu(cols(C_MG, MLA_WIDTH))

    z_scr[tile, :C_RKV] = jnp.dot(u, wa_ref[:, :C_RKV], preferred_element_type=F32)
    cwt = cwt_ref[...]
    conv = sum(cols(C_CONV + 256, 256, back) * cols(C_CONV + 512, 256, back) * cwt[2 - back:3 - back, :]
               for back in range(3))
    yconv_ref[...] = (cols(C_CONV, 256) * conv * _silu(cols(C_CONV + 768, 256))).astype(BF16)

    shifted = slice(C_CONV + 256, C_RG)
    z_scr[0:HALO, shifted] = z_scr[pl.ds(tm, HALO), shifted]


def _const_spec(arr):
    return pl.BlockSpec(arr.shape, lambda *_: (0,) * arr.ndim)


def _layer_spec(arr, layer):
    zeros = (0,) * (arr.ndim - 1)
    return pl.BlockSpec((None,) + arr.shape[1:], lambda *_: (layer,) + zeros)


def _proj_call(h, lw, layer, ck, sk, batch, seq):
    t = h.shape[0]
    tiles_per_seq = seq // TM
    row = lambda w: pl.BlockSpec((TM, w), lambda i: (i, 0))
    chunk_id = np.arange(TM) // CHUNK
    csum = jnp.asarray(np.arange(TM // CHUNK)[:, None] == chunk_id[None, :], BF16)
    head = np.arange(RWKV_WIDTH) // RWKV_HEAD_DIM
    seg = jnp.asarray(head[:, None] == head[None, :], BF16)
    per_layer = lambda *names: [(lw[n], _layer_spec(lw[n], layer)) for n in names]
    shared = lambda *arrs: [(a, _const_spec(a)) for a in arrs]
    consts = (per_layer("ng", "win_a", "win_b", "cw", "mu_rkv", "mu_wa", "w0", "a0", "wl", "kk", "ka", "rk")
              + shared(seg, csum) + per_layer("qg", "wq", "kvg", "wkv"))
    hm_widths = [RWKV_HEAD_DIM, LANE] + [RWKV_HEAD_DIM] * 4 + [LANE]
    hm = [pl.BlockSpec((1, RWKV_HEADS, TM, w), lambda i: (i // tiles_per_seq, 0, i % tiles_per_seq, 0))
          for w in hm_widths]
    hm_shapes = [jax.ShapeDtypeStruct((batch, RWKV_HEADS, seq, w), BF16) for w in hm_widths]
    wc_spec = pl.BlockSpec((1, RWKV_HEADS, TM // CHUNK, RWKV_HEAD_DIM),
                           lambda i: (i // tiles_per_seq, 0, i % tiles_per_seq, 0))
    out_shape = [jax.ShapeDtypeStruct((t, CONV_WIDTH), BF16)] + hm_shapes + [
        jax.ShapeDtypeStruct((batch, RWKV_HEADS, seq // CHUNK, RWKV_HEAD_DIM), F32),
        jax.ShapeDtypeStruct((t, RWKV_WIDTH), F32),
        jax.ShapeDtypeStruct((t, RWKV_WIDTH), F32),
        jax.ShapeDtypeStruct((t // TM, MLA_HEADS * QK_PAD, TM), BF16),
        jax.ShapeDtypeStruct((t, MLA_HEADS * QK_PAD), BF16),
        jax.ShapeDtypeStruct((t // TM, MLA_WIDTH, TM), BF16),
        jax.ShapeDtypeStruct((t, MLA_WIDTH), F32),
    ]
    vt_spec = pl.BlockSpec((1, MLA_WIDTH, TM), lambda i: (i, 0, 0))
    qt_spec = pl.BlockSpec((1, MLA_HEADS * QK_PAD, TM), lambda i: (i, 0, 0))
    out_specs = [row(CONV_WIDTH)] + hm + [wc_spec, row(RWKV_WIDTH), row(RWKV_WIDTH),
                                                  qt_spec, row(MLA_HEADS * QK_PAD),
                                                  vt_spec, row(MLA_WIDTH)]
    return pl.pallas_call(
        functools.partial(_proj_kernel, tiles_per_seq=tiles_per_seq),
        grid=(t // TM,),
        in_specs=[row(D_MODEL)] + [s for _, s in consts] + [row(LANE), row(LANE)],
        out_specs=out_specs,
        out_shape=out_shape,
        scratch_shapes=[pltpu.VMEM((HALO + TM, D_IN_P), F32)],
        compiler_params=pltpu.CompilerParams(dimension_semantics=("arbitrary",),
                                             vmem_limit_bytes=VMEM_LIMIT),
        name="proj",
    )(h, *[a for a, _ in consts], ck, sk)


def _rwkv_chunk_local(r_t, a_t, b_t, k_t, bh, kh, v, wc, tri_incl, tri_strict, eye):
    n = range(len(r_t))
    half = RWKV_HEAD_DIM
    lhs = [jnp.concatenate([a_t[i][:, :half], r_t[i]], axis=0) for i in n]
    bk = [jnp.concatenate([b_t[i], k_t[i]], axis=0) for i in n]
    abk = [_dot_nt(lhs[i], bk[i]) for i in n]
    top = [jnp.where(tri_strict, abk[i][:CHUNK], 0.0).astype(BF16) for i in n]
    bot = [jnp.where(tri_incl, abk[i][CHUNK:], 0.0).astype(BF16) for i in n]

    zeros = jnp.zeros((CHUNK, LANE), BF16)
    x = [a_t[i].astype(F32) + _dot(top[i], jnp.concatenate([zeros, v[i]], axis=0)) for i in n]
    p = [top[i][:, :half] for i in n]
    for it in range(6):
        x = [x[i] + _dot(p[i], x[i]) for i in n]
        if it < 5:
            p = [_dot(p[i], p[i]).astype(BF16) for i in n]
    xv = [jnp.concatenate([x[i].astype(BF16), v[i]], axis=0) for i in n]

    yloc = [_dot(bot[i], xv[i]) for i in n]
    qp = [r_t[i].astype(F32) + yloc[i][:, :half] for i in n]
    nc = [_dot_tn(jnp.concatenate([bh[i], kh[i]], axis=0), xv[i]) for i in n]
    mc = [jnp.where(eye, wc[i], 0.0) + nc[i][:, :half] for i in n]
    return qp, yloc, mc, nc


def _rwkv_kernel(rt_ref, at_ref, bt_ref, kt_ref, bh_ref, kh_ref, v_ref, wc_ref, gate_ref, bg_ref,
                 gnw_ref, gnb_ref, gnm_ref, y_ref, state_ref):
    step = pl.program_id(0)
    batch = rt_ref.shape[0]

    @pl.when(step == 0)
    def _():
        state_ref[...] = jnp.zeros_like(state_ref)

    ri = lax.broadcasted_iota(jnp.int32, (CHUNK, LANE), 0)
    li = lax.broadcasted_iota(jnp.int32, (CHUNK, LANE), 1)
    ci = li % CHUNK
    tri_incl = ri >= ci
    tri_strict = ri > ci
    eye = (ri == li)[:, :RWKV_HEAD_DIM]
    value_lanes = li >= RWKV_HEAD_DIM
    seqs = [(b, hd) for b in range(batch) for hd in range(RWKV_HEADS)]
    items = [(c, b, hd) for c in range(RWKV_CHUNKS_PER_STEP) for b, hd in seqs]
    n = range(len(items))
    load = lambda ref: [ref[b, hd, pl.ds(c * CHUNK, CHUNK), :] for c, b, hd in items]
    wc = [wc_ref[b, hd, pl.ds(step * RWKV_CHUNKS_PER_STEP + c, 1), :] for c, b, hd in items]
    qp, yloc, mc, nc = _rwkv_chunk_local(load(rt_ref), load(at_ref), load(bt_ref), load(kt_ref),
                                         load(bh_ref), load(kh_ref), load(v_ref), wc,
                                         tri_incl, tri_strict, eye)
    ns = range(len(seqs))
    states = [state_ref[s] for s in ns]
    y = []
    for c in range(RWKV_CHUNKS_PER_STEP):
        o = c * len(seqs)
        y += [_dot(qp[o + s], states[s]) + yloc[o + s] for s in ns]
        states = [jnp.where(value_lanes, _dot(mc[o + s], states[s]) + nc[o + s], 0.0) for s in ns]
    for s in ns:
        state_ref[s] = states[s]
    gnm = gnm_ref[...]
    head = [hd for _, _, hd in items]
    yall = jnp.concatenate(y, axis=0)
    yc = yall - _dot(yall, gnm)
    scale = lax.rsqrt(_dot(yc * yc, gnm) + GN_EPS)
    part = lambda a, i: a[i * CHUNK:(i + 1) * CHUNK]
    gn = [part(yc, i) * part(scale, i) * gnw_ref[head[i]:head[i] + 1, :]
          + gnb_ref[head[i]:head[i] + 1, :] for i in n]
    for c in range(RWKV_CHUNKS_PER_STEP):
        rows = pl.ds(c * CHUNK, CHUNK)
        for b in range(batch):
            o = (c * batch + b) * RWKV_HEADS
            pair = [pltpu.roll(gn[o + hd], RWKV_HEAD_DIM, axis=1) + gn[o + hd + 1] for hd in (0, 2)]
            y_ref[b, rows, :] = (jnp.concatenate(pair, axis=1) * gate_ref[b, rows, :]
                                 + bg_ref[b, rows, :]).astype(BF16)


def _rwkv_call(hm, wc, gate, bg, lw, layer, batch, seq):
    tc = RWKV_CHUNKS_PER_STEP * CHUNK
    hm_spec = lambda a: pl.BlockSpec((batch, RWKV_HEADS, tc, a.shape[-1]), lambda c: (0, 0, c, 0))
    nat = pl.BlockSpec((batch, tc, RWKV_WIDTH), lambda c: (0, c, 0))
    value = np.arange(LANE) >= RWKV_HEAD_DIM
    gnm = jnp.asarray((value[:, None] & value[None, :]) / RWKV_HEAD_DIM, BF16)
    return pl.pallas_call(
        _rwkv_kernel,
        grid=(seq // tc,),
        in_specs=[hm_spec(a) for a in hm] + [_const_spec(wc), nat, nat,
                                             _layer_spec(lw["gnw"], layer), _layer_spec(lw["gnb"], layer),
                                             _const_spec(gnm)],
        out_specs=nat,
        out_shape=jax.ShapeDtypeStruct((batch, seq, RWKV_WIDTH), BF16),
        scratch_shapes=[pltpu.VMEM((batch * RWKV_HEADS, RWKV_HEAD_DIM, LANE), F32)],
        compiler_params=pltpu.CompilerParams(dimension_semantics=("arbitrary",),
                                             vmem_limit_bytes=VMEM_LIMIT),
        name="rwkv",
    )(*hm, wc, gate.reshape(batch, seq, RWKV_WIDTH), bg.reshape(batch, seq, RWKV_WIDTH),
      lw["gnw"], lw["gnb"], gnm)


def _attn_kernel(qt_ref, k_ref, vt_ref, g_ref, o_ref, m_scr, acc_scr,
                 s0_scr, s1_scr, c0_scr, c1_scr, p0_scr, p1_scr, a0_scr, a1_scr):
    tq = s0_scr.shape[1]
    n_q = qt_ref.shape[1]
    sbufs = ((s0_scr, c0_scr), (s1_scr, c1_scr))
    pbufs = ((p0_scr, a0_scr), (p1_scr, a1_scr))
    ones = jnp.ones((BF16_ROWS, tq), BF16)
    tiles = [(i, j) for i in range(n_q) for j in range(i + 1)]

    half = tq // 2

    def masked(s, key0, query0):
        kc = (key0 + lax.broadcasted_iota(jnp.int32, s.shape, 0)) // CHUNK
        qc = (query0 + lax.broadcasted_iota(jnp.int32, s.shape, 1)) // CHUNK
        return jnp.where(kc <= qc, s, -1e30)

    def scores(i, j, slot):
        s_ref, c_ref = sbufs[slot]
        if i == j:
            top = masked(jnp.dot(k_ref[0, j * tq:j * tq + half, :], qt_ref[0, i],
                                 preferred_element_type=F32), 0, 0)
            bot = masked(jnp.dot(k_ref[0, j * tq + half:(j + 1) * tq, :], qt_ref[0, i, :, half:],
                                 preferred_element_type=F32), half, half)
            s_ref[:half, :] = top
            s_ref[half:, half:] = bot
            ctop = jnp.max(top, axis=0, keepdims=True)
            cbot = jnp.max(bot, axis=0, keepdims=True)
            c_ref[...] = jnp.concatenate([ctop[:, :half], jnp.maximum(ctop[:, half:], cbot)], axis=1)
            return
        k = k_ref[0, j * tq:(j + 1) * tq, :]
        s = jnp.dot(k, qt_ref[0, i], preferred_element_type=F32)
        s_ref[...] = s
        c_ref[...] = jnp.max(s, axis=0, keepdims=True)

    def exponentiate(i, j, slot):
        s_ref, c_ref = sbufs[slot]
        p_ref, a_ref = pbufs[slot]
        if j == 0:
            m_new = c_ref[...]
        else:
            m_old = m_scr[...]
            m_new = jnp.maximum(m_old, c_ref[...])
            a_ref[...] = jnp.exp2(m_old - m_new)
        m8 = jnp.broadcast_to(m_new, (8, tq))
        done = []
        for b in range(tq // EXP_ROWS):
            lanes = slice(half if i == j and b * EXP_ROWS >= half else 0, tq)
            width = tq - lanes.start
            mb = m8[:, lanes]
            if b >= EXP_LAG:
                seen = jnp.concatenate([done[b - EXP_LAG]] * (width // LANE), axis=1)
                mb = jnp.where(seen <= 1.0, mb, 0.0)
            rows = pl.ds(b * EXP_ROWS, EXP_ROWS)
            x = s_ref[rows, lanes].reshape(EXP_ROWS // 8, 8, width) - mb[None]
            pb = jnp.exp2(x.reshape(EXP_ROWS, width).astype(BF16))
            p_ref[rows, lanes] = pb
            done.append(pb[0:BF16_ROWS, 0:LANE].astype(F32)[0:8])
        if j < i:
            m_scr[...] = m_new

    def accumulate(i, j, slot):
        p_ref, a_ref = pbufs[slot]
        vt1 = jnp.concatenate([vt_ref[0, j], ones], axis=0)
        if j == i:
            atop = jnp.dot(vt1[:, :half], p_ref[:half, :], preferred_element_type=F32)
            abot = jnp.dot(vt1[:, half:], p_ref[half:, half:], preferred_element_type=F32)
            acc = jnp.concatenate([atop[:, :half], atop[:, half:] + abot], axis=1)
        else:
            acc = jnp.dot(vt1, p_ref[...], preferred_element_type=F32)
        if j > 0:
            acc = a_ref[...] * acc_scr[...] + acc
        if j == i:
            rows = slice(i * tq, (i + 1) * tq)
            o = acc[:V_HEAD_DIM] / acc[V_HEAD_DIM:V_HEAD_DIM + 1]
            o_ref[0, rows, :] = (o.T * g_ref[0, rows, :]).astype(BF16)
        else:
            acc_scr[...] = acc

    scores(*tiles[0], 0)
    scores(*tiles[1], 1)
    exponentiate(*tiles[0], 0)
    for t, (i, j) in enumerate(tiles):
        if t + 2 < len(tiles):
            scores(*tiles[t + 2], t % 2)
        if t + 1 < len(tiles):
            exponentiate(*tiles[t + 1], (t + 1) % 2)
        accumulate(i, j, t % 2)


def _attn_call(qt, k, vt, gate, batch, seq):
    qt = qt.reshape(batch, seq // TQ, MLA_HEADS * QK_PAD, TQ)
    k = k.reshape(batch, seq, MLA_HEADS * QK_PAD)
    vt = vt.reshape(batch, seq // TQ, MLA_WIDTH, TQ)
    gate = gate.reshape(batch, seq, MLA_WIDTH)
    return pl.pallas_call(
        _attn_kernel,
        grid=(batch, MLA_HEADS),
        in_specs=[pl.BlockSpec((1, seq // TQ, QK_PAD, TQ), lambda b, h: (b, 0, h, 0)),
                  pl.BlockSpec((1, seq, QK_PAD), lambda b, h: (b, 0, h)),
                  pl.BlockSpec((1, seq // TQ, V_HEAD_DIM, TQ), lambda b, h: (b, 0, h, 0)),
                  pl.BlockSpec((1, seq, V_HEAD_DIM), lambda b, h: (b, 0, h))],
        out_specs=pl.BlockSpec((1, seq, V_HEAD_DIM), lambda b, h: (b, 0, h)),
        out_shape=jax.ShapeDtypeStruct((batch, seq, MLA_WIDTH), BF16),
        scratch_shapes=[pltpu.VMEM((1, TQ), F32),
                        pltpu.VMEM((V_HEAD_DIM + BF16_ROWS, TQ), F32),
                        pltpu.VMEM((TQ, TQ), F32), pltpu.VMEM((TQ, TQ), F32),
                        pltpu.VMEM((1, TQ), F32), pltpu.VMEM((1, TQ), F32),
                        pltpu.VMEM((TQ, TQ), BF16), pltpu.VMEM((TQ, TQ), BF16),
                        pltpu.VMEM((1, TQ), F32), pltpu.VMEM((1, TQ), F32)],
        compiler_params=pltpu.CompilerParams(
            dimension_semantics=("arbitrary", "arbitrary"), vmem_limit_bytes=VMEM_LIMIT),
        name="attn",
    )(qt, k, vt, gate)


def _out_kernel(h_ref, yc_ref, yr_ref, ym_ref, wo_ref, p_ref, wple_ref, png_ref, wpg_ref, fg_ref,
                o_ref, *, final):
    tm = h_ref.shape[0]
    blocks = [pl.ds(r, tm // OUT_ROW_BLOCKS) for r in range(0, tm, tm // OUT_ROW_BLOCKS)]
    ycat = [jnp.concatenate([yc_ref[b, :], yr_ref[b, :], ym_ref[b, :]], axis=-1) for b in blocks]
    h = [h_ref[b, :] + jnp.dot(y, wo_ref[...], preferred_element_type=F32) for b, y in zip(blocks, ycat)]
    hn = [_rms(x, png_ref[...]).astype(BF16) for x in h]
    gate = [jax.nn.sigmoid(jnp.dot(x, wpg_ref[...], preferred_element_type=F32)) for x in hn]
    ple = [jnp.dot(p_ref[b, :].astype(BF16), wple_ref[...], preferred_element_type=F32) for b in blocks]
    h = [x + e * g for x, e, g in zip(h, ple, gate)]
    if final:
        h = [_rms(x, fg_ref[...]) for x in h]
    for b, x in zip(blocks, h):
        o_ref[b, :] = x


def _out_call(h, yc, yr, ym, p, layer, lw, final_g, final):
    t = h.shape[0]
    row = lambda w: pl.BlockSpec((TM, w), lambda i: (i, 0))
    return pl.pallas_call(
        functools.partial(_out_kernel, final=final),
        grid=(t // TM,),
        in_specs=[row(D_MODEL), row(CONV_WIDTH), row(RWKV_WIDTH), row(MLA_WIDTH),
                  _layer_spec(lw["wo"], layer),
                  pl.BlockSpec((None, TM, PLE_DIM), lambda i: (layer, i, 0)),
                  _layer_spec(lw["wple"], layer), _layer_spec(lw["png"], layer),
                  _layer_spec(lw["wpg"], layer), _const_spec(final_g)],
        out_specs=row(D_MODEL),
        out_shape=jax.ShapeDtypeStruct((t, D_MODEL), F32),
        compiler_params=pltpu.CompilerParams(dimension_semantics=("arbitrary",),
                                             vmem_limit_bytes=VMEM_LIMIT),
        name="out",
    )(h, yc, yr, ym, lw["wo"], p, lw["wple"], lw["png"], lw["wpg"], final_g)


def _prep(norm_mix_g, w_in, conv_w, rwkv_mu, rwkv_w0, rwkv_w2, rwkv_a0, rwkv_a2, rwkv_kk, rwkv_ka, rwkv_rk,
          rwkv_gn_w, rwkv_gn_b, mla_q_norm_g, mla_w_qb, mla_kv_norm_g, mla_w_kvb, w_out, ple_w, ple_norm_g,
          ple_gate_w):
    depth = w_in.shape[0]
    o_kr = C_MG
    o_mg = o_kr + QK_ROPE_DIM
    zeros = jnp.zeros((depth, D_MODEL, QK_ROPE_DIM), F32)
    win_a = w_in.astype(BF16)
    win_b = jnp.concatenate([w_in[:, :, o_mg:], w_in[:, :, o_kr:o_mg], zeros], axis=2).astype(BF16)

    zl = jnp.zeros((depth, LORA, RWKV_WIDTH), F32)
    wl = jnp.concatenate([jnp.concatenate([rwkv_w2, zl], axis=2),
                          jnp.concatenate([zl, rwkv_a2], axis=2)], axis=1).astype(BF16)

    wqb = mla_w_qb.reshape(depth, Q_LORA_RANK, MLA_HEADS, QK_NOPE_DIM + QK_ROPE_DIM)
    zq = jnp.zeros((depth, Q_LORA_RANK, MLA_HEADS, QK_ROPE_DIM), F32)
    wq = jnp.concatenate([wqb, zq], axis=-1).reshape(depth, Q_LORA_RANK, MLA_HEADS * QK_PAD).astype(BF16)
    wkvb = mla_w_kvb.reshape(depth, KV_LORA_RANK, MLA_HEADS, QK_NOPE_DIM + V_HEAD_DIM)
    wkv = jnp.concatenate([wkvb[..., :QK_NOPE_DIM].reshape(depth, KV_LORA_RANK, -1),
                           wkvb[..., QK_NOPE_DIM:].reshape(depth, KV_LORA_RANK, -1)], axis=2).astype(BF16)
    row = lambda x: x.reshape(depth, 1, -1)
    hd = lambda x: jnp.pad(x.reshape(depth, RWKV_HEADS, RWKV_HEAD_DIM),
                           ((0, 0), (0, 0), (LANE - RWKV_HEAD_DIM, 0)))
    return dict(
        ng=row(norm_mix_g), win_a=win_a, win_b=win_b, cw=conv_w,
        mu_rkv=row(rwkv_mu[:, :3 * RWKV_WIDTH]), mu_wa=row(rwkv_mu[:, 3 * RWKV_WIDTH:]),
        w0=row(rwkv_w0), a0=row(rwkv_a0), wl=wl, kk=row(rwkv_kk), ka=row(rwkv_ka),
        qg=row(mla_q_norm_g), wq=wq, kvg=row(mla_kv_norm_g), wkv=wkv,
        rk=row(rwkv_rk), gnw=hd(rwkv_gn_w), gnb=hd(rwkv_gn_b),
        wo=w_out.astype(BF16), wple=ple_w.astype(BF16), png=row(ple_norm_g), wpg=ple_gate_w.astype(BF16))


def kernel(x, p, positions, norm_mix_g, w_in, conv_w, rwkv_mu, rwkv_w0, rwkv_w2, rwkv_a0, rwkv_a2,
           rwkv_kk, rwkv_ka, rwkv_rk, rwkv_gn_w, rwkv_gn_b, mla_q_norm_g, mla_w_qb, mla_kv_norm_g,
           mla_w_kvb, w_out, ple_w, ple_norm_g, ple_gate_w, final_norm_g):
    batch, seq, _ = x.shape
    depth = w_in.shape[0]
    t = batch * seq

    half = QK_ROPE_DIM // 2
    inv_freq = 1.0 / (ROPE_THETA ** (jnp.arange(0, QK_ROPE_DIM, 2, dtype=F32) / QK_ROPE_DIM))
    inv_tile = jnp.concatenate([inv_freq, inv_freq, jnp.zeros((LANE - QK_ROPE_DIM,), F32)])
    lane = np.arange(LANE)
    ang = positions.astype(F32).reshape(t, 1) * inv_tile
    ck = jnp.where(lane < QK_ROPE_DIM, jnp.cos(ang), 0.0)
    sk = jnp.sin(ang) * jnp.asarray(np.where(lane < half, -1.0, 1.0), F32)

    h = x.reshape(t, D_MODEL)
    p = p.reshape(depth, t, PLE_DIM)
    final_g = final_norm_g.reshape(1, D_MODEL)
    lw = _prep(norm_mix_g, w_in, conv_w, rwkv_mu, rwkv_w0, rwkv_w2, rwkv_a0, rwkv_a2, rwkv_kk, rwkv_ka,
               rwkv_rk, rwkv_gn_w, rwkv_gn_b, mla_q_norm_g, mla_w_qb, mla_kv_norm_g, mla_w_kvb, w_out,
               ple_w, ple_norm_g, ple_gate_w)
    for i in range(depth):
        (yconv, *hm, wc, gate_r, bg, q, kq, vq, gate_m) = _proj_call(h, lw, i, ck, sk, batch, seq)
        y_rwkv = _rwkv_call(hm, wc, gate_r, bg, lw, i, batch, seq).reshape(t, RWKV_WIDTH)
        y_mla = _attn_call(q, kq, vq, gate_m, batch, seq).reshape(t, MLA_WIDTH)
        h = _out_call(h, yconv, y_rwkv, y_mla, p, i, lw, final_g, final=(i == depth - 1))
    return h.reshape(batch, seq, D_MODEL)
```

```python
import functools
import math

import jax
import jax.numpy as jnp
import numpy as np
from jax import lax
from jax.experimental import pallas as pl
from jax.experimental.pallas import tpu as pltpu

F32 = jnp.float32
BF16 = jnp.bfloat16

D_MODEL = 1024
CHUNK = 64
PLE_DIM = 256
NORM_EPS = 1e-6
CONV_WIDTH = 256
RWKV_HEADS = 4
RWKV_HEAD_DIM = 64
RWKV_WIDTH = RWKV_HEADS * RWKV_HEAD_DIM
LORA = 64
DECAY_SCALE = math.exp(-0.5)
GN_EPS = 64e-5
MLA_HEADS = 4
QK_NOPE_DIM = 128
QK_ROPE_DIM = 64
V_HEAD_DIM = 128
Q_LORA_RANK = 384
KV_LORA_RANK = 256
MLA_WIDTH = MLA_HEADS * V_HEAD_DIM
ROPE_THETA = 10000.0
D_MIX = CONV_WIDTH + RWKV_WIDTH + MLA_WIDTH

C_CONV = 0
C_RKV = 1024
C_WA = 1792
C_RG = 1920
C_QA = 2176
C_KVA = 2560
C_MG = 2816
C_KR = 3328
D_IN_P = 3456
QK_PAD = 256

LANE = 128
BF16_ROWS = 16
HALO = BF16_ROWS
TM = 512
OUT_ROW_BLOCKS = 2
TQ = 512
EXP_ROWS = 128
EXP_LAG = 1
RWKV_CHUNKS_PER_STEP = 4
VMEM_LIMIT = 56 * 1024 * 1024


def _dot(a, b):
    return jnp.dot(a.astype(BF16), b.astype(BF16), preferred_element_type=F32)


def _dot_nt(a, b):
    return lax.dot_general(a.astype(BF16), b.astype(BF16), (((1,), (1,)), ((), ())),
                           preferred_element_type=F32)


def _dot_tn(a, b):
    return lax.dot_general(a.astype(BF16), b.astype(BF16), (((0,), (0,)), ((), ())),
                           preferred_element_type=F32)


def _split2(x):
    hi = x.astype(BF16)
    return hi, (x - hi.astype(F32)).astype(BF16)


def _dot_mask(x, m01, left=False):
    hi, lo = _split2(x)
    if left:
        return jnp.dot(m01, hi, preferred_element_type=F32) + jnp.dot(m01, lo, preferred_element_type=F32)
    return jnp.dot(hi, m01, preferred_element_type=F32) + jnp.dot(lo, m01, preferred_element_type=F32)


def _chunk_cumsum(x):
    pos = lax.broadcasted_iota(jnp.int32, x.shape, 0) % CHUNK
    d = 1
    while d < CHUNK:
        x = x + jnp.where(pos >= d, pltpu.roll(x, d, axis=0), 0.0)
        d *= 2
    return x


def _swap_rope_halves(x):
    half = QK_ROPE_DIM // 2
    return pltpu.roll(x, half, axis=1) + pltpu.roll(x, LANE - half, axis=1)


def _rms(x, g):
    return x * lax.rsqrt(jnp.mean(x * x, axis=-1, keepdims=True) + NORM_EPS) * g


def _silu(x):
    return x * jax.nn.sigmoid(x)


def _proj_kernel(h_ref, ng_ref, wa_ref, wb_ref, cwt_ref, mu_rkv_ref, mu_wa_ref, w0_ref, a0_ref,
                 wl_ref, kk_ref, ka_ref, rk_ref, seg_ref, csum_ref, qg_ref, wq_ref, kvg_ref, wkv_ref,
                 ck_ref, sk_ref,
                 yconv_ref, rt_ref, at_ref, bt_ref, kt_ref, bh_ref, kh_ref, v_ref, wc_ref, gr_ref, bg_ref,
                 q_ref, kq_ref, vq_ref, gm_ref, z_scr, *, tiles_per_seq):
    hm_refs = (rt_ref, at_ref, bt_ref, kt_ref, bh_ref, kh_ref, v_ref)
    tm = h_ref.shape[0]
    tile = pl.ds(HALO, tm)

    @pl.when(pl.program_id(0) == 0)
    def _():
        z_scr[0:HALO, :] = jnp.zeros((HALO, D_IN_P), F32)

    u = _rms(h_ref[...], ng_ref[...]).astype(BF16)
    z_scr[tile, C_RKV:C_MG] = jnp.dot(u, wa_ref[:, C_RKV:C_MG], preferred_element_type=F32)

    first = pl.program_id(0) % tiles_per_seq == 0
    row = lax.broadcasted_iota(jnp.int32, (tm, 1), 0)

    def cols(c, w, back=0):
        x = z_scr[pl.ds(HALO - back, tm), c:c + w]
        return jnp.where(row >= jnp.where(first, back, 0), x, 0.0) if back else x

    cur = cols(C_RKV, 768)
    rkv = cur + (cols(C_RKV, 768, 1) - cur) * mu_rkv_ref[...]
    cur = cols(C_WA, 128)
    wa = cur + (cols(C_WA, 128, 1) - cur) * mu_wa_ref[...]
    r = rkv[:, 0:256]
    k = rkv[:, 256:512]
    v = rkv[:, 512:768]
    lane = lax.broadcasted_iota(jnp.int32, wa.shape, 1)
    lora_in = jnp.where(lane < LORA, jnp.tanh(wa), wa)
    lora = jnp.dot(lora_in.astype(BF16), wl_ref[...], preferred_element_type=F32)
    wlog = -DECAY_SCALE * jax.nn.sigmoid(w0_ref[...] + lora[:, :RWKV_WIDTH])
    a = jax.nn.sigmoid(a0_ref[...] + lora[:, RWKV_WIDTH:])
    kk = k * kk_ref[...]
    kk = kk * lax.rsqrt(_dot_mask(kk * kk, seg_ref[...]) + 1e-12)
    qn = _rms(cols(C_QA, Q_LORA_RANK), qg_ref[...]).astype(BF16)
    qm = jnp.dot(qn, wq_ref[...], preferred_element_type=F32)
    kvn = _rms(cols(C_KVA, KV_LORA_RANK), kvg_ref[...]).astype(BF16)
    kv = jnp.dot(kvn, wkv_ref[...], preferred_element_type=F32)
    z_scr[tile, C_MG:] = jnp.dot(u, wb_ref[...], preferred_element_type=F32)
    kmod = k * (1.0 + (a - 1.0) * ka_ref[...])
    zb = kk * a
    cw = _chunk_cumsum(wlog)
    total = _dot_mask(wlog, csum_ref[...], left=True)
    nchunk = tm // CHUNK
    rest = jnp.broadcast_to(total[:, None, :], (nchunk, CHUNK, RWKV_WIDTH)).reshape(tm, RWKV_WIDTH) - cw
    inv = jnp.exp(-cw)
    to_end = jnp.exp(rest)
    wc = jnp.exp(total)
    ops = (r * jnp.exp(cw), -kk * jnp.exp(cw - wlog), zb * inv, kmod * inv, zb * to_end, kmod * to_end, v)
    zpad = jnp.zeros((tm, LANE - RWKV_HEAD_DIM), BF16)
    for hd in range(RWKV_HEADS):
        sl = slice(hd * RWKV_HEAD_DIM, (hd + 1) * RWKV_HEAD_DIM)
        for ref, val in zip(hm_refs, ops):
            piece = val[:, sl].astype(BF16)
            if ref is at_ref:
                piece = jnp.concatenate([piece, zpad], axis=1)
            elif ref is v_ref:
                piece = jnp.concatenate([zpad, piece], axis=1)
            ref[0, hd] = piece
        wc_ref[0, hd] = wc[:, sl]
    gate_r = _silu(cols(C_RG, 256))
    bonus = _dot_mask(r * kmod * rk_ref[...], seg_ref[...]) * v
    gr_ref[...] = gate_r
    bg_ref[...] = bonus * gate_r

    ck = ck_ref[...]
    sk = sk_ref[...]
    scale = math.log2(math.e) / math.sqrt(QK_NOPE_DIM + QK_ROPE_DIM)
    kx = cols(C_KR, LANE)
    kr = kx * ck + _swap_rope_halves(kx) * sk
    for hd in range(MLA_HEADS):
        o = hd * QK_PAD
        qx = qm[:, o + LANE:o + QK_PAD]
        qr = qx * ck + _swap_rope_halves(qx) * sk
        q_ref[0, o:o + LANE, :] = (qm[:, o:o + LANE] * scale).T.astype(BF16)
        q_ref[0, o + LANE:o + QK_PAD, :] = (qr * scale).T.astype(BF16)
        kq_ref[:, o:o + LANE] = kv[:, hd * LANE:(hd + 1) * LANE].astype(BF16)
        kq_ref[:, o + LANE:o + QK_PAD] = kr.astype(BF16)
    vq_ref[0] = kv[:, MLA_HEADS * QK_NOPE_DIM:].T.astype(BF16)
    gm_ref[...] = _silu(cols(C_MG, MLA_WIDTH))

    z_scr[tile, :C_RKV] = jnp.dot(u, wa_ref[:, :C_RKV], preferred_element_type=F32)
    cwt = cwt_ref[...]
    conv = sum(cols(C_CONV + 256, 256, back) * cols(C_CONV + 512, 256, back) * cwt[2 - back:3 - back, :]
               for back in range(3))
    yconv_ref[...] = (cols(C_CONV, 256) * conv * _silu(cols(C_CONV + 768, 256))).astype(BF16)

    shifted = slice(C_CONV + 256, C_RG)
    z_scr[0:HALO, shifted] = z_scr[pl.ds(tm, HALO), shifted]


def _const_spec(arr):
    return pl.BlockSpec(arr.shape, lambda *_: (0,) * arr.ndim)


def _layer_spec(arr, layer):
    zeros = (0,) * (arr.ndim - 1)
    return pl.BlockSpec((None,) + arr.shape[1:], lambda *_: (layer,) + zeros)


def _proj_call(h, lw, layer, ck, sk, batch, seq):
    t = h.shape[0]
    tiles_per_seq = seq // TM
    row = lambda w: pl.BlockSpec((TM, w), lambda i: (i, 0))
    chunk_id = np.arange(TM) // CHUNK
    csum = jnp.asarray(np.arange(TM // CHUNK)[:, None] == chunk_id[None, :], BF16)
    head = np.arange(RWKV_WIDTH) // RWKV_HEAD_DIM
    seg = jnp.asarray(head[:, None] == head[None, :], BF16)
    per_layer = lambda *names: [(lw[n], _layer_spec(lw[n], layer)) for n in names]
    shared = lambda *arrs: [(a, _const_spec(a)) for a in arrs]
    consts = (per_layer("ng", "win_a", "win_b", "cw", "mu_rkv", "mu_wa", "w0", "a0", "wl", "kk", "ka", "rk")
              + shared(seg, csum) + per_layer("qg", "wq", "kvg", "wkv"))
    hm_widths = [RWKV_HEAD_DIM, LANE] + [RWKV_HEAD_DIM] * 4 + [LANE]
    hm = [pl.BlockSpec((1, RWKV_HEADS, TM, w), lambda i: (i // tiles_per_seq, 0, i % tiles_per_seq, 0))
          for w in hm_widths]
    hm_shapes = [jax.ShapeDtypeStruct((batch, RWKV_HEADS, seq, w), BF16) for w in hm_widths]
    wc_spec = pl.BlockSpec((1, RWKV_HEADS, TM // CHUNK, RWKV_HEAD_DIM),
                           lambda i: (i // tiles_per_seq, 0, i % tiles_per_seq, 0))
    out_shape = [jax.ShapeDtypeStruct((t, CONV_WIDTH), BF16)] + hm_shapes + [
        jax.ShapeDtypeStruct((batch, RWKV_HEADS, seq // CHUNK, RWKV_HEAD_DIM), F32),
        jax.ShapeDtypeStruct((t, RWKV_WIDTH), F32),
        jax.ShapeDtypeStruct((t, RWKV_WIDTH), F32),
        jax.ShapeDtypeStruct((t // TM, MLA_HEADS * QK_PAD, TM), BF16),
        jax.ShapeDtypeStruct((t, MLA_HEADS * QK_PAD), BF16),
        jax.ShapeDtypeStruct((t // TM, MLA_WIDTH, TM), BF16),
        jax.ShapeDtypeStruct((t, MLA_WIDTH), F32),
    ]
    vt_spec = pl.BlockSpec((1, MLA_WIDTH, TM), lambda i: (i, 0, 0))
    qt_spec = pl.BlockSpec((1, MLA_HEADS * QK_PAD, TM), lambda i: (i, 0, 0))
    out_specs = [row(CONV_WIDTH)] + hm + [wc_spec, row(RWKV_WIDTH), row(RWKV_WIDTH),
                                                  qt_spec, row(MLA_HEADS * QK_PAD),
                                                  vt_spec, row(MLA_WIDTH)]
    return pl.pallas_call(
        functools.partial(_proj_kernel, tiles_per_seq=tiles_per_seq),
        grid=(t // TM,),
        in_specs=[row(D_MODEL)] + [s for _, s in consts] + [row(LANE), row(LANE)],
        out_specs=out_specs,
        out_shape=out_shape,
        scratch_shapes=[pltpu.VMEM((HALO + TM, D_IN_P), F32)],
        compiler_params=pltpu.CompilerParams(dimension_semantics=("arbitrary",),
                                             vmem_limit_bytes=VMEM_LIMIT),
        name="proj",
    )(h, *[a for a, _ in consts], ck, sk)


def _rwkv_chunk_local(r_t, a_t, b_t, k_t, bh, kh, v, wc, tri_incl, tri_strict, eye):
    n = range(len(r_t))
    half = RWKV_HEAD_DIM
    lhs = [jnp.concatenate([a_t[i][:, :half], r_t[i]], axis=0) for i in n]
    bk = [jnp.concatenate([b_t[i], k_t[i]], axis=0) for i in n]
    abk = [_dot_nt(lhs[i], bk[i]) for i in n]
    top = [jnp.where(tri_strict, abk[i][:CHUNK], 0.0).astype(BF16) for i in n]
    bot = [jnp.where(tri_incl, abk[i][CHUNK:], 0.0).astype(BF16) for i in n]

    zeros = jnp.zeros((CHUNK, LANE), BF16)
    x = [a_t[i].astype(F32) + _dot(top[i], jnp.concatenate([zeros, v[i]], axis=0)) for i in n]
    p = [top[i][:, :half] for i in n]
    for it in range(6):
        x = [x[i] + _dot(p[i], x[i]) for i in n]
        if it < 5:
            p = [_dot(p[i], p[i]).astype(BF16) for i in n]
    xv = [jnp.concatenate([x[i].astype(BF16), v[i]], axis=0) for i in n]

    yloc = [_dot(bot[i], xv[i]) for i in n]
    qp = [r_t[i].astype(F32) + yloc[i][:, :half] for i in n]
    nc = [_dot_tn(jnp.concatenate([bh[i], kh[i]], axis=0), xv[i]) for i in n]
    mc = [jnp.where(eye, wc[i], 0.0) + nc[i][:, :half] for i in n]
    return qp, yloc, mc, nc


def _rwkv_kernel(rt_ref, at_ref, bt_ref, kt_ref, bh_ref, kh_ref, v_ref, wc_ref, gate_ref, bg_ref,
                 gnw_ref, gnb_ref, gnm_ref, y_ref, state_ref):
    step = pl.program_id(0)
    batch = rt_ref.shape[0]

    @pl.when(step == 0)
    def _():
        state_ref[...] = jnp.zeros_like(state_ref)

    ri = lax.broadcasted_iota(jnp.int32, (CHUNK, LANE), 0)
    li = lax.broadcasted_iota(jnp.int32, (CHUNK, LANE), 1)
    ci = li % CHUNK
    tri_incl = ri >= ci
    tri_strict = ri > ci
    eye = (ri == li)[:, :RWKV_HEAD_DIM]
    value_lanes = li >= RWKV_HEAD_DIM
    seqs = [(b, hd) for b in range(batch) for hd in range(RWKV_HEADS)]
    items = [(c, b, hd) for c in range(RWKV_CHUNKS_PER_STEP) for b, hd in seqs]
    n = range(len(items))
    load = lambda ref: [ref[b, hd, pl.ds(c * CHUNK, CHUNK), :] for c, b, hd in items]
    wc = [wc_ref[b, hd, pl.ds(step * RWKV_CHUNKS_PER_STEP + c, 1), :] for c, b, hd in items]
    qp, yloc, mc, nc = _rwkv_chunk_local(load(rt_ref), load(at_ref), load(bt_ref), load(kt_ref),
                                         load(bh_ref), load(kh_ref), load(v_ref), wc,
                                         tri_incl, tri_strict, eye)
    ns = range(len(seqs))
    states = [state_ref[s] for s in ns]
    y = []
    for c in range(RWKV_CHUNKS_PER_STEP):
        o = c * len(seqs)
        y += [_dot(qp[o + s], states[s]) + yloc[o + s] for s in ns]
        states = [jnp.where(value_lanes, _dot(mc[o + s], states[s]) + nc[o + s], 0.0) for s in ns]
    for s in ns:
        state_ref[s] = states[s]
    gnm = gnm_ref[...]
    head = [hd for _, _, hd in items]
    yall = jnp.concatenate(y, axis=0)
    yc = yall - _dot(yall, gnm)
    scale = lax.rsqrt(_dot(yc * yc, gnm) + GN_EPS)
    part = lambda a, i: a[i * CHUNK:(i + 1) * CHUNK]
    gn = [part(yc, i) * part(scale, i) * gnw_ref[head[i]:head[i] + 1, :]
          + gnb_ref[head[i]:head[i] + 1, :] for i in n]
    for c in range(RWKV_CHUNKS_PER_STEP):
        rows = pl.ds(c * CHUNK, CHUNK)
        for b in range(batch):
            o = (c * batch + b) * RWKV_HEADS
            pair = [pltpu.roll(gn[o + hd], RWKV_HEAD_DIM, axis=1) + gn[o + hd + 1] for hd in (0, 2)]
            y_ref[b, rows, :] = (jnp.concatenate(pair, axis=1) * gate_ref[b, rows, :]
                                 + bg_ref[b, rows, :]).astype(BF16)


def _rwkv_call(hm, wc, gate, bg, lw, layer, batch, seq):
    tc = RWKV_CHUNKS_PER_STEP * CHUNK
    hm_spec = lambda a: pl.BlockSpec((batch, RWKV_HEADS, tc, a.shape[-1]), lambda c: (0, 0, c, 0))
    nat = pl.BlockSpec((batch, tc, RWKV_WIDTH), lambda c: (0, c, 0))
    value = np.arange(LANE) >= RWKV_HEAD_DIM
    gnm = jnp.asarray((value[:, None] & value[None, :]) / RWKV_HEAD_DIM, BF16)
    return pl.pallas_call(
        _rwkv_kernel,
        grid=(seq // tc,),
        in_specs=[hm_spec(a) for a in hm] + [_const_spec(wc), nat, nat,
                                             _layer_spec(lw["gnw"], layer), _layer_spec(lw["gnb"], layer),
                                             _const_spec(gnm)],
        out_specs=nat,
        out_shape=jax.ShapeDtypeStruct((batch, seq, RWKV_WIDTH), BF16),
        scratch_shapes=[pltpu.VMEM((batch * RWKV_HEADS, RWKV_HEAD_DIM, LANE), F32)],
        compiler_params=pltpu.CompilerParams(dimension_semantics=("arbitrary",),
                                             vmem_limit_bytes=VMEM_LIMIT),
        name="rwkv",
    )(*hm, wc, gate.reshape(batch, seq, RWKV_WIDTH), bg.reshape(batch, seq, RWKV_WIDTH),
      lw["gnw"], lw["gnb"], gnm)


def _attn_kernel(qt_ref, k_ref, vt_ref, g_ref, o_ref, m_scr, acc_scr,
                 s0_scr, s1_scr, c0_scr, c1_scr, p0_scr, p1_scr, a0_scr, a1_scr):
    tq = s0_scr.shape[1]
    n_q = qt_ref.shape[1]
    sbufs = ((s0_scr, c0_scr), (s1_scr, c1_scr))
    pbufs = ((p0_scr, a0_scr), (p1_scr, a1_scr))
    ones = jnp.ones((BF16_ROWS, tq), BF16)
    tiles = [(i, j) for i in range(n_q) for j in range(i + 1)]

    half = tq // 2

    def masked(s, key0, query0):
        kc = (key0 + lax.broadcasted_iota(jnp.int32, s.shape, 0)) // CHUNK
        qc = (query0 + lax.broadcasted_iota(jnp.int32, s.shape, 1)) // CHUNK
        return jnp.where(kc <= qc, s, -1e30)

    def scores(i, j, slot):
        s_ref, c_ref = sbufs[slot]
        if i == j:
            top = masked(jnp.dot(k_ref[0, j * tq:j * tq + half, :], qt_ref[0, i],
                                 preferred_element_type=F32), 0, 0)
            bot = masked(jnp.dot(k_ref[0, j * tq + half:(j + 1) * tq, :], qt_ref[0, i, :, half:],
                                 preferred_element_type=F32), half, half)
            s_ref[:half, :] = top
            s_ref[half:, half:] = bot
            ctop = jnp.max(top, axis=0, keepdims=True)
            cbot = jnp.max(bot, axis=0, keepdims=True)
            c_ref[...] = jnp.concatenate([ctop[:, :half], jnp.maximum(ctop[:, half:], cbot)], axis=1)
            return
        k = k_ref[0, j * tq:(j + 1) * tq, :]
        s = jnp.dot(k, qt_ref[0, i], preferred_element_type=F32)
        s_ref[...] = s
        c_ref[...] = jnp.max(s, axis=0, keepdims=True)

    def exponentiate(i, j, slot):
        s_ref, c_ref = sbufs[slot]
        p_ref, a_ref = pbufs[slot]
        if j == 0:
            m_new = c_ref[...]
        else:
            m_old = m_scr[...]
            m_new = jnp.maximum(m_old, c_ref[...])
            a_ref[...] = jnp.exp2(m_old - m_new)
        m8 = jnp.broadcast_to(m_new, (8, tq))
        done = []
        for b in range(tq // EXP_ROWS):
            lanes = slice(half if i == j and b * EXP_ROWS >= half else 0, tq)
            width = tq - lanes.start
            mb = m8[:, lanes]
            if b >= EXP_LAG:
                seen = jnp.concatenate([done[b - EXP_LAG]] * (width // LANE), axis=1)
                mb = jnp.where(seen <= 1.0, mb, 0.0)
            rows = pl.ds(b * EXP_ROWS, EXP_ROWS)
            x = s_ref[rows, lanes].reshape(EXP_ROWS // 8, 8, width) - mb[None]
            pb = jnp.exp2(x.reshape(EXP_ROWS, width).astype(BF16))
            p_ref[rows, lanes] = pb
            done.append(pb[0:BF16_ROWS, 0:LANE].astype(F32)[0:8])
        if j < i:
            m_scr[...] = m_new

    def accumulate(i, j, slot):
        p_ref, a_ref = pbufs[slot]
        vt1 = jnp.concatenate([vt_ref[0, j], ones], axis=0)
        if j == i:
            atop = jnp.dot(vt1[:, :half], p_ref[:half, :], preferred_element_type=F32)
            abot = jnp.dot(vt1[:, half:], p_ref[half:, half:], preferred_element_type=F32)
            acc = jnp.concatenate([atop[:, :half], atop[:, half:] + abot], axis=1)
        else:
            acc = jnp.dot(vt1, p_ref[...], preferred_element_type=F32)
        if j > 0:
            acc = a_ref[...] * acc_scr[...] + acc
        if j == i:
            rows = slice(i * tq, (i + 1) * tq)
            o = acc[:V_HEAD_DIM] / acc[V_HEAD_DIM:V_HEAD_DIM + 1]
            o_ref[0, rows, :] = (o.T * g_ref[0, rows, :]).astype(BF16)
        else:
            acc_scr[...] = acc

    scores(*tiles[0], 0)
    scores(*tiles[1], 1)
    exponentiate(*tiles[0], 0)
    for t, (i, j) in enumerate(tiles):
        if t + 2 < len(tiles):
            scores(*tiles[t + 2], t % 2)
        if t + 1 < len(tiles):
            exponentiate(*tiles[t + 1], (t + 1) % 2)
        accumulate(i, j, t % 2)


def _attn_call(qt, k, vt, gate, batch, seq):
    qt = qt.reshape(batch, seq // TQ, MLA_HEADS * QK_PAD, TQ)
    k = k.reshape(batch, seq, MLA_HEADS * QK_PAD)
    vt = vt.reshape(batch, seq // TQ, MLA_WIDTH, TQ)
    gate = gate.reshape(batch, seq, MLA_WIDTH)
    return pl.pallas_call(
        _attn_kernel,
        grid=(batch, MLA_HEADS),
        in_specs=[pl.BlockSpec((1, seq // TQ, QK_PAD, TQ), lambda b, h: (b, 0, h, 0)),
                  pl.BlockSpec((1, seq, QK_PAD), lambda b, h: (b, 0, h)),
                  pl.BlockSpec((1, seq // TQ, V_HEAD_DIM, TQ), lambda b, h: (b, 0, h, 0)),
                  pl.BlockSpec((1, seq, V_HEAD_DIM), lambda b, h: (b, 0, h))],
        out_specs=pl.BlockSpec((1, seq, V_HEAD_DIM), lambda b, h: (b, 0, h)),
        out_shape=jax.ShapeDtypeStruct((batch, seq, MLA_WIDTH), BF16),
        scratch_shapes=[pltpu.VMEM((1, TQ), F32),
                        pltpu.VMEM((V_HEAD_DIM + BF16_ROWS, TQ), F32),
                        pltpu.VMEM((TQ, TQ), F32), pltpu.VMEM((TQ, TQ), F32),
                        pltpu.VMEM((1, TQ), F32), pltpu.VMEM((1, TQ), F32),
                        pltpu.VMEM((TQ, TQ), BF16), pltpu.VMEM((TQ, TQ), BF16),
                        pltpu.VMEM((1, TQ), F32), pltpu.VMEM((1, TQ), F32)],
        compiler_params=pltpu.CompilerParams(
            dimension_semantics=("arbitrary", "arbitrary"), vmem_limit_bytes=VMEM_LIMIT),
        name="attn",
    )(qt, k, vt, gate)


def _out_kernel(h_ref, yc_ref, yr_ref, ym_ref, wo_ref, p_ref, wple_ref, png_ref, wpg_ref, fg_ref,
                o_ref, *, final):
    tm = h_ref.shape[0]
    blocks = [pl.ds(r, tm // OUT_ROW_BLOCKS) for r in range(0, tm, tm // OUT_ROW_BLOCKS)]
    ycat = [jnp.concatenate([yc_ref[b, :], yr_ref[b, :], ym_ref[b, :]], axis=-1) for b in blocks]
    h = [h_ref[b, :] + jnp.dot(y, wo_ref[...], preferred_element_type=F32) for b, y in zip(blocks, ycat)]
    hn = [_rms(x, png_ref[...]).astype(BF16) for x in h]
    gate = [jax.nn.sigmoid(jnp.dot(x, wpg_ref[...], preferred_element_type=F32)) for x in hn]
    ple = [jnp.dot(p_ref[b, :].astype(BF16), wple_ref[...], preferred_element_type=F32) for b in blocks]
    h = [x + e * g for x, e, g in zip(h, ple, gate)]
    if final:
        h = [_rms(x, fg_ref[...]) for x in h]
    for b, x in zip(blocks, h):
        o_ref[b, :] = x


def _out_call(h, yc, yr, ym, p, layer, lw, final_g, final):
    t = h.shape[0]
    row = lambda w: pl.BlockSpec((TM, w), lambda i: (i, 0))
    return pl.pallas_call(
        functools.partial(_out_kernel, final=final),
        grid=(t // TM,),
        in_specs=[row(D_MODEL), row(CONV_WIDTH), row(RWKV_WIDTH), row(MLA_WIDTH),
                  _layer_spec(lw["wo"], layer),
                  pl.BlockSpec((None, TM, PLE_DIM), lambda i: (layer, i, 0)),
                  _layer_spec(lw["wple"], layer), _layer_spec(lw["png"], layer),
                  _layer_spec(lw["wpg"], layer), _const_spec(final_g)],
        out_specs=row(D_MODEL),
        out_shape=jax.ShapeDtypeStruct((t, D_MODEL), F32),
        compiler_params=pltpu.CompilerParams(dimension_semantics=("arbitrary",),
                                             vmem_limit_bytes=VMEM_LIMIT),
        name="out",
    )(h, yc, yr, ym, lw["wo"], p, lw["wple"], lw["png"], lw["wpg"], final_g)


def _prep(norm_mix_g, w_in, conv_w, rwkv_mu, rwkv_w0, rwkv_w2, rwkv_a0, rwkv_a2, rwkv_kk, rwkv_ka, rwkv_rk,
          rwkv_gn_w, rwkv_gn_b, mla_q_norm_g, mla_w_qb, mla_kv_norm_g, mla_w_kvb, w_out, ple_w, ple_norm_g,
          ple_gate_w):
    depth = w_in.shape[0]
    o_kr = C_MG
    o_mg = o_kr + QK_ROPE_DIM
    zeros = jnp.zeros((depth, D_MODEL, QK_ROPE_DIM), F32)
    win_a = w_in.astype(BF16)
    win_b = jnp.concatenate([w_in[:, :, o_mg:], w_in[:, :, o_kr:o_mg], zeros], axis=2).astype(BF16)

    zl = jnp.zeros((depth, LORA, RWKV_WIDTH), F32)
    wl = jnp.concatenate([jnp.concatenate([rwkv_w2, zl], axis=2),
                          jnp.concatenate([zl, rwkv_a2], axis=2)], axis=1).astype(BF16)

    wqb = mla_w_qb.reshape(depth, Q_LORA_RANK, MLA_HEADS, QK_NOPE_DIM + QK_ROPE_DIM)
    zq = jnp.zeros((depth, Q_LORA_RANK, MLA_HEADS, QK_ROPE_DIM), F32)
    wq = jnp.concatenate([wqb, zq], axis=-1).reshape(depth, Q_LORA_RANK, MLA_HEADS * QK_PAD).astype(BF16)
    wkvb = mla_w_kvb.reshape(depth, KV_LORA_RANK, MLA_HEADS, QK_NOPE_DIM + V_HEAD_DIM)
    wkv = jnp.concatenate([wkvb[..., :QK_NOPE_DIM].reshape(depth, KV_LORA_RANK, -1),
                           wkvb[..., QK_NOPE_DIM:].reshape(depth, KV_LORA_RANK, -1)], axis=2).astype(BF16)
    row = lambda x: x.reshape(depth, 1, -1)
    hd = lambda x: jnp.pad(x.reshape(depth, RWKV_HEADS, RWKV_HEAD_DIM),
                           ((0, 0), (0, 0), (LANE - RWKV_HEAD_DIM, 0)))
    return dict(
        ng=row(norm_mix_g), win_a=win_a, win_b=win_b, cw=conv_w,
        mu_rkv=row(rwkv_mu[:, :3 * RWKV_WIDTH]), mu_wa=row(rwkv_mu[:, 3 * RWKV_WIDTH:]),
        w0=row(rwkv_w0), a0=row(rwkv_a0), wl=wl, kk=row(rwkv_kk), ka=row(rwkv_ka),
        qg=row(mla_q_norm_g), wq=wq, kvg=row(mla_kv_norm_g), wkv=wkv,
        rk=row(rwkv_rk), gnw=hd(rwkv_gn_w), gnb=hd(rwkv_gn_b),
        wo=w_out.astype(BF16), wple=ple_w.astype(BF16), png=row(ple_norm_g), wpg=ple_gate_w.astype(BF16))


def kernel(x, p, positions, norm_mix_g, w_in, conv_w, rwkv_mu, rwkv_w0, rwkv_w2, rwkv_a0, rwkv_a2,
           rwkv_kk, rwkv_ka, rwkv_rk, rwkv_gn_w, rwkv_gn_b, mla_q_norm_g, mla_w_qb, mla_kv_norm_g,
           mla_w_kvb, w_out, ple_w, ple_norm_g, ple_gate_w, final_norm_g):
    batch, seq, _ = x.shape
    depth = w_in.shape[0]
    t = batch * seq

    half = QK_ROPE_DIM // 2
    inv_freq = 1.0 / (ROPE_THETA ** (jnp.arange(0, QK_ROPE_DIM, 2, dtype=F32) / QK_ROPE_DIM))
    inv_tile = jnp.concatenate([inv_freq, inv_freq, jnp.zeros((LANE - QK_ROPE_DIM,), F32)])
    lane = np.arange(LANE)
    ang = positions.astype(F32).reshape(t, 1) * inv_tile
    ck = jnp.where(lane < QK_ROPE_DIM, jnp.cos(ang), 0.0)
    sk = jnp.sin(ang) * jnp.asarray(np.where(lane < half, -1.0, 1.0), F32)

    h = x.reshape(t, D_MODEL)
    p = p.reshape(depth, t, PLE_DIM)
    final_g = final_norm_g.reshape(1, D_MODEL)
    lw = _prep(norm_mix_g, w_in, conv_w, rwkv_mu, rwkv_w0, rwkv_w2, rwkv_a0, rwkv_a2, rwkv_kk, rwkv_ka,
               rwkv_rk, rwkv_gn_w, rwkv_gn_b, mla_q_norm_g, mla_w_qb, mla_kv_norm_g, mla_w_kvb, w_out,
               ple_w, ple_norm_g, ple_gate_w)
    for i in range(depth):
        (yconv, *hm, wc, gate_r, bg, q, kq, vq, gate_m) = _proj_call(h, lw, i, ck, sk, batch, seq)
        y_rwkv = _rwkv_call(hm, wc, gate_r, bg, lw, i, batch, seq).reshape(t, RWKV_WIDTH)
        y_mla = _attn_call(q, kq, vq, gate_m, batch, seq).reshape(t, MLA_WIDTH)
        h = _out_call(h, yconv, y_rwkv, y_mla, p, i, lw, final_g, final=(i == depth - 1))
    return h.reshape(batch, seq, D_MODEL)
```

```python
import functools
import math

import jax
import jax.numpy as jnp
import numpy as np
from jax import lax
from jax.experimental import pallas as pl
from jax.experimental.pallas import tpu as pltpu

F32 = jnp.float32
BF16 = jnp.bfloat16

D_MODEL = 1024
CHUNK = 64
PLE_DIM = 256
NORM_EPS = 1e-6
CONV_WIDTH = 256
RWKV_HEADS = 4
RWKV_HEAD_DIM = 64
RWKV_WIDTH = RWKV_HEADS * RWKV_HEAD_DIM
LORA = 64
DECAY_SCALE = math.exp(-0.5)
GN_EPS = 64e-5
MLA_HEADS = 4
QK_NOPE_DIM = 128
QK_ROPE_DIM = 64
V_HEAD_DIM = 128
Q_LORA_RANK = 384
KV_LORA_RANK = 256
MLA_WIDTH = MLA_HEADS * V_HEAD_DIM
ROPE_THETA = 10000.0
D_MIX = CONV_WIDTH + RWKV_WIDTH + MLA_WIDTH

C_CONV = 0
C_RKV = 1024
C_WA = 1792
C_RG = 1920
C_QA = 2176
C_KVA = 2560
C_MG = 2816
C_KR = 3328
D_IN_P = 3456
QK_PAD = 256

LANE = 128
BF16_ROWS = 16
HALO = BF16_ROWS
TM = 512
OUT_ROW_BLOCKS = 2
TQ = 512
EXP_ROWS = 128
EXP_LAG = 1
RWKV_CHUNKS_PER_STEP = 4
VMEM_LIMIT = 56 * 1024 * 1024


def _dot(a, b):
    return jnp.dot(a.astype(BF16), b.astype(BF16), preferred_element_type=F32)


def _dot_nt(a, b):
    return lax.dot_general(a.astype(BF16), b.astype(BF16), (((1,), (1,)), ((), ())),
                           preferred_element_type=F32)


def _dot_tn(a, b):
    return lax.dot_general(a.astype(BF16), b.astype(BF16), (((0,), (0,)), ((), ())),
                           preferred_element_type=F32)


def _split2(x):
    hi = x.astype(BF16)
    return hi, (x - hi.astype(F32)).astype(BF16)


def _dot_mask(x, m01, left=False):
    hi, lo = _split2(x)
    if left:
        return jnp.dot(m01, hi, preferred_element_type=F32) + jnp.dot(m01, lo, preferred_element_type=F32)
    return jnp.dot(hi, m01, preferred_element_type=F32) + jnp.dot(lo, m01, preferred_element_type=F32)


def _chunk_cumsum(x):
    pos = lax.broadcasted_iota(jnp.int32, x.shape, 0) % CHUNK
    d = 1
    while d < CHUNK:
        x = x + jnp.where(pos >= d, pltpu.roll(x, d, axis=0), 0.0)
        d *= 2
    return x


def _swap_rope_halves(x):
    half = QK_ROPE_DIM // 2
    return pltpu.roll(x, half, axis=1) + pltpu.roll(x, LANE - half, axis=1)


def _rms(x, g):
    return x * lax.rsqrt(jnp.mean(x * x, axis=-1, keepdims=True) + NORM_EPS) * g


def _silu(x):
    return x * jax.nn.sigmoid(x)


def _proj_kernel(h_ref, ng_ref, wa_ref, wb_ref, cwt_ref, mu_rkv_ref, mu_wa_ref, w0_ref, a0_ref,
                 wl_ref, kk_ref, ka_ref, rk_ref, seg_ref, csum_ref, qg_ref, wq_ref, kvg_ref, wkv_ref,
                 ck_ref, sk_ref,
                 yconv_ref, rt_ref, at_ref, bt_ref, kt_ref, bh_ref, kh_ref, v_ref, wc_ref, gr_ref, bg_ref,
                 q_ref, kq_ref, vq_ref, gm_ref, z_scr, *, tiles_per_seq):
    hm_refs = (rt_ref, at_ref, bt_ref, kt_ref, bh_ref, kh_ref, v_ref)
    tm = h_ref.shape[0]
    tile = pl.ds(HALO, tm)

    @pl.when(pl.program_id(0) == 0)
    def _():
        z_scr[0:HALO, :] = jnp.zeros((HALO, D_IN_P), F32)

    u = _rms(h_ref[...], ng_ref[...]).astype(BF16)
    z_scr[tile, C_RKV:C_MG] = jnp.dot(u, wa_ref[:, C_RKV:C_MG], preferred_element_type=F32)

    first = pl.program_id(0) % tiles_per_seq == 0
    row = lax.broadcasted_iota(jnp.int32, (tm, 1), 0)

    def cols(c, w, back=0):
        x = z_scr[pl.ds(HALO - back, tm), c:c + w]
        return jnp.where(row >= jnp.where(first, back, 0), x, 0.0) if back else x

    cur = cols(C_RKV, 768)
    rkv = cur + (cols(C_RKV, 768, 1) - cur) * mu_rkv_ref[...]
    cur = cols(C_WA, 128)
    wa = cur + (cols(C_WA, 128, 1) - cur) * mu_wa_ref[...]
    r = rkv[:, 0:256]
    k = rkv[:, 256:512]
    v = rkv[:, 512:768]
    lane = lax.broadcasted_iota(jnp.int32, wa.shape, 1)
    lora_in = jnp.where(lane < LORA, jnp.tanh(wa), wa)
    lora = jnp.dot(lora_in.astype(BF16), wl_ref[...], preferred_element_type=F32)
    wlog = -DECAY_SCALE * jax.nn.sigmoid(w0_ref[...] + lora[:, :RWKV_WIDTH])
    a = jax.nn.sigmoid(a0_ref[...] + lora[:, RWKV_WIDTH:])
    kk = k * kk_ref[...]
    kk = kk * lax.rsqrt(_dot_mask(kk * kk, seg_ref[...]) + 1e-12)
    qn = _rms(cols(C_QA, Q_LORA_RANK), qg_ref[...]).astype(BF16)
    qm = jnp.dot(qn, wq_ref[...], preferred_element_type=F32)
    kvn = _rms(cols(C_KVA, KV_LORA_RANK), kvg_ref[...]).astype(BF16)
    kv = jnp.dot(kvn, wkv_ref[...], preferred_element_type=F32)
    z_scr[tile, C_MG:] = jnp.dot(u, wb_ref[...], preferred_element_type=F32)
    kmod = k * (1.0 + (a - 1.0) * ka_ref[...])
    zb = kk * a
    cw = _chunk_cumsum(wlog)
    total = _dot_mask(wlog, csum_ref[...], left=True)
    nchunk = tm // CHUNK
    rest = jnp.broadcast_to(total[:, None, :], (nchunk, CHUNK, RWKV_WIDTH)).reshape(tm, RWKV_WIDTH) - cw
    inv = jnp.exp(-cw)
    to_end = jnp.exp(rest)
    wc = jnp.exp(total)
    ops = (r * jnp.exp(cw), -kk * jnp.exp(cw - wlog), zb * inv, kmod * inv, zb * to_end, kmod * to_end, v)
    zpad = jnp.zeros((tm, LANE - RWKV_HEAD_DIM), BF16)
    for hd in range(RWKV_HEADS):
        sl = slice(hd * RWKV_HEAD_DIM, (hd + 1) * RWKV_HEAD_DIM)
        for ref, val in zip(hm_refs, ops):
            piece = val[:, sl].astype(BF16)
            if ref is at_ref:
                piece = jnp.concatenate([piece, zpad], axis=1)
            elif ref is v_ref:
                piece = jnp.concatenate([zpad, piece], axis=1)
            ref[0, hd] = piece
        wc_ref[0, hd] = wc[:, sl]
    gate_r = _silu(cols(C_RG, 256))
    bonus = _dot_mask(r * kmod * rk_ref[...], seg_ref[...]) * v
    gr_ref[...] = gate_r
    bg_ref[...] = bonus * gate_r

    ck = ck_ref[...]
    sk = sk_ref[...]
    scale = math.log2(math.e) / math.sqrt(QK_NOPE_DIM + QK_ROPE_DIM)
    kx = cols(C_KR, LANE)
    kr = kx * ck + _swap_rope_halves(kx) * sk
    for hd in range(MLA_HEADS):
        o = hd * QK_PAD
        qx = qm[:, o + LANE:o + QK_PAD]
        qr = qx * ck + _swap_rope_halves(qx) * sk
        q_ref[0, o:o + LANE, :] = (qm[:, o:o + LANE] * scale).T.astype(BF16)
        q_ref[0, o + LANE:o + QK_PAD, :] = (qr * scale).T.astype(BF16)
        kq_ref[:, o:o + LANE] = kv[:, hd * LANE:(hd + 1) * LANE].astype(BF16)
        kq_ref[:, o + LANE:o + QK_PAD] = kr.astype(BF16)
    vq_ref[0] = kv[:, MLA_HEADS * QK_NOPE_DIM:].T.astype(BF16)
    gm_ref[...] = _silu(cols(C_MG, MLA_WIDTH))

    z_scr[tile, :C_RKV] = jnp.dot(u, wa_ref[:, :C_RKV], preferred_element_type=F32)
    cwt = cwt_ref[...]
    conv = sum(cols(C_CONV + 256, 256, back) * cols(C_CONV + 512, 256, back) * cwt[2 - back:3 - back, :]
               for back in range(3))
    yconv_ref[...] = (cols(C_CONV, 256) * conv * _silu(cols(C_CONV + 768, 256))).astype(BF16)

    shifted = slice(C_CONV + 256, C_RG)
    z_scr[0:HALO, shifted] = z_scr[pl.ds(tm, HALO), shifted]


def _const_spec(arr):
    return pl.BlockSpec(arr.shape, lambda *_: (0,) * arr.ndim)


def _layer_spec(arr, layer):
    zeros = (0,) * (arr.ndim - 1)
    return pl.BlockSpec((None,) + arr.shape[1:], lambda *_: (layer,) + zeros)


def _proj_call(h, lw, layer, ck, sk, batch, seq):
    t = h.shape[0]
    tiles_per_seq = seq // TM
    row = lambda w: pl.BlockSpec((TM, w), lambda i: (i, 0))
    chunk_id = np.arange(TM) // CHUNK
    csum = jnp.asarray(np.arange(TM // CHUNK)[:, None] == chunk_id[None, :], BF16)
    head = np.arange(RWKV_WIDTH) // RWKV_HEAD_DIM
    seg = jnp.asarray(head[:, None] == head[None, :], BF16)
    per_layer = lambda *names: [(lw[n], _layer_spec(lw[n], layer)) for n in names]
    shared = lambda *arrs: [(a, _const_spec(a)) for a in arrs]
    consts = (per_layer("ng", "win_a", "win_b", "cw", "mu_rkv", "mu_wa", "w0", "a0", "wl", "kk", "ka", "rk")
              + shared(seg, csum) + per_layer("qg", "wq", "kvg", "wkv"))
    hm_widths = [RWKV_HEAD_DIM, LANE] + [RWKV_HEAD_DIM] * 4 + [LANE]
    hm = [pl.BlockSpec((1, RWKV_HEADS, TM, w), lambda i: (i // tiles_per_seq, 0, i % tiles_per_seq, 0))
          for w in hm_widths]
    hm_shapes = [jax.ShapeDtypeStruct((batch, RWKV_HEADS, seq, w), BF16) for w in hm_widths]
    wc_spec = pl.BlockSpec((1, RWKV_HEADS, TM // CHUNK, RWKV_HEAD_DIM),
                           lambda i: (i // tiles_per_seq, 0, i % tiles_per_seq, 0))
    out_shape = [jax.ShapeDtypeStruct((t, CONV_WIDTH), BF16)] + hm_shapes + [
        jax.ShapeDtypeStruct((batch, RWKV_HEADS, seq // CHUNK, RWKV_HEAD_DIM), F32),
        jax.ShapeDtypeStruct((t, RWKV_WIDTH), F32),
        jax.ShapeDtypeStruct((t, RWKV_WIDTH), F32),
        jax.ShapeDtypeStruct((t // TM, MLA_HEADS * QK_PAD, TM), BF16),
        jax.ShapeDtypeStruct((t, MLA_HEADS * QK_PAD), BF16),
        jax.ShapeDtypeStruct((t // TM, MLA_WIDTH, TM), BF16),
        jax.ShapeDtypeStruct((t, MLA_WIDTH), F32),
    ]
    vt_spec = pl.BlockSpec((1, MLA_WIDTH, TM), lambda i: (i, 0, 0))
    qt_spec = pl.BlockSpec((1, MLA_HEADS * QK_PAD, TM), lambda i: (i, 0, 0))
    out_specs = [row(CONV_WIDTH)] + hm + [wc_spec, row(RWKV_WIDTH), row(RWKV_WIDTH),
                                                  qt_spec, row(MLA_HEADS * QK_PAD),
                                                  vt_spec, row(MLA_WIDTH)]
    return pl.pallas_call(
        functools.partial(_proj_kernel, tiles_per_seq=tiles_per_seq),
        grid=(t // TM,),
        in_specs=[row(D_MODEL)] + [s for _, s in consts] + [row(LANE), row(LANE)],
        out_specs=out_specs,
        out_shape=out_shape,
        scratch_shapes=[pltpu.VMEM((HALO + TM, D_IN_P), F32)],
        compiler_params=pltpu.CompilerParams(dimension_semantics=("arbitrary",),
                                             vmem_limit_bytes=VMEM_LIMIT),
        name="proj",
    )(h, *[a for a, _ in consts], ck, sk)


def _rwkv_chunk_local(r_t, a_t, b_t, k_t, bh, kh, v, wc, tri_incl, tri_strict, eye):
    n = range(len(r_t))
    half = RWKV_HEAD_DIM
    lhs = [jnp.concatenate([a_t[i][:, :half], r_t[i]], axis=0) for i in n]
    bk = [jnp.concatenate([b_t[i], k_t[i]], axis=0) for i in n]
    abk = [_dot_nt(lhs[i], bk[i]) for i in n]
    top = [jnp.where(tri_strict, abk[i][:CHUNK], 0.0).astype(BF16) for i in n]
    bot = [jnp.where(tri_incl, abk[i][CHUNK:], 0.0).astype(BF16) for i in n]

    zeros = jnp.zeros((CHUNK, LANE), BF16)
    x = [a_t[i].astype(F32) + _dot(top[i], jnp.concatenate([zeros, v[i]], axis=0)) for i in n]
    p = [top[i][:, :half] for i in n]
    for it in range(6):
        x = [x[i] + _dot(p[i], x[i]) for i in n]
        if it < 5:
            p = [_dot(p[i], p[i]).astype(BF16) for i in n]
    xv = [jnp.concatenate([x[i].astype(BF16), v[i]], axis=0) for i in n]

    yloc = [_dot(bot[i], xv[i]) for i in n]
    qp = [r_t[i].astype(F32) + yloc[i][:, :half] for i in n]
    nc = [_dot_tn(jnp.concatenate([bh[i], kh[i]], axis=0), xv[i]) for i in n]
    mc = [jnp.where(eye, wc[i], 0.0) + nc[i][:, :half] for i in n]
    return qp, yloc, mc, nc


def _rwkv_kernel(rt_ref, at_ref, bt_ref, kt_ref, bh_ref, kh_ref, v_ref, wc_ref, gate_ref, bg_ref,
                 gnw_ref, gnb_ref, gnm_ref, y_ref, state_ref):
    step = pl.program_id(0)
    batch = rt_ref.shape[0]

    @pl.when(step == 0)
    def _():
        state_ref[...] = jnp.zeros_like(state_ref)

    ri = lax.broadcasted_iota(jnp.int32, (CHUNK, LANE), 0)
    li = lax.broadcasted_iota(jnp.int32, (CHUNK, LANE), 1)
    ci = li % CHUNK
    tri_incl = ri >= ci
    tri_strict = ri > ci
    eye = (ri == li)[:, :RWKV_HEAD_DIM]
    value_lanes = li >= RWKV_HEAD_DIM
    seqs = [(b, hd) for b in range(batch) for hd in range(RWKV_HEADS)]
    items = [(c, b, hd) for c in range(RWKV_CHUNKS_PER_STEP) for b, hd in seqs]
    n = range(len(items))
    load = lambda ref: [ref[b, hd, pl.ds(c * CHUNK, CHUNK), :] for c, b, hd in items]
    wc = [wc_ref[b, hd, pl.ds(step * RWKV_CHUNKS_PER_STEP + c, 1), :] for c, b, hd in items]
    qp, yloc, mc, nc = _rwkv_chunk_local(load(rt_ref), load(at_ref), load(bt_ref), load(kt_ref),
                                         load(bh_ref), load(kh_ref), load(v_ref), wc,
                                         tri_incl, tri_strict, eye)
    ns = range(len(seqs))
    states = [state_ref[s] for s in ns]
    y = []
    for c in range(RWKV_CHUNKS_PER_STEP):
        o = c * len(seqs)
        y += [_dot(qp[o + s], states[s]) + yloc[o + s] for s in ns]
        states = [jnp.where(value_lanes, _dot(mc[o + s], states[s]) + nc[o + s], 0.0) for s in ns]
    for s in ns:
        state_ref[s] = states[s]
    gnm = gnm_ref[...]
    head = [hd for _, _, hd in items]
    yall = jnp.concatenate(y, axis=0)
    yc = yall - _dot(yall, gnm)
    scale = lax.rsqrt(_dot(yc * yc, gnm) + GN_EPS)
    part = lambda a, i: a[i * CHUNK:(i + 1) * CHUNK]
    gn = [part(yc, i) * part(scale, i) * gnw_ref[head[i]:head[i] + 1, :]
          + gnb_ref[head[i]:head[i] + 1, :] for i in n]
    for c in range(RWKV_CHUNKS_PER_STEP):
        rows = pl.ds(c * CHUNK, CHUNK)
        for b in range(batch):
            o = (c * batch + b) * RWKV_HEADS
            pair = [pltpu.roll(gn[o + hd], RWKV_HEAD_DIM, axis=1) + gn[o + hd + 1] for hd in (0, 2)]
            y_ref[b, rows, :] = (jnp.concatenate(pair, axis=1) * gate_ref[b, rows, :]
                                 + bg_ref[b, rows, :]).astype(BF16)


def _rwkv_call(hm, wc, gate, bg, lw, layer, batch, seq):
    tc = RWKV_CHUNKS_PER_STEP * CHUNK
    hm_spec = lambda a: pl.BlockSpec((batch, RWKV_HEADS, tc, a.shape[-1]), lambda c: (0, 0, c, 0))
    nat = pl.BlockSpec((batch, tc, RWKV_WIDTH), lambda c: (0, c, 0))
    value = np.arange(LANE) >= RWKV_HEAD_DIM
    gnm = jnp.asarray((value[:, None] & value[None, :]) / RWKV_HEAD_DIM, BF16)
    return pl.pallas_call(
        _rwkv_kernel,
        grid=(seq // tc,),
        in_specs=[hm_spec(a) for a in hm] + [_const_spec(wc), nat, nat,
                                             _layer_spec(lw["gnw"], layer), _layer_spec(lw["gnb"], layer),
                                             _const_spec(gnm)],
        out_specs=nat,
        out_shape=jax.ShapeDtypeStruct((batch, seq, RWKV_WIDTH), BF16),
        scratch_shapes=[pltpu.VMEM((batch * RWKV_HEADS, RWKV_HEAD_DIM, LANE), F32)],
        compiler_params=pltpu.CompilerParams(dimension_semantics=("arbitrary",),
                                             vmem_limit_bytes=VMEM_LIMIT),
        name="rwkv",
    )(*hm, wc, gate.reshape(batch, seq, RWKV_WIDTH), bg.reshape(batch, seq, RWKV_WIDTH),
      lw["gnw"], lw["gnb"], gnm)


def _attn_kernel(qt_ref, k_ref, vt_ref, g_ref, o_ref, m_scr, acc_scr,
                 s0_scr, s1_scr, c0_scr, c1_scr, p0_scr, p1_scr, a0_scr, a1_scr):
    tq = s0_scr.shape[1]
    n_q = qt_ref.shape[1]
    sbufs = ((s0_scr, c0_scr), (s1_scr, c1_scr))
    pbufs = ((p0_scr, a0_scr), (p1_scr, a1_scr))
    ones = jnp.ones((BF16_ROWS, tq), BF16)
    tiles = [(i, j) for i in range(n_q) for j in range(i + 1)]

    half = tq // 2

    def masked(s, key0, query0):
        kc = (key0 + lax.broadcasted_iota(jnp.int32, s.shape, 0)) // CHUNK
        qc = (query0 + lax.broadcasted_iota(jnp.int32, s.shape, 1)) // CHUNK
        return jnp.where(kc <= qc, s, -1e30)

    def scores(i, j, slot):
        s_ref, c_ref = sbufs[slot]
        if i == j:
            top = masked(jnp.dot(k_ref[0, j * tq:j * tq + half, :], qt_ref[0, i],
                                 preferred_element_type=F32), 0, 0)
            bot = masked(jnp.dot(k_ref[0, j * tq + half:(j + 1) * tq, :], qt_ref[0, i, :, half:],
                                 preferred_element_type=F32), half, half)
            s_ref[:half, :] = top
            s_ref[half:, half:] = bot
            ctop = jnp.max(top, axis=0, keepdims=True)
            cbot = jnp.max(bot, axis=0, keepdims=True)
            c_ref[...] = jnp.concatenate([ctop[:, :half], jnp.maximum(ctop[:, half:], cbot)], axis=1)
            return
        k = k_ref[0, j * tq:(j + 1) * tq, :]
        s = jnp.dot(k, qt_ref[0, i], preferred_element_type=F32)
        s_ref[...] = s
        c_ref[...] = jnp.max(s, axis=0, keepdims=True)

    def exponentiate(i, j, slot):
        s_ref, c_ref = sbufs[slot]
        p_ref, a_ref = pbufs[slot]
        if j == 0:
            m_new = c_ref[...]
        else:
            m_old = m_scr[...]
            m_new = jnp.maximum(m_old, c_ref[...])
            a_ref[...] = jnp.exp2(m_old - m_new)
        m8 = jnp.broadcast_to(m_new, (8, tq))
        done = []
        for b in range(tq // EXP_ROWS):
            lanes = slice(half if i == j and b * EXP_ROWS >= half else 0, tq)
            width = tq - lanes.start
            mb = m8[:, lanes]
            if b >= EXP_LAG:
                seen = jnp.concatenate([done[b - EXP_LAG]] * (width // LANE), axis=1)
                mb = jnp.where(seen <= 1.0, mb, 0.0)
            rows = pl.ds(b * EXP_ROWS, EXP_ROWS)
            x = s_ref[rows, lanes].reshape(EXP_ROWS // 8, 8, width) - mb[None]
            pb = jnp.exp2(x.reshape(EXP_ROWS, width).astype(BF16))
            p_ref[rows, lanes] = pb
            done.append(pb[0:BF16_ROWS, 0:LANE].astype(F32)[0:8])
        if j < i:
            m_scr[...] = m_new

    def accumulate(i, j, slot):
        p_ref, a_ref = pbufs[slot]
        vt1 = jnp.concatenate([vt_ref[0, j], ones], axis=0)
        if j == i:
            atop = jnp.dot(vt1[:, :half], p_ref[:half, :], preferred_element_type=F32)
            abot = jnp.dot(vt1[:, half:], p_ref[half:, half:], preferred_element_type=F32)
            acc = jnp.concatenate([atop[:, :half], atop[:, half:] + abot], axis=1)
        else:
            acc = jnp.dot(vt1, p_ref[...], preferred_element_type=F32)
        if j > 0:
            acc = a_ref[...] * acc_scr[...] + acc
        if j == i:
            rows = slice(i * tq, (i + 1) * tq)
            o = acc[:V_HEAD_DIM] / acc[V_HEAD_DIM:V_HEAD_DIM + 1]
            o_ref[0, rows, :] = (o.T * g_ref[0, rows, :]).astype(BF16)
        else:
            acc_scr[...] = acc

    scores(*tiles[0], 0)
    scores(*tiles[1], 1)
    exponentiate(*tiles[0], 0)
    for t, (i, j) in enumerate(tiles):
        if t + 2 < len(tiles):
            scores(*tiles[t + 2], t % 2)
        if t + 1 < len(tiles):
            exponentiate(*tiles[t + 1], (t + 1) % 2)
        accumulate(i, j, t % 2)


def _attn_call(qt, k, vt, gate, batch, seq):
    qt = qt.reshape(batch, seq // TQ, MLA_HEADS * QK_PAD, TQ)
    k = k.reshape(batch, seq, MLA_HEADS * QK_PAD)
    vt = vt.reshape(batch, seq // TQ, MLA_WIDTH, TQ)
    gate = gate.reshape(batch, seq, MLA_WIDTH)
    return pl.pallas_call(
        _attn_kernel,
        grid=(batch, MLA_HEADS),
        in_specs=[pl.BlockSpec((1, seq // TQ, QK_PAD, TQ), lambda b, h: (b, 0, h, 0)),
                  pl.BlockSpec((1, seq, QK_PAD), lambda b, h: (b, 0, h)),
                  pl.BlockSpec((1, seq // TQ, V_HEAD_DIM, TQ), lambda b, h: (b, 0, h, 0)),
                  pl.BlockSpec((1, seq, V_HEAD_DIM), lambda b, h: (b, 0, h))],
        out_specs=pl.BlockSpec((1, seq, V_HEAD_DIM), lambda b, h: (b, 0, h)),
        out_shape=jax.ShapeDtypeStruct((batch, seq, MLA_WIDTH), BF16),
        scratch_shapes=[pltpu.VMEM((1, TQ), F32),
                        pltpu.VMEM((V_HEAD_DIM + BF16_ROWS, TQ), F32),
                        pltpu.VMEM((TQ, TQ), F32), pltpu.VMEM((TQ, TQ), F32),
                        pltpu.VMEM((1, TQ), F32), pltpu.VMEM((1, TQ), F32),
                        pltpu.VMEM((TQ, TQ), BF16), pltpu.VMEM((TQ, TQ), BF16),
                        pltpu.VMEM((1, TQ), F32), pltpu.VMEM((1, TQ), F32)],
        compiler_params=pltpu.CompilerParams(
            dimension_semantics=("arbitrary", "arbitrary"), vmem_limit_bytes=VMEM_LIMIT),
        name="attn",
    )(qt, k, vt, gate)


def _out_kernel(h_ref, yc_ref, yr_ref, ym_ref, wo_ref, p_ref, wple_ref, png_ref, wpg_ref, fg_ref,
                o_ref, *, final):
    tm = h_ref.shape[0]
    blocks = [pl.ds(r, tm // OUT_ROW_BLOCKS) for r in range(0, tm, tm // OUT_ROW_BLOCKS)]
    ycat = [jnp.concatenate([yc_ref[b, :], yr_ref[b, :], ym_ref[b, :]], axis=-1) for b in blocks]
    h = [h_ref[b, :] + jnp.dot(y, wo_ref[...], preferred_element_type=F32) for b, y in zip(blocks, ycat)]
    hn = [_rms(x, png_ref[...]).astype(BF16) for x in h]
    gate = [jax.nn.sigmoid(jnp.dot(x, wpg_ref[...], preferred_element_type=F32)) for x in hn]
    ple = [jnp.dot(p_ref[b, :].astype(BF16), wple_ref[...], preferred_element_type=F32) for b in blocks]
    h = [x + e * g for x, e, g in zip(h, ple, gate)]
    if final:
        h = [_rms(x, fg_ref[...]) for x in h]
    for b, x in zip(blocks, h):
        o_ref[b, :] = x


def _out_call(h, yc, yr, ym, p, layer, lw, final_g, final):
    t = h.shape[0]
    row = lambda w: pl.BlockSpec((TM, w), lambda i: (i, 0))
    return pl.pallas_call(
        functools.partial(_out_kernel, final=final),
        grid=(t // TM,),
        in_specs=[row(D_MODEL), row(CONV_WIDTH), row(RWKV_WIDTH), row(MLA_WIDTH),
                  _layer_spec(lw["wo"], layer),
                  pl.BlockSpec((None, TM, PLE_DIM), lambda i: (layer, i, 0)),
                  _layer_spec(lw["wple"], layer), _layer_spec(lw["png"], layer),
                  _layer_spec(lw["wpg"], layer), _const_spec(final_g)],
        out_specs=row(D_MODEL),
        out_shape=jax.ShapeDtypeStruct((t, D_MODEL), F32),
        compiler_params=pltpu.CompilerParams(dimension_semantics=("arbitrary",),
                                             vmem_limit_bytes=VMEM_LIMIT),
        name="out",
    )(h, yc, yr, ym, lw["wo"], p, lw["wple"], lw["png"], lw["wpg"], final_g)


def _prep(norm_mix_g, w_in, conv_w, rwkv_mu, rwkv_w0, rwkv_w2, rwkv_a0, rwkv_a2, rwkv_kk, rwkv_ka, rwkv_rk,
          rwkv_gn_w, rwkv_gn_b, mla_q_norm_g, mla_w_qb, mla_kv_norm_g, mla_w_kvb, w_out, ple_w, ple_norm_g,
          ple_gate_w):
    depth = w_in.shape[0]
    o_kr = C_MG
    o_mg = o_kr + QK_ROPE_DIM
    zeros = jnp.zeros((depth, D_MODEL, QK_ROPE_DIM), F32)
    win_a = w_in.astype(BF16)
    win_b = jnp.concatenate([w_in[:, :, o_mg:], w_in[:, :, o_kr:o_mg], zeros], axis=2).astype(BF16)

    zl = jnp.zeros((depth, LORA, RWKV_WIDTH), F32)
    wl = jnp.concatenate([jnp.concatenate([rwkv_w2, zl], axis=2),
                          jnp.concatenate([zl, rwkv_a2], axis=2)], axis=1).astype(BF16)

    wqb = mla_w_qb.reshape(depth, Q_LORA_RANK, MLA_HEADS, QK_NOPE_DIM + QK_ROPE_DIM)
    zq = jnp.zeros((depth, Q_LORA_RANK, MLA_HEADS, QK_ROPE_DIM), F32)
    wq = jnp.concatenate([wqb, zq], axis=-1).reshape(depth, Q_LORA_RANK, MLA_HEADS * QK_PAD).astype(BF16)
    wkvb = mla_w_kvb.reshape(depth, KV_LORA_RANK, MLA_HEADS, QK_NOPE_DIM + V_HEAD_DIM)
    wkv = jnp.concatenate([wkvb[..., :QK_NOPE_DIM].reshape(depth, KV_LORA_RANK, -1),
                           wkvb[..., QK_NOPE_DIM:].reshape(depth, KV_LORA_RANK, -1)], axis=2).astype(BF16)
    row = lambda x: x.reshape(depth, 1, -1)
    hd = lambda x: jnp.pad(x.reshape(depth, RWKV_HEADS, RWKV_HEAD_DIM),
                           ((0, 0), (0, 0), (LANE - RWKV_HEAD_DIM, 0)))
    return dict(
        ng=row(norm_mix_g), win_a=win_a, win_b=win_b, cw=conv_w,
        mu_rkv=row(rwkv_mu[:, :3 * RWKV_WIDTH]), mu_wa=row(rwkv_mu[:, 3 * RWKV_WIDTH:]),
        w0=row(rwkv_w0), a0=row(rwkv_a0), wl=wl, kk=row(rwkv_kk), ka=row(rwkv_ka),
        qg=row(mla_q_norm_g), wq=wq, kvg=row(mla_kv_norm_g), wkv=wkv,
        rk=row(rwkv_rk), gnw=hd(rwkv_gn_w), gnb=hd(rwkv_gn_b),
        wo=w_out.astype(BF16), wple=ple_w.astype(BF16), png=row(ple_norm_g), wpg=ple_gate_w.astype(BF16))


def kernel(x, p, positions, norm_mix_g, w_in, conv_w, rwkv_mu, rwkv_w0, rwkv_w2, rwkv_a0, rwkv_a2,
           rwkv_kk, rwkv_ka, rwkv_rk, rwkv_gn_w, rwkv_gn_b, mla_q_norm_g, mla_w_qb, mla_kv_norm_g,
           mla_w_kvb, w_out, ple_w, ple_norm_g, ple_gate_w, final_norm_g):
    batch, seq, _ = x.shape
    depth = w_in.shape[0]
    t = batch * seq

    half = QK_ROPE_DIM // 2
    inv_freq = 1.0 / (ROPE_THETA ** (jnp.arange(0, QK_ROPE_DIM, 2, dtype=F32) / QK_ROPE_DIM))
    per_row = LANE // half
    ang = (jnp.repeat(positions.astype(F32).reshape(t // per_row, per_row), half, axis=1)
           * jnp.tile(inv_freq, per_row))
    cos, sin = lax.optimization_barrier((jnp.cos(ang), jnp.sin(ang)))
    cos, sin = cos.reshape(t, half), sin.reshape(t, half)
    zeros = jnp.zeros((t, LANE - QK_ROPE_DIM), F32)
    ck = jnp.concatenate([cos, cos, zeros], axis=1)
    sk = jnp.concatenate([-sin, sin, zeros], axis=1)

    h = x.reshape(t, D_MODEL)
    p = p.reshape(depth, t, PLE_DIM)
    final_g = final_norm_g.reshape(1, D_MODEL)
    lw = _prep(norm_mix_g, w_in, conv_w, rwkv_mu, rwkv_w0, rwkv_w2, rwkv_a0, rwkv_a2, rwkv_kk, rwkv_ka,
               rwkv_rk, rwkv_gn_w, rwkv_gn_b, mla_q_norm_g, mla_w_qb, mla_kv_norm_g, mla_w_kvb, w_out,
               ple_w, ple_norm_g, ple_gate_w)
    for i in range(depth):
        (yconv, *hm, wc, gate_r, bg, q, kq, vq, gate_m) = _proj_call(h, lw, i, ck, sk, batch, seq)
        y_rwkv = _rwkv_call(hm, wc, gate_r, bg, lw, i, batch, seq).reshape(t, RWKV_WIDTH)
        y_mla = _attn_call(q, kq, vq, gate_m, batch, seq).reshape(t, MLA_WIDTH)
        h = _out_call(h, yconv, y_rwkv, y_mla, p, i, lw, final_g, final=(i == depth - 1))
    return h.reshape(batch, seq, D_MODEL)
```
